```python
import math
import jax, jax.numpy as jnp
from jax import lax
import numpy as np

D_MODEL = 4096
BATCH = 1
SEQ = 8192
DEPTH = 2

D_MIX = D_MODEL
A_HEADS = 8
A_HDIM = 64
A_WIDTH = A_HEADS * 2 * A_HDIM
B_HDIM = 64
B_WIDTH = 2048
B_HEADS = B_WIDTH // B_HDIM
B_DECAY_LORA = 96
B_AAA_LORA = 96
B_MV_LORA = 64
B_GATE_LORA = 256
B_GN_EPS = 64e-5
C_HEADS = 8
C_HDIM = 128
C_WIDTH = C_HEADS * C_HDIM
IDX_HEADS = 16
IDX_HDIM = 64
TOPK_MAX = 256
D_FF = 4 * D_MODEL
Q_BLOCK = 128
EPS = 1e-6
SPLITS = (A_WIDTH, A_WIDTH, A_WIDTH,
          B_WIDTH, B_WIDTH, B_WIDTH,
          C_WIDTH, C_WIDTH, C_WIDTH,
          IDX_HEADS * IDX_HDIM, IDX_HDIM, IDX_HEADS)
D_IN = 3 * A_WIDTH + 3 * B_WIDTH + 3 * C_WIDTH + IDX_HEADS * IDX_HDIM + IDX_HDIM + IDX_HEADS

kernel_name = "hymba_diffattn_rwkv7_dsa_trunk"


def rms_norm(x, g, eps=EPS):
    xf = x.astype(jnp.float32)
    y = xf * lax.rsqrt(jnp.mean(xf * xf, axis=-1, keepdims=True) + eps)
    return (y * g.astype(jnp.float32)).astype(x.dtype)


def token_shift(t):
    return jnp.pad(t, ((0, 0), (1, 0), (0, 0)))[:, :-1]


def diff_attention(q, k, v, lam, lam_init, subln_g):
    B, S, _ = q.shape
    nb = S // Q_BLOCK
    scale = A_HDIM ** -0.5
    q = q.reshape(B, S, A_HEADS, 2, A_HDIM)
    k = k.reshape(B, S, A_HEADS, 2, A_HDIM)
    v = v.reshape(B, S, A_HEADS, 2 * A_HDIM)
    qb = q.reshape(B, nb, Q_BLOCK, A_HEADS, 2, A_HDIM).transpose(1, 0, 2, 3, 4, 5)
    kpos = jnp.arange(S)

    def block(args):
        qi, i = args
        qpos = i * Q_BLOCK + jnp.arange(Q_BLOCK)
        s = jnp.einsum('bqhmd,bkhmd->bhmqk', qi, k).astype(jnp.float32) * scale
        s = jnp.where(kpos[None, :] <= qpos[:, None], s, -jnp.inf)
        p = jax.nn.softmax(s, axis=-1)
        p = p[:, :, 0] - lam * p[:, :, 1]
        return jnp.einsum('bhqk,bkhe->bqhe', p.astype(v.dtype), v)

    o = lax.map(block, (qb, jnp.arange(nb)))
    o = o.transpose(1, 0, 2, 3, 4).reshape(B, S, A_HEADS, 2 * A_HDIM)
    o = rms_norm(o, subln_g) * (1.0 - lam_init)
    return o.reshape(B, S, A_WIDTH)


def wkv7_scan(r, w, k, v, a, b):
    Bsz, S, H, N = r.shape

    def step(state, inp):
        r_t, w_t, k_t, v_t, a_t, b_t = inp
        sa = jnp.einsum('bhvk,bhk->bhv', state, a_t)
        state = (state * w_t[:, :, None, :]
                 + sa[..., None] * b_t[:, :, None, :]
                 + v_t[..., None] * k_t[:, :, None, :])
        y = jnp.einsum('bhvk,bhk->bhv', state, r_t)
        return state, y

    xs = (r.transpose(1, 0, 2, 3), w.transpose(1, 0, 2, 3), k.transpose(1, 0, 2, 3),
          v.transpose(1, 0, 2, 3), a.transpose(1, 0, 2, 3), b.transpose(1, 0, 2, 3))
    s0 = jnp.zeros((Bsz, H, N, N), jnp.float32)
    _, y = lax.scan(step, s0, xs)
    return y.transpose(1, 0, 2, 3)


def rwkv7_mixer(h, r, k, v, mu_rkv, mu_wag, w0, w1, w2, a0, a1, a2, g1, g2,
                k_k, k_a, r_k, ln_w, ln_b, v_res):
    B, S, _ = h.shape
    f32 = jnp.float32
    r = r + (token_shift(r) - r) * mu_rkv[0]
    k = k + (token_shift(k) - k) * mu_rkv[1]
    v = v + (token_shift(v) - v) * mu_rkv[2]
    dh = token_shift(h) - h
    xw = h + dh * mu_wag[0]
    xa = h + dh * mu_wag[1]
    xg = h + dh * mu_wag[2]
    wlog = -jax.nn.softplus(-(w0 + jnp.tanh(xw @ w1) @ w2).astype(f32)) - 0.5
    decay = jnp.exp(-jnp.exp(wlog))
    a = jax.nn.sigmoid((a0 + (xa @ a1) @ a2).astype(f32))
    g = jax.nn.sigmoid(xg @ g1) @ g2
    if v_res is not None:
        v_first, v_mu, v0, v1, v2 = v_res
        xv = h + dh * v_mu
        v = v + (v_first - v) * jax.nn.sigmoid(v0 + (xv @ v1) @ v2)
    v_out = v

    def heads(t):
        return t.reshape(B, S, B_HEADS, B_HDIM).astype(f32)

    kk = heads(k * k_k)
    kk = kk / jnp.maximum(jnp.sqrt(jnp.sum(kk * kk, axis=-1, keepdims=True)), 1e-12)
    k_new = k.astype(f32) * (1.0 + (a - 1.0) * k_a.astype(f32))
    rh, kh, vh, ah = heads(r), heads(k_new), heads(v), heads(a)
    y = wkv7_scan(rh, heads(decay), kh, vh, -kk, kk * ah)
    mean = jnp.mean(y, axis=-1, keepdims=True)
    var = jnp.mean(jnp.square(y - mean), axis=-1, keepdims=True)
    yn = ((y - mean) * lax.rsqrt(var + B_GN_EPS)).reshape(B, S, B_WIDTH)
    yn = yn * ln_w.astype(f32) + ln_b.astype(f32)
    bonus = jnp.sum(rh * kh * r_k.astype(f32), axis=-1, keepdims=True) * vh
    out = (yn + bonus.reshape(B, S, B_WIDTH)) * g.astype(f32)
    return out.astype(h.dtype), v_out


def dsa_mixer(q, k, v, qi, ki, wi):
    B, S, _ = q.shape
    nb = S // Q_BLOCK
    topk = min(TOPK_MAX, S // 4)
    scale = C_HDIM ** -0.5
    q = q.reshape(B, S, C_HEADS, C_HDIM)
    k = k.reshape(B, S, C_HEADS, C_HDIM)
    v = v.reshape(B, S, C_HEADS, C_HDIM)
    qi = qi.reshape(B, S, IDX_HEADS, IDX_HDIM)
    wi = wi * (IDX_HEADS ** -0.5 * IDX_HDIM ** -0.5)
    qb = q.reshape(B, nb, Q_BLOCK, C_HEADS, C_HDIM).transpose(1, 0, 2, 3, 4)
    qib = qi.reshape(B, nb, Q_BLOCK, IDX_HEADS, IDX_HDIM).transpose(1, 0, 2, 3, 4)
    wib = wi.reshape(B, nb, Q_BLOCK, IDX_HEADS).transpose(1, 0, 2, 3)
    kpos = jnp.arange(S)

    def block(args):
        qc, qic, wic, i = args
        qpos = i * Q_BLOCK + jnp.arange(Q_BLOCK)
        dots = jnp.einsum('bqhd,bkd->bqhk', qic, ki).astype(jnp.float32)
        score = jnp.einsum('bqh,bqhk->bqk', wic.astype(jnp.float32), jax.nn.relu(dots))
        score = jnp.where(kpos[None, None, :] <= qpos[None, :, None], score, -jnp.inf)
        _, idx = lax.top_k(score, topk)
        kg = jax.vmap(lambda kb, ib: kb[ib])(k, idx)
        vg = jax.vmap(lambda vb, ib: vb[ib])(v, idx)
        valid = idx <= qpos[None, :, None]
        s = jnp.einsum('bqhd,bqkhd->bqhk', qc, kg).astype(jnp.float32) * scale
        s = jnp.where(valid[:, :, None, :], s, -jnp.inf)
        p = jax.nn.softmax(s, axis=-1)
        return jnp.einsum('bqhk,bqkhd->bqhd', p.astype(vg.dtype), vg)

    o = lax.map(block, (qb, qib, wib, jnp.arange(nb)))
    return o.transpose(1, 0, 2, 3, 4).reshape(B, S, C_WIDTH)


def setup_inputs(seed: int = 0) -> dict:
    key = jax.random.key(seed)
    keys = jax.random.split(key, 40)
    counter = [0]

    def nk():
        kk = keys[counter[0]]
        counter[0] += 1
        return kk

    def nrm(shape, scale):
        return scale * jax.random.normal(nk(), shape, jnp.float32)

    def unif(shape, lo, hi):
        return jax.random.uniform(nk(), shape, jnp.float32, lo, hi)

    L = DEPTH
    Lv = DEPTH - 1
    return {
        "x": nrm((BATCH, SEQ, D_MODEL), 1.0),
        "norm_mix_g": 1.0 + nrm((L, D_MODEL), 0.02),
        "w_in": nrm((L, D_MODEL, D_IN), D_MODEL ** -0.5),
        "lam_q1": nrm((L, A_HDIM), 0.1),
        "lam_k1": nrm((L, A_HDIM), 0.1),
        "lam_q2": nrm((L, A_HDIM), 0.1),
        "lam_k2": nrm((L, A_HDIM), 0.1),
        "diff_subln_g": 1.0 + nrm((L, 2 * A_HDIM), 0.02),
        "rw_mu_rkv": unif((L, 3, B_WIDTH), 0.0, 1.0),
        "rw_mu_wag": unif((L, 3, D_MODEL), 0.0, 1.0),
        "rw_w0": unif((L, B_WIDTH), -4.0, 1.0),
        "rw_w1": nrm((L, D_MODEL, B_DECAY_LORA), D_MODEL ** -0.5),
        "rw_w2": nrm((L, B_DECAY_LORA, B_WIDTH), 0.1 * B_DECAY_LORA ** -0.5),
        "rw_a0": nrm((L, B_WIDTH), 0.1),
        "rw_a1": nrm((L, D_MODEL, B_AAA_LORA), D_MODEL ** -0.5),
        "rw_a2": nrm((L, B_AAA_LORA, B_WIDTH), 0.5 * B_AAA_LORA ** -0.5),
        "rw_g1": nrm((L, D_MODEL, B_GATE_LORA), D_MODEL ** -0.5),
        "rw_g2": nrm((L, B_GATE_LORA, B_WIDTH), B_GATE_LORA ** -0.5),
        "rw_k_k": 0.85 + nrm((L, B_WIDTH), 0.05),
        "rw_k_a": 1.0 + nrm((L, B_WIDTH), 0.05),
        "rw_r_k": nrm((L, B_HEADS, B_HDIM), 0.1),
        "rw_ln_w": 1.0 + nrm((L, B_WIDTH), 0.02),
        "rw_ln_b": nrm((L, B_WIDTH), 0.02),
        "rw_v_mu": unif((Lv, D_MODEL), 0.0, 1.0),
        "rw_v0": 0.5 + nrm((Lv, B_WIDTH), 0.1),
        "rw_v1": nrm((Lv, D_MODEL, B_MV_LORA), D_MODEL ** -0.5),
        "rw_v2": nrm((Lv, B_MV_LORA, B_WIDTH), 0.5 * B_MV_LORA ** -0.5),
        "w_out": nrm((L, D_MIX, D_MODEL), D_MIX ** -0.5),
        "norm_ffn_g": 1.0 + nrm((L, D_MODEL), 0.02),
        "w_up": nrm((L, D_MODEL, D_FF), D_MODEL ** -0.5),
        "w_down": nrm((L, D_FF, D_MODEL), D_FF ** -0.5),
        "norm_final_g": 1.0 + nrm((D_MODEL,), 0.02),
    }


def reference(x, norm_mix_g, w_in, lam_q1, lam_k1, lam_q2, lam_k2, diff_subln_g,
              rw_mu_rkv, rw_mu_wag, rw_w0, rw_w1, rw_w2, rw_a0, rw_a1, rw_a2,
              rw_g1, rw_g2, rw_k_k, rw_k_a, rw_r_k, rw_ln_w, rw_ln_b,
              rw_v_mu, rw_v0, rw_v1, rw_v2, w_out, norm_ffn_g, w_up, w_down,
              norm_final_g):
    cuts = []
    acc = 0
    for s in SPLITS[:-1]:
        acc += s
        cuts.append(acc)
    v_first = None
    for l in range(DEPTH):
        h = rms_norm(x, norm_mix_g[l])
        proj = h @ w_in[l]
        (qa, ka, va, rb, kb, vb, qc, kc, vc, qi, ki, wi) = jnp.split(proj, cuts, axis=-1)
        lam_init = 0.8 - 0.6 * math.exp(-0.3 * l)
        lam = (jnp.exp(jnp.sum(lam_q1[l] * lam_k1[l]).astype(jnp.float32))
               - jnp.exp(jnp.sum(lam_q2[l] * lam_k2[l]).astype(jnp.float32)) + lam_init)
        o_a = diff_attention(qa, ka, va, lam, lam_init, diff_subln_g[l])
        v_res = None if l == 0 else (v_first, rw_v_mu[l - 1], rw_v0[l - 1], rw_v1[l - 1], rw_v2[l - 1])
        o_b, v_out = rwkv7_mixer(h, rb, kb, vb, rw_mu_rkv[l], rw_mu_wag[l], rw_w0[l], rw_w1[l], rw_w2[l],
                                 rw_a0[l], rw_a1[l], rw_a2[l], rw_g1[l], rw_g2[l], rw_k_k[l], rw_k_a[l],
                                 rw_r_k[l], rw_ln_w[l], rw_ln_b[l], v_res)
        if l == 0:
            v_first = v_out
        o_c = dsa_mixer(qc, kc, vc, qi, ki, wi)
        mixed = jnp.concatenate([o_a, o_b.astype(o_a.dtype), o_c], axis=-1)
        x = x + mixed @ w_out[l]
        h2 = rms_norm(x, norm_ffn_g[l])
        x = x + jnp.square(jax.nn.relu(h2 @ w_up[l])) @ w_down[l]
    return rms_norm(x, norm_final_g)
```

```python
import functools
import math

import jax
import jax.numpy as jnp
from jax import lax
from jax.experimental import pallas as pl
from jax.experimental.pallas import tpu as pltpu

F32 = jnp.float32
BF16 = jnp.bfloat16

A_HEADS, A_HDIM = 8, 64
A_WIDTH = A_HEADS * 2 * A_HDIM
B_HDIM, B_WIDTH = 64, 2048
B_HEADS = B_WIDTH // B_HDIM
B_GN_EPS = 64e-5
C_HEADS, C_HDIM = 8, 128
C_WIDTH = C_HEADS * C_HDIM
IDX_HEADS, IDX_HDIM = 16, 64
TOPK_MAX = 256
EPS = 1e-6

LANES = 128
VMEM_LIMIT = 56 * 1024 * 1024
NEG = -1e30
WKV_CHUNK = 64
HI = lax.Precision.HIGHEST


def _cparams(sem):
    return pltpu.CompilerParams(dimension_semantics=sem, vmem_limit_bytes=VMEM_LIMIT)


def _rmsnorm_body(x_ref, g_ref, o_ref):
    x = x_ref[...]
    ms = jnp.mean(x * x, axis=-1, keepdims=True)
    o_ref[...] = (x * lax.rsqrt(ms + EPS) * g_ref[...]).astype(o_ref.dtype)


def rmsnorm(x, g, out_dtype, tm=256):
    S, D = x.shape
    tm = min(tm, S)
    return pl.pallas_call(
        _rmsnorm_body,
        grid=(S // tm,),
        in_specs=[pl.BlockSpec((tm, D), lambda i: (i, 0)),
                  pl.BlockSpec((1, D), lambda i: (0, 0))],
        out_specs=pl.BlockSpec((tm, D), lambda i: (i, 0)),
        out_shape=jax.ShapeDtypeStruct((S, D), out_dtype),
        compiler_params=_cparams(("parallel",)),
    )(x, g.reshape(1, D))


def _mm_body(a_ref, b_ref, *rest, nk, epilogue):
    if epilogue == "residual":
        res_ref, o_ref, acc_ref = rest
    else:
        o_ref, acc_ref = rest
    k = pl.program_id(2)

    @pl.when(k == 0)
    def _():
        acc_ref[...] = jnp.zeros_like(acc_ref)

    acc_ref[...] += jnp.dot(a_ref[...], b_ref[...], preferred_element_type=F32)

    @pl.when(k == nk - 1)
    def _():
        acc = acc_ref[...]
        if epilogue == "relu2":
            r = jnp.maximum(acc, 0.0)
            acc = r * r
        elif epilogue == "residual":
            acc = acc + res_ref[...]
        o_ref[...] = acc.astype(o_ref.dtype)


def _pick(n, pref):
    for t in pref:
        if n % t == 0:
            return t
    return n


def matmul(a, b, out_dtype, epilogue="none", residual=None):
    M, K = a.shape
    _, N = b.shape
    tm = _pick(M, (1024, 512, 256))
    tn = _pick(N, (1024, 768, 512, 384, 256, 128))
    tk = _pick(K, (1024, 512))
    nk = K // tk
    in_specs = [pl.BlockSpec((tm, tk), lambda i, j, k: (i, k)),
                pl.BlockSpec((tk, tn), lambda i, j, k: (k, j))]
    args = [a, b]
    if epilogue == "residual":
        in_specs.append(pl.BlockSpec((tm, tn), lambda i, j, k: (i, j)))
        args.append(residual)
    return pl.pallas_call(
        functools.partial(_mm_body, nk=nk, epilogue=epilogue),
        grid=(M // tm, N // tn, nk),
        in_specs=in_specs,
        out_specs=pl.BlockSpec((tm, tn), lambda i, j, k: (i, j)),
        out_shape=jax.ShapeDtypeStruct((M, N), out_dtype),
        scratch_shapes=[pltpu.VMEM((tm, tn), F32)],
        compiler_params=_cparams(("parallel", "parallel", "arbitrary")),
    )(*args)


def _diffattn_body(q_ref, k_ref, v_ref, lam_ref, g_ref, o_ref,
                   m1, l1, acc1, m2, l2, acc2, *, tq, tk, lam_init):
    i = pl.program_id(1)
    j = pl.program_id(2)
    nj = pl.num_programs(2)
    scale = A_HDIM ** -0.5

    @pl.when(j == 0)
    def _():
        m1[...] = jnp.full_like(m1, NEG)
        m2[...] = jnp.full_like(m2, NEG)
        l1[...] = jnp.zeros_like(l1)
        l2[...] = jnp.zeros_like(l2)
        acc1[...] = jnp.zeros_like(acc1)
        acc2[...] = jnp.zeros_like(acc2)

    @pl.when(j * tk <= i * tq + tq - 1)
    def _():
        q = q_ref[...]
        k = k_ref[...]
        v = v_ref[...]
        qpos = i * tq + lax.broadcasted_iota(jnp.int32, (tq, tk), 0)
        kpos = j * tk + lax.broadcasted_iota(jnp.int32, (tq, tk), 1)
        causal = kpos <= qpos
        for (lo, m_ref, l_ref, acc_ref) in ((0, m1, l1, acc1), (A_HDIM, m2, l2, acc2)):
            s = lax.dot_general(q[:, lo:lo + A_HDIM], k[:, lo:lo + A_HDIM],
                                (((1,), (1,)), ((), ())), preferred_element_type=F32) * scale
            s = jnp.where(causal, s, NEG)
            m_old = m_ref[...]
            m_new = jnp.maximum(m_old, jnp.max(s, axis=-1, keepdims=True))
            alpha = jnp.exp(m_old - m_new)
            p = jnp.exp(s - m_new)
            l_ref[...] = alpha * l_ref[...] + jnp.sum(p, axis=-1, keepdims=True)
            acc_ref[...] = alpha * acc_ref[...] + jnp.dot(p.astype(BF16), v, preferred_element_type=F32)
            m_ref[...] = m_new

    @pl.when(j == nj - 1)
    def _():
        lv = lam_ref[...]
        lam = (jnp.exp(jnp.sum(lv[0:1] * lv[1:2], axis=-1, keepdims=True))
               - jnp.exp(jnp.sum(lv[2:3] * lv[3:4], axis=-1, keepdims=True)) + lam_init)
        o = acc1[...] / l1[...] - lam * (acc2[...] / l2[...])
        ms = jnp.mean(o * o, axis=-1, keepdims=True)
        o = o * lax.rsqrt(ms + EPS) * g_ref[...] * (1.0 - lam_init)
        o_ref[...] = o.astype(o_ref.dtype)


def diff_attention(qkv, lam_vecs, subln_g, lam_init, t=512):
    S = qkv.shape[0]
    t = min(t, S)
    n = S // t
    hw = 2 * A_HDIM
    kv_map = lambda off: (lambda h, i, j: (jnp.minimum(j, i), off + h))
    return pl.pallas_call(
        functools.partial(_diffattn_body, tq=t, tk=t, lam_init=lam_init),
        grid=(A_HEADS, n, n),
        in_specs=[pl.BlockSpec((t, hw), lambda h, i, j: (i, h)),
                  pl.BlockSpec((t, hw), kv_map(A_HEADS)),
                  pl.BlockSpec((t, hw), kv_map(2 * A_HEADS)),
                  pl.BlockSpec((4, A_HDIM), lambda h, i, j: (0, 0)),
                  pl.BlockSpec((1, hw), lambda h, i, j: (0, 0))],
        out_specs=pl.BlockSpec((t, hw), lambda h, i, j: (i, h)),
        out_shape=jax.ShapeDtypeStruct((S, A_WIDTH), BF16),
        scratch_shapes=[pltpu.VMEM((t, 1), F32), pltpu.VMEM((t, 1), F32), pltpu.VMEM((t, hw), F32),
                        pltpu.VMEM((t, 1), F32), pltpu.VMEM((t, 1), F32), pltpu.VMEM((t, hw), F32)],
        compiler_params=_cparams(("parallel", "parallel", "arbitrary")),
    )(qkv, qkv, qkv, lam_vecs, subln_g.reshape(1, hw))


KEY_NEG_INF = -2139095041


def _float_key(s):
    b = pltpu.bitcast(s, jnp.int32)
    return b ^ ((b >> 31) & jnp.int32(0x7FFFFFFF))


def _dsa_index_body(ki_ref, qi_ref, wiT_ref, bias_ref, key_ref, *, qb, kc, topk, nkc_total):
    i = pl.program_id(0)
    nch = ((i + 1) * qb) // kc
    wi = wiT_ref[...] * (IDX_HEADS ** -0.5 * IDX_HDIM ** -0.5)
    qi = qi_ref[...]
    tpos = i * qb + lax.broadcasted_iota(jnp.int32, (kc, qb), 1)
    srow = lax.broadcasted_iota(jnp.int32, (kc, qb), 0)

    def score_chunk(c, carry):
        r0 = pl.multiple_of(c * kc, kc)
        kic = ki_ref[pl.ds(r0, kc), :]
        acc = jnp.zeros((kc, qb), F32)
        for h in range(IDX_HEADS):
            d = lax.dot_general(kic, qi[:, h * IDX_HDIM:(h + 1) * IDX_HDIM],
                                (((1,), (1,)), ((), ())), preferred_element_type=F32)
            acc = acc + jnp.maximum(d, 0.0) * wi[h:h + 1, :]
        acc = jnp.where(r0 + srow <= tpos, acc, -jnp.inf)
        key_ref[pl.ds(r0, kc), :] = _float_key(acc)
        return carry

    lax.fori_loop(0, nch, score_chunk, 0)

    def count_ge(cand):
        def body(c, cnt):
            r0 = pl.multiple_of(c * kc, kc)
            ge = (key_ref[pl.ds(r0, kc), :] >= cand).astype(jnp.int32)
            return cnt + jnp.sum(ge, axis=0, keepdims=True)
        return lax.fori_loop(0, nch, body, jnp.zeros((1, qb), jnp.int32))

    def bit_step(it, tau):
        cand = tau + (jnp.int32(1) << (31 - it))
        return jnp.where(count_ge(cand) >= topk, cand, tau)

    tau = lax.fori_loop(0, 32, bit_step, jnp.full((1, qb), jnp.iinfo(jnp.int32).min, jnp.int32))
    tau = jnp.maximum(tau, KEY_NEG_INF + 1)

    def write_chunk(c, carry):
        r0 = pl.multiple_of(c * kc, kc)
        sel = key_ref[pl.ds(r0, kc), :] >= tau
        bias_ref[pl.ds(r0, kc), :] = jnp.where(sel, 0.0, NEG).astype(bias_ref.dtype)
        return carry

    lax.fori_loop(0, nch, write_chunk, 0)

    def fill_chunk(c, carry):
        r0 = pl.multiple_of(c * kc, kc)
        bias_ref[pl.ds(r0, kc), :] = jnp.full((kc, qb), NEG, bias_ref.dtype)
        return carry

    lax.fori_loop(nch, nkc_total, fill_chunk, 0)


def dsa_index(ki, qi, wiT, topk, qb=256, kc=256):
    S = ki.shape[0]
    qb = min(qb, S)
    kc = min(kc, qb)
    return pl.pallas_call(
        functools.partial(_dsa_index_body, qb=qb, kc=kc, topk=topk, nkc_total=S // kc),
        grid=(S // qb,),
        in_specs=[pl.BlockSpec((S, IDX_HDIM), lambda i: (0, 0)),
                  pl.BlockSpec((qb, IDX_HEADS * IDX_HDIM), lambda i: (i, 0)),
                  pl.BlockSpec((IDX_HEADS, qb), lambda i: (0, i))],
        out_specs=pl.BlockSpec((S, qb), lambda i: (0, i)),
        out_shape=jax.ShapeDtypeStruct((S, S), BF16),
        scratch_shapes=[pltpu.VMEM((S, qb), jnp.int32)],
        compiler_params=_cparams(("parallel",)),
    )(ki, qi, wiT)


def _dsa_attn_body(q_ref, k_ref, vT_ref, bias_ref, oT_ref, m_ref, l_ref, acc_ref, *, qb, kc):
    i = pl.program_id(0)
    j = pl.program_id(1)
    nj = pl.num_programs(1)
    scale = C_HDIM ** -0.5

    @pl.when(j == 0)
    def _():
        m_ref[...] = jnp.full_like(m_ref, NEG)
        l_ref[...] = jnp.zeros_like(l_ref)
        acc_ref[...] = jnp.zeros_like(acc_ref)

    @pl.when(j * kc <= i * qb + qb - 1)
    def _():
        bias = bias_ref[...].astype(F32)
        for h in range(C_HEADS):
            cs = slice(h * C_HDIM, (h + 1) * C_HDIM)
            sT = lax.dot_general(k_ref[:, cs], q_ref[:, cs], (((1,), (1,)), ((), ())),
                                 preferred_element_type=F32) * scale + bias
            m_old = m_ref[h:h + 1, :]
            m_new = jnp.maximum(m_old, jnp.max(sT, axis=0, keepdims=True))
            alpha = jnp.exp(m_old - m_new)
            p = jnp.exp(sT - m_new)
            l_ref[h:h + 1, :] = alpha * l_ref[h:h + 1, :] + jnp.sum(p, axis=0, keepdims=True)
            acc_ref[cs, :] = alpha * acc_ref[cs, :] + jnp.dot(vT_ref[cs, :], p.astype(BF16),
                                                               preferred_element_type=F32)
            m_ref[h:h + 1, :] = m_new

    @pl.when(j == nj - 1)
    def _():
        for h in range(C_HEADS):
            cs = slice(h * C_HDIM, (h + 1) * C_HDIM)
            oT_ref[cs, :] = (acc_ref[cs, :] / l_ref[h:h + 1, :]).astype(oT_ref.dtype)


def dsa_attention(qkv, vT, bias, qb=256, kc=256):
    S = qkv.shape[0]
    qb = min(qb, S)
    kc = min(kc, qb)
    last = lambda i: ((i + 1) * qb - 1) // kc
    return pl.pallas_call(
        functools.partial(_dsa_attn_body, qb=qb, kc=kc),
        grid=(S // qb, S // kc),
        in_specs=[pl.BlockSpec((qb, C_WIDTH), lambda i, j: (i, 0)),
                  pl.BlockSpec((kc, C_WIDTH), lambda i, j: (jnp.minimum(j, last(i)), 1)),
                  pl.BlockSpec((C_WIDTH, kc), lambda i, j: (0, jnp.minimum(j, last(i)))),
                  pl.BlockSpec((kc, qb), lambda i, j: (jnp.minimum(j, last(i)), i))],
        out_specs=pl.BlockSpec((C_WIDTH, qb), lambda i, j: (0, i)),
        out_shape=jax.ShapeDtypeStruct((C_WIDTH, S), BF16),
        scratch_shapes=[pltpu.VMEM((C_HEADS, qb), F32), pltpu.VMEM((C_HEADS, qb), F32),
                        pltpu.VMEM((C_WIDTH, qb), F32)],
        compiler_params=_cparams(("parallel", "arbitrary")),
    )(qkv, qkv, vT, bias)


def _sigmoid(x):
    return 1.0 / (1.0 + jnp.exp(-x))


def _shift_rows(cur, prev_ref, first_tile):
    prev_row = jnp.where(first_tile, 0.0, prev_ref[7:8, :].astype(F32))
    rolled = pltpu.roll(cur, 1, axis=0)
    row = lax.broadcasted_iota(jnp.int32, cur.shape, 0)
    return jnp.where(row == 0, prev_row, rolled)


def _rwkv_lora_body(*refs, has_v):
    if has_v:
        (h_ref, hp_ref, mu_ref, w1_ref, a1_ref, g1_ref, v1_ref, ow_ref, oa_ref, og_ref, ov_ref) = refs
    else:
        (h_ref, hp_ref, mu_ref, w1_ref, a1_ref, g1_ref, ow_ref, oa_ref, og_ref) = refs
    i = pl.program_id(0)
    h = h_ref[...].astype(F32)
    dh = _shift_rows(h, hp_ref, i == 0) - h

    def lora(row, w_ref):
        xm = (h + dh * mu_ref[row:row + 1, :]).astype(BF16)
        return jnp.dot(xm, w_ref[...], preferred_element_type=F32)

    ow_ref[...] = jnp.tanh(lora(0, w1_ref))
    oa_ref[...] = lora(1, a1_ref)
    og_ref[...] = _sigmoid(lora(2, g1_ref))
    if has_v:
        ov_ref[...] = lora(3, v1_ref)


def rwkv_lora(h, mu, w1, a1, g1, v1, tm=256):
    S, D = h.shape
    tm = min(tm, S)
    has_v = v1 is not None
    ws = [w1, a1, g1] + ([v1] if has_v else [])
    full = lambda a: pl.BlockSpec(a.shape, lambda i: (0, 0))
    return pl.pallas_call(
        functools.partial(_rwkv_lora_body, has_v=has_v),
        grid=(S // tm,),
        in_specs=[pl.BlockSpec((tm, D), lambda i: (i, 0)),
                  pl.BlockSpec((8, D), lambda i: (jnp.maximum(i * (tm // 8) - 1, 0), 0)),
                  full(mu)] + [full(w) for w in ws],
        out_specs=[pl.BlockSpec((tm, w.shape[1]), lambda i: (i, 0)) for w in ws],
        out_shape=[jax.ShapeDtypeStruct((S, w.shape[1]), F32) for w in ws],
        compiler_params=_cparams(("parallel",)),
    )(h, h, mu, *ws)


(V_MU_R, V_MU_K, V_MU_V, V_W0, V_A0, V_KK, V_KA, V_RK, V_V0) = range(9)
N_VEC_ROWS = 16


def _wkv_pre_body(*refs, has_v, tt):
    if has_v:
        (r_ref, k_ref, v_ref, rp_ref, kp_ref, vp_ref, hw_ref, ha_ref, hg_ref, w2_ref, a2_ref, g2_ref,
         vec_ref, hv_ref, v2_ref, vf_ref,
         m_out, g_out, q_out, z_out, vout_ref, gate_ref, bonus_ref) = refs
    else:
        (r_ref, k_ref, v_ref, rp_ref, kp_ref, vp_ref, hw_ref, ha_ref, hg_ref, w2_ref, a2_ref, g2_ref,
         vec_ref,
         m_out, g_out, q_out, z_out, vout_ref, gate_ref, bonus_ref) = refs
    i = pl.program_id(0)
    first = i == 0
    C = WKV_CHUNK
    N = B_HDIM
    vec = lambda row: vec_ref[row:row + 1, :]

    r = r_ref[...]
    k = k_ref[...]
    v = v_ref[...]
    r = r + (_shift_rows(r, rp_ref, first) - r) * vec(V_MU_R)
    k = k + (_shift_rows(k, kp_ref, first) - k) * vec(V_MU_K)
    v = v + (_shift_rows(v, vp_ref, first) - v) * vec(V_MU_V)

    wl = vec(V_W0) + jnp.dot(hw_ref[...], w2_ref[...], preferred_element_type=F32, precision=HI)
    z = -wl
    softplus = jnp.maximum(z, 0.0) + jnp.log(1.0 + jnp.exp(-jnp.abs(z)))
    logw = -jnp.exp(-softplus - 0.5)
    a_sig = _sigmoid(vec(V_A0) + jnp.dot(ha_ref[...], a2_ref[...], preferred_element_type=F32, precision=HI))
    gate = jnp.dot(hg_ref[...], g2_ref[...], preferred_element_type=F32, precision=HI)
    if has_v:
        mix = _sigmoid(vec(V_V0) + jnp.dot(hv_ref[...], v2_ref[...], preferred_element_type=F32, precision=HI))
        v = v + (vf_ref[...] - v) * mix
    vout_ref[...] = v
    gate_ref[...] = gate

    kk = k * vec(V_KK)
    k_new = k * (1.0 + (a_sig - 1.0) * vec(V_KA))
    rk = r * k_new * vec(V_RK)

    rowc = lax.broadcasted_iota(jnp.int32, (C, C), 0)
    colc = lax.broadcasted_iota(jnp.int32, (C, C), 1)
    tril_incl = (rowc >= colc).astype(F32)
    row2 = lax.broadcasted_iota(jnp.int32, (2 * C, 2 * C), 0)
    col2 = lax.broadcasted_iota(jnp.int32, (2 * C, 2 * C), 1)
    rr = jnp.where(row2 >= C, row2 - C, row2)
    cc = jnp.where(col2 >= C, col2 - C, col2)
    keep = (rr - cc) >= jnp.where(row2 >= C, 0, 1)
    eye_n = (lax.broadcasted_iota(jnp.int32, (N, N), 0) == lax.broadcasted_iota(jnp.int32, (N, N), 1)).astype(F32)

    bonus_parts = []
    per_head = []
    for hh in range(LANES // N):
        hs = slice(hh * N, (hh + 1) * N)
        kkh = kk[:, hs]
        nrm = jnp.sqrt(jnp.sum(kkh * kkh, axis=-1, keepdims=True))
        kkn = kkh / jnp.maximum(nrm, 1e-12)
        bonus_parts.append(jnp.sum(rk[:, hs], axis=-1, keepdims=True) * v[:, hs])
        per_head.append((r[:, hs], k_new[:, hs], v[:, hs], -kkn, kkn * a_sig[:, hs]))
    bonus_ref[...] = jnp.concatenate(bonus_parts, axis=-1)

    for c in range(tt // C):
        ts = slice(c * C, (c + 1) * C)
        cum = jnp.dot(tril_incl, logw[ts], preferred_element_type=F32, precision=HI)
        cum_last = cum[C - 1:C, :]
        e_in = jnp.exp(cum)
        e_ex = jnp.exp(cum - logw[ts])
        e_neg = jnp.exp(-cum)
        e_end = jnp.exp(cum_last - cum)
        gam = jnp.exp(cum_last)
        m_parts, g_parts, q_parts, z_parts = [], [], [], []
        for hh in range(LANES // N):
            hs = slice(hh * N, (hh + 1) * N)
            rh, kh, vh, ah, bh = (t_[ts] for t_ in per_head[hh])
            At = ah * e_ex[:, hs]
            Rt = rh * e_in[:, hs]
            Bt = bh * e_neg[:, hs]
            Kt = kh * e_neg[:, hs]
            Bg = bh * e_end[:, hs]
            Kg = kh * e_end[:, hs]
            left = jnp.concatenate([At, Rt], axis=0).astype(BF16)
            right = jnp.concatenate([Bt, Kt], axis=0).astype(BF16)
            AA = lax.dot_general(left, right, (((1,), (1,)), ((), ())), preferred_element_type=F32)
            AA = jnp.where(keep, AA, 0.0)
            A_ab = AA[:C, :C]
            A_ak = AA[:C, C:]
            vb = vh.astype(BF16)
            akv = jnp.dot(A_ak.astype(BF16), vb, preferred_element_type=F32)
            X = jnp.concatenate([At, akv], axis=-1)
            Ap = A_ab
            for step in range(6):
                Apb = Ap.astype(BF16)
                X = X + jnp.dot(Apb, X.astype(BF16), preferred_element_type=F32)
                if step < 5:
                    Ap = jnp.dot(Apb, Apb, preferred_element_type=F32)
            lower = jnp.concatenate([jnp.zeros((C, N), F32), vh], axis=-1)
            W2 = jnp.concatenate([X, lower], axis=0).astype(BF16)
            bk = jnp.concatenate([Bg, Kg], axis=0).astype(BF16)
            MG = lax.dot_general(bk, W2, (((0,), (0,)), ((), ())), preferred_element_type=F32)
            QZ = jnp.dot(AA[C:, :].astype(BF16), W2, preferred_element_type=F32)
            m_parts.append(eye_n * gam[:, hs] + MG[:, :N])
            g_parts.append(MG[:, N:])
            q_parts.append(Rt + QZ[:, :N])
            z_parts.append(QZ[:, N:])
        m_out[c * N:(c + 1) * N, :] = jnp.concatenate(m_parts, axis=-1)
        g_out[c * N:(c + 1) * N, :] = jnp.concatenate(g_parts, axis=-1)
        q_out[ts, :] = jnp.concatenate(q_parts, axis=-1)
        z_out[ts, :] = jnp.concatenate(z_parts, axis=-1)


def wkv_pre(proj_b, hids, w2s, vecs, v_first, tt=256):
    S = proj_b.shape[0]
    tt = min(tt, S)
    has_v = v_first is not None
    nb = B_WIDTH // LANES
    nchunk_rows = (tt // WKV_CHUNK) * B_HDIM
    tok = lambda off: pl.BlockSpec((tt, LANES), lambda i, p: (i, off + p))
    prev = lambda off: pl.BlockSpec((8, LANES), lambda i, p: (jnp.maximum(i * (tt // 8) - 1, 0), off + p))
    hid = lambda a: pl.BlockSpec((tt, a.shape[1]), lambda i, p: (i, 0))
    wcol = lambda a: pl.BlockSpec((a.shape[0], LANES), lambda i, p: (0, p))
    in_specs = [tok(0), tok(nb), tok(2 * nb), prev(0), prev(nb), prev(2 * nb),
                hid(hids[0]), hid(hids[1]), hid(hids[2]), wcol(w2s[0]), wcol(w2s[1]), wcol(w2s[2]),
                pl.BlockSpec((N_VEC_ROWS, LANES), lambda i, p: (0, p))]
    args = [proj_b] * 6 + list(hids[:3]) + list(w2s[:3]) + [vecs]
    if has_v:
        in_specs += [hid(hids[3]), wcol(w2s[3]), tok(0)]
        args += [hids[3], w2s[3], v_first]
    n_state_rows = (S // WKV_CHUNK) * B_HDIM
    out_tok = pl.BlockSpec((tt, LANES), lambda i, p: (i, p))
    out_st = pl.BlockSpec((nchunk_rows, LANES), lambda i, p: (i, p))
    return pl.pallas_call(
        functools.partial(_wkv_pre_body, has_v=has_v, tt=tt),
        grid=(S // tt, nb),
        in_specs=in_specs,
        out_specs=[out_st, out_st, out_tok, out_tok, out_tok, out_tok, out_tok],
        out_shape=[jax.ShapeDtypeStruct((n_state_rows, B_WIDTH), F32)] * 2
                  + [jax.ShapeDtypeStruct((S, B_WIDTH), F32)] * 5,
        compiler_params=_cparams(("parallel", "parallel")),
    )(*args)


def _wkv_scan_body(m_ref, g_ref, q_ref, z_ref, gate_ref, bonus_ref, ln_ref, o_ref, h_ref, *, tt):
    i = pl.program_id(1)
    C = WKV_CHUNK
    N = B_HDIM

    @pl.when(i == 0)
    def _():
        h_ref[...] = jnp.zeros_like(h_ref)

    for c in range(tt // C):
        ts = slice(c * C, (c + 1) * C)
        ss = slice(c * N, (c + 1) * N)
        outs = []
        for hh in range(LANES // N):
            hs = slice(hh * N, (hh + 1) * N)
            H = h_ref[:, hs]
            y = jnp.dot(q_ref[ts, hs], H, preferred_element_type=F32, precision=HI) + z_ref[ts, hs]
            h_ref[:, hs] = jnp.dot(m_ref[ss, hs], H, preferred_element_type=F32, precision=HI) + g_ref[ss, hs]
            mean = jnp.mean(y, axis=-1, keepdims=True)
            yc = y - mean
            var = jnp.mean(yc * yc, axis=-1, keepdims=True)
            outs.append(yc * lax.rsqrt(var + B_GN_EPS))
        yn = jnp.concatenate(outs, axis=-1) * ln_ref[0:1, :] + ln_ref[1:2, :]
        o_ref[ts, :] = ((yn + bonus_ref[ts, :]) * gate_ref[ts, :]).astype(o_ref.dtype)


def wkv_scan(m, g, q, z, gate, bonus, ln, tt=256):
    S = q.shape[0]
    tt = min(tt, S)
    nb = B_WIDTH // LANES
    nchunk_rows = (tt // WKV_CHUNK) * B_HDIM
    tok = pl.BlockSpec((tt, LANES), lambda p, i: (i, p))
    st = pl.BlockSpec((nchunk_rows, LANES), lambda p, i: (i, p))
    return pl.pallas_call(
        functools.partial(_wkv_scan_body, tt=tt),
        grid=(nb, S // tt),
        in_specs=[st, st, tok, tok, tok, tok, pl.BlockSpec((8, LANES), lambda p, i: (0, p))],
        out_specs=tok,
        out_shape=jax.ShapeDtypeStruct((S, B_WIDTH), BF16),
        scratch_shapes=[pltpu.VMEM((B_HDIM, LANES), F32)],
        compiler_params=_cparams(("parallel", "arbitrary")),
    )(m, g, q, z, gate, bonus, ln)


def rwkv7_mixer(h, proj_b, p, v_first):
    has_v = v_first is not None
    mu = p["mu_wag"] if not has_v else jnp.concatenate([p["mu_wag"], p["v_mu"][None]], axis=0)
    hids = rwkv_lora(h, mu, p["w1"].astype(BF16), p["a1"].astype(BF16), p["g1"].astype(BF16),
                     p["v1"].astype(BF16) if has_v else None)
    rows = [p["mu_rkv"][0], p["mu_rkv"][1], p["mu_rkv"][2], p["w0"], p["a0"], p["k_k"], p["k_a"],
            p["r_k"].reshape(-1), p["v0"] if has_v else jnp.zeros((B_WIDTH,), F32)]
    vecs = jnp.concatenate([jnp.stack(rows), jnp.zeros((N_VEC_ROWS - len(rows), B_WIDTH), F32)], axis=0)
    w2s = [p["w2"], p["a2"], p["g2"]] + ([p["v2"]] if has_v else [])
    m, g, q, z, v_out, gate, bonus = wkv_pre(proj_b, hids, w2s, vecs, v_first)
    ln = jnp.concatenate([p["ln_w"][None], p["ln_b"][None], jnp.zeros((6, B_WIDTH), F32)], axis=0)
    return wkv_scan(m, g, q, z, gate, bonus, ln), v_out


def kernel(x, norm_mix_g, w_in, lam_q1, lam_k1, lam_q2, lam_k2, diff_subln_g, rw_mu_rkv, rw_mu_wag, rw_w0, rw_w1, rw_w2, rw_a0, rw_a1, rw_a2, rw_g1, rw_g2, rw_k_k, rw_k_a, rw_r_k, rw_ln_w, rw_ln_b, rw_v_mu, rw_v0, rw_v1, rw_v2, w_out, norm_ffn_g, w_up, w_down, norm_final_g):
    Bsz, S, D = x.shape
    depth = w_in.shape[0]
    topk = min(TOPK_MAX, S // 4)
    outs = []
    for b in range(Bsz):
        xb = x[b]
        v_first = None
        for l in range(depth):
            h = rmsnorm(xb, norm_mix_g[l], BF16)
            w = w_in[l]
            oA, oB, oC = 3 * A_WIDTH, 3 * A_WIDTH + 3 * B_WIDTH, 3 * A_WIDTH + 3 * B_WIDTH + 3 * C_WIDTH
            proj_a = matmul(h, w[:, :oA].astype(BF16), BF16)
            proj_b = matmul(h, w[:, oA:oB].astype(BF16), F32)
            proj_c = matmul(h, w[:, oB:oC].astype(BF16), BF16)
            n_idx = w.shape[1] - oC
            n_idx_pad = -(-n_idx // LANES) * LANES
            w_idx = jnp.pad(w[:, oC:], ((0, 0), (0, n_idx_pad - n_idx))).astype(BF16)
            proj_i = matmul(h, w_idx, F32)

            lam_init = 0.8 - 0.6 * math.exp(-0.3 * l)
            lam_vecs = jnp.stack([lam_q1[l], lam_k1[l], lam_q2[l], lam_k2[l]])
            o_a = diff_attention(proj_a, lam_vecs, diff_subln_g[l], lam_init)

            p = dict(mu_rkv=rw_mu_rkv[l], mu_wag=rw_mu_wag[l], w0=rw_w0[l], w1=rw_w1[l], w2=rw_w2[l],
                     a0=rw_a0[l], a1=rw_a1[l], a2=rw_a2[l], g1=rw_g1[l], g2=rw_g2[l], k_k=rw_k_k[l],
                     k_a=rw_k_a[l], r_k=rw_r_k[l], ln_w=rw_ln_w[l], ln_b=rw_ln_b[l])
            if l > 0:
                p.update(v_mu=rw_v_mu[l - 1], v0=rw_v0[l - 1], v1=rw_v1[l - 1], v2=rw_v2[l - 1])
            o_b, v_out = rwkv7_mixer(h, proj_b, p, v_first if l > 0 else None)
            if l == 0:
                v_first = v_out

            nq = IDX_HEADS * IDX_HDIM
            qi = proj_i[:, :nq].astype(BF16)
            ki = proj_i[:, nq:nq + IDX_HDIM].astype(BF16)
            wiT = proj_i[:, nq + IDX_HDIM:nq + IDX_HDIM + IDX_HEADS].T
            bias = dsa_index(ki, qi, wiT, topk)
            vT = proj_c[:, 2 * C_WIDTH:].T
            o_c = dsa_attention(proj_c, vT, bias).T

            mixed = jnp.concatenate([o_a, o_b, o_c], axis=-1)
            xb = matmul(mixed, w_out[l].astype(BF16), F32, epilogue="residual", residual=xb)
            h2 = rmsnorm(xb, norm_ffn_g[l], BF16)
            up = matmul(h2, w_up[l].astype(BF16), BF16, epilogue="relu2")
            xb = matmul(up, w_down[l].astype(BF16), F32, epilogue="residual", residual=xb)
        outs.append(rmsnorm(xb, norm_final_g, F32))
    return jnp.stack(outs)
```

```python
import functools
import math

import jax
import jax.numpy as jnp
from jax import lax
from jax.experimental import pallas as pl
from jax.experimental.pallas import tpu as pltpu

F32 = jnp.float32
BF16 = jnp.bfloat16

A_HEADS, A_HDIM = 8, 64
A_WIDTH = A_HEADS * 2 * A_HDIM
B_HDIM, B_WIDTH = 64, 2048
B_HEADS = B_WIDTH // B_HDIM
B_GN_EPS = 64e-5
C_HEADS, C_HDIM = 8, 128
C_WIDTH = C_HEADS * C_HDIM
IDX_HEADS, IDX_HDIM = 16, 64
TOPK_MAX = 256
EPS = 1e-6

LANES = 128
VMEM_LIMIT = 56 * 1024 * 1024
NEG = -1e30
WKV_CHUNK = 64
HI = lax.Precision.HIGHEST


def _cparams(sem):
    return pltpu.CompilerParams(dimension_semantics=sem, vmem_limit_bytes=VMEM_LIMIT)


def _rmsnorm_body(x_ref, g_ref, o_ref):
    x = x_ref[...]
    ms = jnp.mean(x * x, axis=-1, keepdims=True)
    o_ref[...] = (x * lax.rsqrt(ms + EPS) * g_ref[...]).astype(o_ref.dtype)


def rmsnorm(x, g, out_dtype, tm=256):
    S, D = x.shape
    tm = min(tm, S)
    return pl.pallas_call(
        _rmsnorm_body,
        grid=(S // tm,),
        in_specs=[pl.BlockSpec((tm, D), lambda i: (i, 0)),
                  pl.BlockSpec((1, D), lambda i: (0, 0))],
        out_specs=pl.BlockSpec((tm, D), lambda i: (i, 0)),
        out_shape=jax.ShapeDtypeStruct((S, D), out_dtype),
        compiler_params=_cparams(("parallel",)),
    )(x, g.reshape(1, D))


def _mm_body(a_ref, b_ref, *rest, nk, epilogue):
    if epilogue == "residual":
        res_ref, o_ref, acc_ref = rest
    else:
        o_ref, acc_ref = rest
    k = pl.program_id(2)

    @pl.when(k == 0)
    def _():
        acc_ref[...] = jnp.zeros_like(acc_ref)

    acc_ref[...] += jnp.dot(a_ref[...], b_ref[...], preferred_element_type=F32)

    @pl.when(k == nk - 1)
    def _():
        acc = acc_ref[...]
        if epilogue == "relu2":
            r = jnp.maximum(acc, 0.0)
            acc = r * r
        elif epilogue == "residual":
            acc = acc + res_ref[...]
        o_ref[...] = acc.astype(o_ref.dtype)


def _pick(n, pref):
    for t in pref:
        if n % t == 0:
            return t
    return n


def matmul(a, b, out_dtype, epilogue="none", residual=None):
    M, K = a.shape
    _, N = b.shape
    tm = _pick(M, (1024, 512, 256))
    tn = _pick(N, (1024, 768, 512, 384, 256, 128))
    tk = _pick(K, (1024, 512))
    nk = K // tk
    in_specs = [pl.BlockSpec((tm, tk), lambda i, j, k: (i, k)),
                pl.BlockSpec((tk, tn), lambda i, j, k: (k, j))]
    args = [a, b]
    if epilogue == "residual":
        in_specs.append(pl.BlockSpec((tm, tn), lambda i, j, k: (i, j)))
        args.append(residual)
    return pl.pallas_call(
        functools.partial(_mm_body, nk=nk, epilogue=epilogue),
        grid=(M // tm, N // tn, nk),
        in_specs=in_specs,
        out_specs=pl.BlockSpec((tm, tn), lambda i, j, k: (i, j)),
        out_shape=jax.ShapeDtypeStruct((M, N), out_dtype),
        scratch_shapes=[pltpu.VMEM((tm, tn), F32)],
        compiler_params=_cparams(("parallel", "parallel", "arbitrary")),
    )(*args)


def _diffattn_body(q_ref, k_ref, v_ref, lam_ref, g_ref, o_ref, m_ref, acc_ref, *, tq, lam_init):
    tk = tq
    i = pl.program_id(1)
    hw = 2 * A_HDIM
    q = q_ref[...] * (A_HDIM ** -0.5)
    lane = lax.broadcasted_iota(jnp.int32, (tq, hw), 1)
    zero = jnp.zeros_like(q)
    qz = jnp.concatenate([jnp.where(lane < A_HDIM, q, zero), jnp.where(lane >= A_HDIM, q, zero)], axis=0)
    m_ref[...] = jnp.full_like(m_ref, NEG)
    acc_ref[...] = jnp.zeros_like(acc_ref)
    ones = jnp.ones((tk, hw), BF16)

    def scores(j):
        k = k_ref[pl.ds(pl.multiple_of(j * tk, tk), tk), :]
        return lax.dot_general(qz, k, (((1,), (1,)), ((), ())), preferred_element_type=F32)

    def update(j, s, masked):
        r0 = pl.multiple_of(j * tk, tk)
        if masked:
            row = lax.broadcasted_iota(jnp.int32, (2 * tq, tk), 0)
            qpos = i * tq + jnp.where(row >= tq, row - tq, row)
            kpos = r0 + lax.broadcasted_iota(jnp.int32, (2 * tq, tk), 1)
            s = jnp.where(kpos <= qpos, s, NEG)
        m_old = m_ref[...]
        m_new = jnp.maximum(m_old, jnp.max(s, axis=-1, keepdims=True))
        alpha = jnp.exp(m_old - m_new)
        p = jnp.exp(s - m_new)
        v_ext = jnp.concatenate([v_ref[pl.ds(r0, tk), :], ones], axis=-1)
        acc_ref[...] = alpha * acc_ref[...] + jnp.dot(p.astype(BF16), v_ext, preferred_element_type=F32)
        m_ref[...] = m_new

    def body(j, s):
        s_next = scores(j + 1)
        update(j, s, False)
        return s_next

    s_diag = lax.fori_loop(0, i, body, scores(0))
    update(i, s_diag, True)

    lv = lam_ref[...]
    lam = (jnp.exp(jnp.sum(lv[0:1] * lv[1:2], axis=-1, keepdims=True))
           - jnp.exp(jnp.sum(lv[2:3] * lv[3:4], axis=-1, keepdims=True)) + lam_init)
    acc = acc_ref[...]
    on = acc[:, :hw] / acc[:, hw:]
    o = on[:tq] - lam * on[tq:]
    ms = jnp.mean(o * o, axis=-1, keepdims=True)
    o_ref[...] = (o * lax.rsqrt(ms + EPS) * g_ref[...] * (1.0 - lam_init)).astype(o_ref.dtype)


def diff_attention(qkv, lam_vecs, subln_g, lam_init, tq=512):
    S = qkv.shape[0]
    tq = min(tq, S)
    hw = 2 * A_HDIM
    return pl.pallas_call(
        functools.partial(_diffattn_body, tq=tq, lam_init=lam_init),
        grid=(A_HEADS, S // tq),
        in_specs=[pl.BlockSpec((tq, hw), lambda h, i: (i, h)),
                  pl.BlockSpec((S, hw), lambda h, i: (0, A_HEADS + h)),
                  pl.BlockSpec((S, hw), lambda h, i: (0, 2 * A_HEADS + h)),
                  pl.BlockSpec((4, A_HDIM), lambda h, i: (0, 0)),
                  pl.BlockSpec((1, hw), lambda h, i: (0, 0))],
        out_specs=pl.BlockSpec((tq, hw), lambda h, i: (i, h)),
        out_shape=jax.ShapeDtypeStruct((S, A_WIDTH), BF16),
        scratch_shapes=[pltpu.VMEM((2 * tq, 1), F32), pltpu.VMEM((2 * tq, 2 * hw), F32)],
        compiler_params=_cparams(("parallel", "parallel")),
    )(qkv, qkv, qkv, lam_vecs, subln_g.reshape(1, hw))


KEY_NEG_INF = -2139095041


def _float_key(s):
    b = pltpu.bitcast(s, jnp.int32)
    return b ^ ((b >> 31) & jnp.int32(0x7FFFFFFF))


def _dsa_index_body(ki_ref, qi_ref, wiT_ref, bias_ref, key_ref, *, qb, kc, topk, nkc_total):
    i = pl.program_id(0)
    nch = ((i + 1) * qb) // kc
    wi = wiT_ref[...] * (IDX_HEADS ** -0.5 * IDX_HDIM ** -0.5)
    qi = qi_ref[...]
    tpos = i * qb + lax.broadcasted_iota(jnp.int32, (kc, qb), 1)
    srow = lax.broadcasted_iota(jnp.int32, (kc, qb), 0)

    def score_chunk(c, carry):
        r0 = pl.multiple_of(c * kc, kc)
        kic = ki_ref[pl.ds(r0, kc), :]
        acc = jnp.zeros((kc, qb), F32)
        for h in range(IDX_HEADS):
            d = lax.dot_general(kic, qi[:, h * IDX_HDIM:(h + 1) * IDX_HDIM],
                                (((1,), (1,)), ((), ())), preferred_element_type=F32)
            acc = acc + jnp.maximum(d, 0.0) * wi[h:h + 1, :]
        acc = jnp.where(r0 + srow <= tpos, acc, -jnp.inf)
        key_ref[pl.ds(r0, kc), :] = _float_key(acc)
        return carry

    lax.fori_loop(0, nch, score_chunk, 0)

    def count_ge(cand):
        def body(c, cnt):
            r0 = pl.multiple_of(c * kc, kc)
            ge = (key_ref[pl.ds(r0, kc), :] >= cand).astype(jnp.int32)
            return cnt + jnp.sum(ge.reshape(kc // 8, 8, qb), axis=0)
        cnt8 = lax.fori_loop(0, nch, body, jnp.zeros((8, qb), jnp.int32))
        return jnp.sum(cnt8, axis=0, keepdims=True)

    def bit_step(it, tau):
        cand = tau + (jnp.int32(1) << (31 - it))
        return jnp.where(count_ge(cand) >= topk, cand, tau)

    tau = lax.fori_loop(0, 32, bit_step, jnp.full((1, qb), jnp.iinfo(jnp.int32).min, jnp.int32))
    tau = jnp.maximum(tau, KEY_NEG_INF + 1)

    def write_chunk(c, carry):
        r0 = pl.multiple_of(c * kc, kc)
        sel = key_ref[pl.ds(r0, kc), :] >= tau
        bias_ref[pl.ds(r0, kc), :] = jnp.where(sel, 0.0, NEG).astype(bias_ref.dtype)
        return carry

    lax.fori_loop(0, nch, write_chunk, 0)

    def fill_chunk(c, carry):
        r0 = pl.multiple_of(c * kc, kc)
        bias_ref[pl.ds(r0, kc), :] = jnp.full((kc, qb), NEG, bias_ref.dtype)
        return carry

    lax.fori_loop(nch, nkc_total, fill_chunk, 0)


def dsa_index(ki, qi, wiT, topk, qb=256, kc=256):
    S = ki.shape[0]
    qb = min(qb, S)
    kc = min(kc, qb)
    return pl.pallas_call(
        functools.partial(_dsa_index_body, qb=qb, kc=kc, topk=topk, nkc_total=S // kc),
        grid=(S // qb,),
        in_specs=[pl.BlockSpec((S, IDX_HDIM), lambda i: (0, 0)),
                  pl.BlockSpec((qb, IDX_HEADS * IDX_HDIM), lambda i: (i, 0)),
                  pl.BlockSpec((IDX_HEADS, qb), lambda i: (0, i))],
        out_specs=pl.BlockSpec((S, qb), lambda i: (0, i)),
        out_shape=jax.ShapeDtypeStruct((S, S), BF16),
        scratch_shapes=[pltpu.VMEM((S, qb), jnp.int32)],
        compiler_params=_cparams(("parallel",)),
    )(ki, qi, wiT)


def _dsa_attn_body(qi_ref, kj_ref, q_ref, k_ref, vT_ref, bias_ref, oT_ref, m_ref, l_ref, acc_ref, *, qb, kc):
    step = pl.program_id(0)
    i = qi_ref[step]
    j = kj_ref[step]
    scale = C_HDIM ** -0.5
    heads = [slice(h * C_HDIM, (h + 1) * C_HDIM) for h in range(C_HEADS)]

    @pl.when(j == 0)
    def _():
        m_ref[...] = jnp.full_like(m_ref, NEG)
        l_ref[...] = jnp.zeros_like(l_ref)
        acc_ref[...] = jnp.zeros_like(acc_ref)

    bias = bias_ref[...].astype(F32)
    sT = [lax.dot_general(k_ref[:, cs], q_ref[:, cs], (((1,), (1,)), ((), ())),
                          preferred_element_type=F32) * scale + bias for cs in heads]
    m_old = m_ref[...]
    m_new = jnp.maximum(m_old, jnp.concatenate([jnp.max(s, axis=0, keepdims=True) for s in sT], axis=0))
    alpha = jnp.exp(m_old - m_new)
    p = [jnp.exp(s - m_new[h:h + 1, :]) for h, s in enumerate(sT)]
    l_ref[...] = alpha * l_ref[...] + jnp.concatenate([jnp.sum(x, axis=0, keepdims=True) for x in p], axis=0)
    m_ref[...] = m_new
    for h, cs in enumerate(heads):
        acc_ref[cs, :] = alpha[h:h + 1, :] * acc_ref[cs, :] + jnp.dot(vT_ref[cs, :], p[h].astype(BF16),
                                                                       preferred_element_type=F32)

    @pl.when(j == ((i + 1) * qb - 1) // kc)
    def _():
        l = l_ref[...]
        for h, cs in enumerate(heads):
            oT_ref[cs, :] = (acc_ref[cs, :] / l[h:h + 1, :]).astype(oT_ref.dtype)


def dsa_attention(qkv, vT, bias, qb=256, kc=512):
    S = qkv.shape[0]
    qb = min(qb, S)
    kc = min(kc, S)
    pairs = [(i, j) for i in range(S // qb) for j in range(((i + 1) * qb - 1) // kc + 1)]
    qi = jnp.asarray([p_[0] for p_ in pairs], jnp.int32)
    kj = jnp.asarray([p_[1] for p_ in pairs], jnp.int32)
    grid_spec = pltpu.PrefetchScalarGridSpec(
        num_scalar_prefetch=2,
        grid=(len(pairs),),
        in_specs=[pl.BlockSpec((qb, C_WIDTH), lambda s, qi, kj: (qi[s], 0)),
                  pl.BlockSpec((kc, C_WIDTH), lambda s, qi, kj: (kj[s], 1)),
                  pl.BlockSpec((C_WIDTH, kc), lambda s, qi, kj: (0, kj[s])),
                  pl.BlockSpec((kc, qb), lambda s, qi, kj: (kj[s], qi[s]))],
        out_specs=pl.BlockSpec((C_WIDTH, qb), lambda s, qi, kj: (0, qi[s])),
        scratch_shapes=[pltpu.VMEM((C_HEADS, qb), F32), pltpu.VMEM((C_HEADS, qb), F32),
                        pltpu.VMEM((C_WIDTH, qb), F32)])
    return pl.pallas_call(
        functools.partial(_dsa_attn_body, qb=qb, kc=kc),
        grid_spec=grid_spec,
        out_shape=jax.ShapeDtypeStruct((C_WIDTH, S), BF16),
        compiler_params=_cparams(("arbitrary",)),
    )(qi, kj, qkv, qkv, vT, bias)


def _sigmoid(x):
    return 1.0 / (1.0 + jnp.exp(-x))


def _shift_rows(cur, prev_ref, first_tile):
    prev_row = jnp.where(first_tile, 0.0, prev_ref[7:8, :].astype(F32))
    rolled = pltpu.roll(cur, 1, axis=0)
    row = lax.broadcasted_iota(jnp.int32, cur.shape, 0)
    return jnp.where(row == 0, prev_row, rolled)


def _rwkv_lora_body(*refs, has_v):
    if has_v:
        (h_ref, hp_ref, mu_ref, w1_ref, a1_ref, g1_ref, v1_ref, ow_ref, oa_ref, og_ref, ov_ref) = refs
    else:
        (h_ref, hp_ref, mu_ref, w1_ref, a1_ref, g1_ref, ow_ref, oa_ref, og_ref) = refs
    i = pl.program_id(0)
    h = h_ref[...].astype(F32)
    dh = _shift_rows(h, hp_ref, i == 0) - h

    def lora(row, w_ref):
        xm = (h + dh * mu_ref[row:row + 1, :]).astype(BF16)
        return jnp.dot(xm, w_ref[...], preferred_element_type=F32)

    ow_ref[...] = jnp.tanh(lora(0, w1_ref))
    oa_ref[...] = lora(1, a1_ref)
    og_ref[...] = _sigmoid(lora(2, g1_ref))
    if has_v:
        ov_ref[...] = lora(3, v1_ref)


def rwkv_lora(h, mu, w1, a1, g1, v1, tm=256):
    S, D = h.shape
    tm = min(tm, S)
    has_v = v1 is not None
    ws = [w1, a1, g1] + ([v1] if has_v else [])
    full = lambda a: pl.BlockSpec(a.shape, lambda i: (0, 0))
    return pl.pallas_call(
        functools.partial(_rwkv_lora_body, has_v=has_v),
        grid=(S // tm,),
        in_specs=[pl.BlockSpec((tm, D), lambda i: (i, 0)),
                  pl.BlockSpec((8, D), lambda i: (jnp.maximum(i * (tm // 8) - 1, 0), 0)),
                  full(mu)] + [full(w) for w in ws],
        out_specs=[pl.BlockSpec((tm, w.shape[1]), lambda i: (i, 0)) for w in ws],
        out_shape=[jax.ShapeDtypeStruct((S, w.shape[1]), F32) for w in ws],
        compiler_params=_cparams(("parallel",)),
    )(h, h, mu, *ws)


(V_MU_R, V_MU_K, V_MU_V, V_W0, V_A0, V_KK, V_KA, V_RK, V_V0) = range(9)
N_VEC_ROWS = 16


def _wkv_pre_body(*refs, has_v, tt):
    if has_v:
        (r_ref, k_ref, v_ref, rp_ref, kp_ref, vp_ref, hw_ref, ha_ref, hg_ref, w2_ref, a2_ref, g2_ref,
         vec_ref, hv_ref, v2_ref, vf_ref,
         m_out, g_out, q_out, z_out, vout_ref, gate_ref, bonus_ref) = refs
    else:
        (r_ref, k_ref, v_ref, rp_ref, kp_ref, vp_ref, hw_ref, ha_ref, hg_ref, w2_ref, a2_ref, g2_ref,
         vec_ref,
         m_out, g_out, q_out, z_out, vout_ref, gate_ref, bonus_ref) = refs
    i = pl.program_id(0)
    first = i == 0
    C = WKV_CHUNK
    N = B_HDIM
    vec = lambda row: vec_ref[row:row + 1, :]

    r = r_ref[...]
    k = k_ref[...]
    v = v_ref[...]
    r = r + (_shift_rows(r, rp_ref, first) - r) * vec(V_MU_R)
    k = k + (_shift_rows(k, kp_ref, first) - k) * vec(V_MU_K)
    v = v + (_shift_rows(v, vp_ref, first) - v) * vec(V_MU_V)

    wl = vec(V_W0) + jnp.dot(hw_ref[...], w2_ref[...], preferred_element_type=F32, precision=HI)
    z = -wl
    softplus = jnp.maximum(z, 0.0) + jnp.log(1.0 + jnp.exp(-jnp.abs(z)))
    logw = -jnp.exp(-softplus - 0.5)
    a_sig = _sigmoid(vec(V_A0) + jnp.dot(ha_ref[...], a2_ref[...], preferred_element_type=F32, precision=HI))
    gate = jnp.dot(hg_ref[...], g2_ref[...], preferred_element_type=F32, precision=HI)
    if has_v:
        mix = _sigmoid(vec(V_V0) + jnp.dot(hv_ref[...], v2_ref[...], preferred_element_type=F32, precision=HI))
        v = v + (vf_ref[...] - v) * mix
    vout_ref[...] = v
    gate_ref[...] = gate

    kk = k * vec(V_KK)
    k_new = k * (1.0 + (a_sig - 1.0) * vec(V_KA))
    rk = r * k_new * vec(V_RK)

    rowc = lax.broadcasted_iota(jnp.int32, (C, C), 0)
    colc = lax.broadcasted_iota(jnp.int32, (C, C), 1)
    tril_incl = (rowc >= colc).astype(F32)
    row2 = lax.broadcasted_iota(jnp.int32, (2 * C, 2 * C), 0)
    col2 = lax.broadcasted_iota(jnp.int32, (2 * C, 2 * C), 1)
    rr = jnp.where(row2 >= C, row2 - C, row2)
    cc = jnp.where(col2 >= C, col2 - C, col2)
    keep = (rr - cc) >= jnp.where(row2 >= C, 0, 1)
    eye_n = (lax.broadcasted_iota(jnp.int32, (N, N), 0) == lax.broadcasted_iota(jnp.int32, (N, N), 1)).astype(F32)

    bonus_parts = []
    per_head = []
    for hh in range(LANES // N):
        hs = slice(hh * N, (hh + 1) * N)
        kkh = kk[:, hs]
        nrm = jnp.sqrt(jnp.sum(kkh * kkh, axis=-1, keepdims=True))
        kkn = kkh / jnp.maximum(nrm, 1e-12)
        bonus_parts.append(jnp.sum(rk[:, hs], axis=-1, keepdims=True) * v[:, hs])
        per_head.append((r[:, hs], k_new[:, hs], v[:, hs], -kkn, kkn * a_sig[:, hs]))
    bonus_ref[...] = jnp.concatenate(bonus_parts, axis=-1)

    nh = LANES // N
    units = [(c, hh) for c in range(tt // C) for hh in range(nh)]
    ts_of = lambda c: slice(c * C, (c + 1) * C)
    hs_of = lambda hh: slice(hh * N, (hh + 1) * N)
    decays = []
    for c in range(tt // C):
        lw = logw[ts_of(c)]
        cum = jnp.dot(tril_incl, lw, preferred_element_type=F32, precision=HI)
        cum_last = cum[C - 1:C, :]
        decays.append((jnp.exp(cum), jnp.exp(cum - lw), jnp.exp(-cum), jnp.exp(cum_last - cum), jnp.exp(cum_last)))
    st = []
    for (c, hh) in units:
        e_in, e_ex, e_neg, e_end, gam = (t_[:, hs_of(hh)] for t_ in decays[c])
        rh, kh, vh, ah, bh = (t_[ts_of(c)] for t_ in per_head[hh])
        st.append(dict(At=ah * e_ex, Rt=rh * e_in, Bt=bh * e_neg, Kt=kh * e_neg, Bg=bh * e_end, Kg=kh * e_end,
                       v=vh, gam=gam))
    for u in st:
        left = jnp.concatenate([u["At"], u["Rt"]], axis=0).astype(BF16)
        right = jnp.concatenate([u["Bt"], u["Kt"]], axis=0).astype(BF16)
        AA = lax.dot_general(left, right, (((1,), (1,)), ((), ())), preferred_element_type=F32)
        u["AA"] = jnp.where(keep, AA, 0.0)
    for u in st:
        u["akv"] = jnp.dot(u["AA"][:C, C:].astype(BF16), u["v"].astype(BF16), preferred_element_type=F32)
    for u in st:
        u["XA"] = jnp.concatenate([u["At"], u["akv"], u["AA"][:C, :C]], axis=-1)
    for step in range(6):
        for u in st:
            XA = u["XA"]
            Ap = XA[:, 2 * N:].astype(BF16)
            if step < 5:
                prod = jnp.dot(Ap, XA.astype(BF16), preferred_element_type=F32)
                u["XA"] = jnp.concatenate([XA[:, :2 * N] + prod[:, :2 * N], prod[:, 2 * N:]], axis=-1)
            else:
                u["X"] = XA[:, :2 * N] + jnp.dot(Ap, XA[:, :2 * N].astype(BF16), preferred_element_type=F32)
    for u in st:
        lower = jnp.concatenate([jnp.zeros((C, N), F32), u["v"]], axis=-1)
        u["W2"] = jnp.concatenate([u["X"], lower], axis=0).astype(BF16)
    for u in st:
        bk = jnp.concatenate([u["Bg"], u["Kg"]], axis=0).astype(BF16)
        u["MG"] = lax.dot_general(bk, u["W2"], (((0,), (0,)), ((), ())), preferred_element_type=F32)
    for u in st:
        u["QZ"] = jnp.dot(u["AA"][C:, :].astype(BF16), u["W2"], preferred_element_type=F32)
    zn = jnp.zeros((N, N), F32)
    blockdiag = lambda x0, x1: jnp.concatenate([jnp.concatenate([x0, zn], axis=-1),
                                                jnp.concatenate([zn, x1], axis=-1)], axis=0)
    for c in range(tt // C):
        u0, u1 = st[c * nh:(c + 1) * nh]
        rows = slice(c * LANES, (c + 1) * LANES)
        m_out[rows, :] = blockdiag(eye_n * u0["gam"] + u0["MG"][:, :N],
                                   eye_n * u1["gam"] + u1["MG"][:, :N]).astype(m_out.dtype)
        g_out[rows, :] = blockdiag(u0["MG"][:, N:], u1["MG"][:, N:]).astype(g_out.dtype)
        q_out[ts_of(c), :] = jnp.concatenate([u["Rt"] + u["QZ"][:, :N] for u in (u0, u1)], axis=-1).astype(q_out.dtype)
        z_out[ts_of(c), :] = jnp.concatenate([u["QZ"][:, N:] for u in (u0, u1)], axis=-1)


def wkv_pre(proj_b, hids, w2s, vecs, v_first, tt=256):
    S = proj_b.shape[0]
    tt = min(tt, S)
    has_v = v_first is not None
    nb = B_WIDTH // LANES
    nstate_rows = (tt // WKV_CHUNK) * LANES
    tok = lambda off: pl.BlockSpec((tt, LANES), lambda i, p: (i, off + p))
    prev = lambda off: pl.BlockSpec((8, LANES), lambda i, p: (jnp.maximum(i * (tt // 8) - 1, 0), off + p))
    hid = lambda a: pl.BlockSpec((tt, a.shape[1]), lambda i, p: (i, 0))
    wcol = lambda a: pl.BlockSpec((a.shape[0], LANES), lambda i, p: (0, p))
    in_specs = [tok(0), tok(nb), tok(2 * nb), prev(0), prev(nb), prev(2 * nb),
                hid(hids[0]), hid(hids[1]), hid(hids[2]), wcol(w2s[0]), wcol(w2s[1]), wcol(w2s[2]),
                pl.BlockSpec((N_VEC_ROWS, LANES), lambda i, p: (0, p))]
    args = [proj_b] * 6 + list(hids[:3]) + list(w2s[:3]) + [vecs]
    if has_v:
        in_specs += [hid(hids[3]), wcol(w2s[3]), tok(0)]
        args += [hids[3], w2s[3], v_first]
    state_shape = jax.ShapeDtypeStruct(((S // WKV_CHUNK) * LANES, B_WIDTH), BF16)
    out_tok = pl.BlockSpec((tt, LANES), lambda i, p: (i, p))
    out_st = pl.BlockSpec((nstate_rows, LANES), lambda i, p: (i, p))
    return pl.pallas_call(
        functools.partial(_wkv_pre_body, has_v=has_v, tt=tt),
        grid=(S // tt, nb),
        in_specs=in_specs,
        out_specs=[out_st, out_st, out_tok, out_tok, out_tok, out_tok, out_tok],
        out_shape=[state_shape, state_shape, jax.ShapeDtypeStruct((S, B_WIDTH), BF16)]
                  + [jax.ShapeDtypeStruct((S, B_WIDTH), F32)] * 4,
        compiler_params=_cparams(("parallel", "parallel")),
    )(*args)


def _wkv_scan_body(m_ref, g_ref, q_ref, z_ref, gate_ref, bonus_ref, ln_ref, o_ref, h_ref, *, tt):
    i = pl.program_id(0)
    C = WKV_CHUNK
    N = B_HDIM
    npair = B_WIDTH // LANES

    @pl.when(i == 0)
    def _():
        h_ref[...] = jnp.zeros_like(h_ref)

    row = lax.broadcasted_iota(jnp.int32, (LANES, LANES), 0)
    col = lax.broadcasted_iota(jnp.int32, (LANES, LANES), 1)
    head_avg = jnp.where((row // N) == (col // N), 1.0 / N, 0.0).astype(BF16)

    def head_mean(x):
        hi = x.astype(BF16)
        lo = (x - hi.astype(F32)).astype(BF16)
        return (jnp.dot(hi, head_avg, preferred_element_type=F32)
                + jnp.dot(lo, head_avg, preferred_element_type=F32))

    for c in range(tt // C):
        ts = slice(c * C, (c + 1) * C)
        ss = slice(c * LANES, (c + 1) * LANES)
        ys = []
        for p in range(npair):
            ps = slice(p * LANES, (p + 1) * LANES)
            Hb = h_ref[p].astype(BF16)
            ys.append(jnp.dot(q_ref[ts, ps], Hb, preferred_element_type=F32) + z_ref[ts, ps])
            h_ref[p] = jnp.dot(m_ref[ss, ps], Hb, preferred_element_type=F32) + g_ref[ss, ps].astype(F32)
        for p in range(npair):
            ps = slice(p * LANES, (p + 1) * LANES)
            yc = ys[p] - head_mean(ys[p])
            var = head_mean(yc * yc)
            yn = yc * lax.rsqrt(var + B_GN_EPS) * ln_ref[0:1, ps] + ln_ref[1:2, ps]
            o_ref[ts, ps] = ((yn + bonus_ref[ts, ps]) * gate_ref[ts, ps]).astype(o_ref.dtype)


def wkv_scan(m, g, q, z, gate, bonus, ln, tt=256):
    S = q.shape[0]
    tt = min(tt, S)
    npair = B_WIDTH // LANES
    nstate_rows = (tt // WKV_CHUNK) * LANES
    tok = pl.BlockSpec((tt, B_WIDTH), lambda i: (i, 0))
    st = pl.BlockSpec((nstate_rows, B_WIDTH), lambda i: (i, 0))
    return pl.pallas_call(
        functools.partial(_wkv_scan_body, tt=tt),
        grid=(S // tt,),
        in_specs=[st, st, tok, tok, tok, tok, pl.BlockSpec((8, B_WIDTH), lambda i: (0, 0))],
        out_specs=tok,
        out_shape=jax.ShapeDtypeStruct((S, B_WIDTH), BF16),
        scratch_shapes=[pltpu.VMEM((npair, LANES, LANES), F32)],
        compiler_params=_cparams(("arbitrary",)),
    )(m, g, q, z, gate, bonus, ln)


def rwkv7_mixer(h, proj_b, p, v_first):
    has_v = v_first is not None
    mu = p["mu_wag"] if not has_v else jnp.concatenate([p["mu_wag"], p["v_mu"][None]], axis=0)
    hids = rwkv_lora(h, mu, p["w1"].astype(BF16), p["a1"].astype(BF16), p["g1"].astype(BF16),
                     p["v1"].astype(BF16) if has_v else None)
    rows = [p["mu_rkv"][0], p["mu_rkv"][1], p["mu_rkv"][2], p["w0"], p["a0"], p["k_k"], p["k_a"],
            p["r_k"].reshape(-1), p["v0"] if has_v else jnp.zeros((B_WIDTH,), F32)]
    vecs = jnp.concatenate([jnp.stack(rows), jnp.zeros((N_VEC_ROWS - len(rows), B_WIDTH), F32)], axis=0)
    w2s = [p["w2"], p["a2"], p["g2"]] + ([p["v2"]] if has_v else [])
    m, g, q, z, v_out, gate, bonus = wkv_pre(proj_b, hids, w2s, vecs, v_first)
    ln = jnp.concatenate([p["ln_w"][None], p["ln_b"][None], jnp.zeros((6, B_WIDTH), F32)], axis=0)
    return wkv_scan(m, g, q, z, gate, bonus, ln), v_out


def kernel(x, norm_mix_g, w_in, lam_q1, lam_k1, lam_q2, lam_k2, diff_subln_g, rw_mu_rkv, rw_mu_wag, rw_w0, rw_w1, rw_w2, rw_a0, rw_a1, rw_a2, rw_g1, rw_g2, rw_k_k, rw_k_a, rw_r_k, rw_ln_w, rw_ln_b, rw_v_mu, rw_v0, rw_v1, rw_v2, w_out, norm_ffn_g, w_up, w_down, norm_final_g):
    Bsz, S, D = x.shape
    depth = w_in.shape[0]
    topk = min(TOPK_MAX, S // 4)
    outs = []
    for b in range(Bsz):
        xb = x[b]
        v_first = None
        for l in range(depth):
            h = rmsnorm(xb, norm_mix_g[l], BF16)
            w = w_in[l]
            oA, oB, oC = 3 * A_WIDTH, 3 * A_WIDTH + 3 * B_WIDTH, 3 * A_WIDTH + 3 * B_WIDTH + 3 * C_WIDTH
            proj_a = matmul(h, w[:, :oA].astype(BF16), BF16)
            proj_b = matmul(h, w[:, oA:oB].astype(BF16), F32)
            proj_c = matmul(h, w[:, oB:oC].astype(BF16), BF16)
            n_idx = w.shape[1] - oC
            n_idx_pad = -(-n_idx // LANES) * LANES
            w_idx = jnp.pad(w[:, oC:], ((0, 0), (0, n_idx_pad - n_idx))).astype(BF16)
            proj_i = matmul(h, w_idx, F32)

            lam_init = 0.8 - 0.6 * math.exp(-0.3 * l)
            lam_vecs = jnp.stack([lam_q1[l], lam_k1[l], lam_q2[l], lam_k2[l]])
            o_a = diff_attention(proj_a, lam_vecs, diff_subln_g[l], lam_init)

            p = dict(mu_rkv=rw_mu_rkv[l], mu_wag=rw_mu_wag[l], w0=rw_w0[l], w1=rw_w1[l], w2=rw_w2[l],
                     a0=rw_a0[l], a1=rw_a1[l], a2=rw_a2[l], g1=rw_g1[l], g2=rw_g2[l], k_k=rw_k_k[l],
                     k_a=rw_k_a[l], r_k=rw_r_k[l], ln_w=rw_ln_w[l], ln_b=rw_ln_b[l])
            if l > 0:
                p.update(v_mu=rw_v_mu[l - 1], v0=rw_v0[l - 1], v1=rw_v1[l - 1], v2=rw_v2[l - 1])
            o_b, v_out = rwkv7_mixer(h, proj_b, p, v_first if l > 0 else None)
            if l == 0:
                v_first = v_out

            nq = IDX_HEADS * IDX_HDIM
            qi = proj_i[:, :nq].astype(BF16)
            ki = proj_i[:, nq:nq + IDX_HDIM].astype(BF16)
            wiT = proj_i[:, nq + IDX_HDIM:nq + IDX_HDIM + IDX_HEADS].T
            bias = dsa_index(ki, qi, wiT, topk)
            vT = proj_c[:, 2 * C_WIDTH:].T
            o_c = dsa_attention(proj_c, vT, bias).T

            mixed = jnp.concatenate([o_a, o_b, o_c], axis=-1)
            xb = matmul(mixed, w_out[l].astype(BF16), F32, epilogue="residual", residual=xb)
            h2 = rmsnorm(xb, norm_ffn_g[l], BF16)
            up = matmul(h2, w_up[l].astype(BF16), BF16, epilogue="relu2")
            xb = matmul(up, w_down[l].astype(BF16), F32, epilogue="residual", residual=xb)
        outs.append(rmsnorm(xb, norm_final_g, F32))
    return jnp.stack(outs)
```

```python
import functools
import math

import jax
import jax.numpy as jnp
from jax import lax
from jax.experimental import pallas as pl
from jax.experimental.pallas import tpu as pltpu

F32 = jnp.float32
BF16 = jnp.bfloat16

A_HEADS, A_HDIM = 8, 64
A_WIDTH = A_HEADS * 2 * A_HDIM
B_HDIM, B_WIDTH = 64, 2048
B_HEADS = B_WIDTH // B_HDIM
B_GN_EPS = 64e-5
C_HEADS, C_HDIM = 8, 128
C_WIDTH = C_HEADS * C_HDIM
IDX_HEADS, IDX_HDIM = 16, 64
TOPK_MAX = 256
EPS = 1e-6

LANES = 128
VMEM_LIMIT = 56 * 1024 * 1024
NEG = -1e30
WKV_CHUNK = 64
HI = lax.Precision.HIGHEST


def _cparams(sem):
    return pltpu.CompilerParams(dimension_semantics=sem, vmem_limit_bytes=VMEM_LIMIT)


def _rmsnorm_body(x_ref, g_ref, o_ref):
    x = x_ref[...]
    ms = jnp.mean(x * x, axis=-1, keepdims=True)
    o_ref[...] = (x * lax.rsqrt(ms + EPS) * g_ref[...]).astype(o_ref.dtype)


def rmsnorm(x, g, out_dtype, tm=256):
    S, D = x.shape
    tm = min(tm, S)
    return pl.pallas_call(
        _rmsnorm_body,
        grid=(S // tm,),
        in_specs=[pl.BlockSpec((tm, D), lambda i: (i, 0)),
                  pl.BlockSpec((1, D), lambda i: (0, 0))],
        out_specs=pl.BlockSpec((tm, D), lambda i: (i, 0)),
        out_shape=jax.ShapeDtypeStruct((S, D), out_dtype),
        compiler_params=_cparams(("parallel",)),
    )(x, g.reshape(1, D))


def _mm_body(a_ref, b_ref, *rest, nk, epilogue):
    if epilogue == "residual":
        res_ref, o_ref, acc_ref = rest
    else:
        o_ref, acc_ref = rest
    k = pl.program_id(2)

    @pl.when(k == 0)
    def _():
        acc_ref[...] = jnp.zeros_like(acc_ref)

    acc_ref[...] += jnp.dot(a_ref[...], b_ref[...], preferred_element_type=F32)

    @pl.when(k == nk - 1)
    def _():
        acc = acc_ref[...]
        if epilogue == "relu2":
            r = jnp.maximum(acc, 0.0)
            acc = r * r
        elif epilogue == "residual":
            acc = acc + res_ref[...]
        o_ref[...] = acc.astype(o_ref.dtype)


def _pick(n, pref):
    for t in pref:
        if n % t == 0:
            return t
    return n


def matmul(a, b, out_dtype, epilogue="none", residual=None):
    M, K = a.shape
    _, N = b.shape
    tm = _pick(M, (1024, 512, 256))
    tn = _pick(N, (1024, 768, 512, 384, 256, 128))
    tk = _pick(K, (2048, 1024, 512))
    nk = K // tk
    in_specs = [pl.BlockSpec((tm, tk), lambda i, j, k: (i, k)),
                pl.BlockSpec((tk, tn), lambda i, j, k: (k, j))]
    args = [a, b]
    if epilogue == "residual":
        in_specs.append(pl.BlockSpec((tm, tn), lambda i, j, k: (i, j)))
        args.append(residual)
    return pl.pallas_call(
        functools.partial(_mm_body, nk=nk, epilogue=epilogue),
        grid=(M // tm, N // tn, nk),
        in_specs=in_specs,
        out_specs=pl.BlockSpec((tm, tn), lambda i, j, k: (i, j)),
        out_shape=jax.ShapeDtypeStruct((M, N), out_dtype),
        scratch_shapes=[pltpu.VMEM((tm, tn), F32)],
        compiler_params=_cparams(("parallel", "parallel", "arbitrary")),
    )(*args)


def _mm_ws_body(x_ref, w_ref, *rest, epilogue):
    if epilogue == "residual":
        res_ref, o_ref, wb_ref = rest
    else:
        o_ref, wb_ref = rest

    @pl.when(pl.program_id(1) == 0)
    def _():
        wb_ref[...] = w_ref[...].astype(BF16)

    acc = jnp.dot(x_ref[...], wb_ref[...], preferred_element_type=F32)
    if epilogue == "relu2":
        r = jnp.maximum(acc, 0.0)
        acc = r * r
    elif epilogue == "residual":
        acc = acc + res_ref[...]
    o_ref[...] = acc.astype(o_ref.dtype)


def matmul_ws(x, w3, layer, col0, ncols, out_dtype, tn, epilogue="none", residual=None, tm=1024):
    M, K = x.shape
    tm = min(tm, M)
    nj = -(-ncols // tn)
    jb = col0 // tn
    assert col0 % tn == 0 and M % tm == 0
    in_specs = [pl.BlockSpec((tm, K), lambda j, i: (i, 0)),
                pl.BlockSpec((None, K, tn), lambda j, i: (layer, 0, jb + j))]
    args = [x, w3]
    if epilogue == "residual":
        in_specs.append(pl.BlockSpec((tm, tn), lambda j, i: (i, j)))
        args.append(residual)
    return pl.pallas_call(
        functools.partial(_mm_ws_body, epilogue=epilogue),
        grid=(nj, M // tm),
        in_specs=in_specs,
        out_specs=pl.BlockSpec((tm, tn), lambda j, i: (i, j)),
        out_shape=jax.ShapeDtypeStruct((M, nj * tn), out_dtype),
        scratch_shapes=[pltpu.VMEM((K, tn), BF16)],
        compiler_params=_cparams(("parallel", "arbitrary")),
    )(*args)


def _diffattn_body(q_ref, k_ref, v_ref, lam_ref, g_ref, o_ref, m_ref, acc_ref, *, tq, lam_init):
    tk = tq
    i = pl.program_id(1)
    hw = 2 * A_HDIM
    q = q_ref[...] * (A_HDIM ** -0.5)
    lane = lax.broadcasted_iota(jnp.int32, (tq, hw), 1)
    zero = jnp.zeros_like(q)
    qz = jnp.concatenate([jnp.where(lane < A_HDIM, q, zero), jnp.where(lane >= A_HDIM, q, zero)], axis=0)
    m_ref[...] = jnp.full_like(m_ref, NEG)
    acc_ref[...] = jnp.zeros_like(acc_ref)
    ones = jnp.ones((tk, hw), BF16)

    def scores(j):
        k = k_ref[pl.ds(pl.multiple_of(j * tk, tk), tk), :]
        return lax.dot_general(qz, k, (((1,), (1,)), ((), ())), preferred_element_type=F32)

    def update(j, s, masked):
        r0 = pl.multiple_of(j * tk, tk)
        if masked:
            row = lax.broadcasted_iota(jnp.int32, (2 * tq, tk), 0)
            qpos = i * tq + jnp.where(row >= tq, row - tq, row)
            kpos = r0 + lax.broadcasted_iota(jnp.int32, (2 * tq, tk), 1)
            s = jnp.where(kpos <= qpos, s, NEG)
        m_old = m_ref[...]
        m_new = jnp.maximum(m_old, jnp.max(s, axis=-1, keepdims=True))
        alpha = jnp.exp(m_old - m_new)
        p = jnp.exp(s - m_new)
        v_ext = jnp.concatenate([v_ref[pl.ds(r0, tk), :], ones], axis=-1)
        acc_ref[...] = alpha * acc_ref[...] + jnp.dot(p.astype(BF16), v_ext, preferred_element_type=F32)
        m_ref[...] = m_new

    def body(j, s):
        s_next = scores(j + 1)
        update(j, s, False)
        return s_next

    s_diag = lax.fori_loop(0, i, body, scores(0))
    update(i, s_diag, True)

    lv = lam_ref[...]
    lam = (jnp.exp(jnp.sum(lv[0:1] * lv[1:2], axis=-1, keepdims=True))
           - jnp.exp(jnp.sum(lv[2:3] * lv[3:4], axis=-1, keepdims=True)) + lam_init)
    acc = acc_ref[...]
    on = acc[:, :hw] / acc[:, hw:]
    o = on[:tq] - lam * on[tq:]
    ms = jnp.mean(o * o, axis=-1, keepdims=True)
    o_ref[...] = (o * lax.rsqrt(ms + EPS) * g_ref[...] * (1.0 - lam_init)).astype(o_ref.dtype)


def diff_attention(qkv, lam_vecs, subln_g, lam_init, tq=512):
    S = qkv.shape[0]
    tq = min(tq, S)
    hw = 2 * A_HDIM
    return pl.pallas_call(
        functools.partial(_diffattn_body, tq=tq, lam_init=lam_init),
        grid=(A_HEADS, S // tq),
        in_specs=[pl.BlockSpec((tq, hw), lambda h, i: (i, h)),
                  pl.BlockSpec((S, hw), lambda h, i: (0, A_HEADS + h)),
                  pl.BlockSpec((S, hw), lambda h, i: (0, 2 * A_HEADS + h)),
                  pl.BlockSpec((4, A_HDIM), lambda h, i: (0, 0)),
                  pl.BlockSpec((1, hw), lambda h, i: (0, 0))],
        out_specs=pl.BlockSpec((tq, hw), lambda h, i: (i, h)),
        out_shape=jax.ShapeDtypeStruct((S, A_WIDTH), BF16),
        scratch_shapes=[pltpu.VMEM((2 * tq, 1), F32), pltpu.VMEM((2 * tq, 2 * hw), F32)],
        compiler_params=_cparams(("parallel", "parallel")),
    )(qkv, qkv, qkv, lam_vecs, subln_g.reshape(1, hw))


KEY_NEG_INF = -2139095041


def _float_key(s):
    b = pltpu.bitcast(s, jnp.int32)
    return b ^ ((b >> 31) & jnp.int32(0x7FFFFFFF))


def _dsa_index_body(ki_ref, qi_ref, wiT_ref, bias_ref, key_ref, *, qb, kc, topk, nkc_total):
    i = pl.program_id(0)
    nch = ((i + 1) * qb) // kc
    wi = wiT_ref[...] * (IDX_HEADS ** -0.5 * IDX_HDIM ** -0.5)
    qi = qi_ref[...]
    tpos = i * qb + lax.broadcasted_iota(jnp.int32, (kc, qb), 1)
    srow = lax.broadcasted_iota(jnp.int32, (kc, qb), 0)

    def score_chunk(c, carry):
        r0 = pl.multiple_of(c * kc, kc)
        kic = ki_ref[pl.ds(r0, kc), :]
        acc = jnp.zeros((kc, qb), F32)
        for h in range(IDX_HEADS):
            d = lax.dot_general(kic, qi[:, h * IDX_HDIM:(h + 1) * IDX_HDIM],
                                (((1,), (1,)), ((), ())), preferred_element_type=F32)
            acc = acc + jnp.maximum(d, 0.0) * wi[h:h + 1, :]
        acc = jnp.where(r0 + srow <= tpos, acc, -jnp.inf)
        key_ref[pl.ds(r0, kc), :] = _float_key(acc)
        return carry

    lax.fori_loop(0, nch, score_chunk, 0)

    def count_ge(cand):
        def body(c, cnt):
            r0 = pl.multiple_of(c * kc, kc)
            ge = (key_ref[pl.ds(r0, kc), :] >= cand).astype(jnp.int32)
            return cnt + jnp.sum(ge.reshape(kc // 8, 8, qb), axis=0)
        cnt8 = lax.fori_loop(0, nch, body, jnp.zeros((8, qb), jnp.int32))
        return jnp.sum(cnt8, axis=0, keepdims=True)

    def bit_step(it, tau):
        cand = tau + (jnp.int32(1) << (31 - it))
        return jnp.where(count_ge(cand) >= topk, cand, tau)

    tau = lax.fori_loop(0, 32, bit_step, jnp.full((1, qb), jnp.iinfo(jnp.int32).min, jnp.int32))
    tau = jnp.maximum(tau, KEY_NEG_INF + 1)

    def write_chunk(c, carry):
        r0 = pl.multiple_of(c * kc, kc)
        sel = key_ref[pl.ds(r0, kc), :] >= tau
        bias_ref[pl.ds(r0, kc), :] = jnp.where(sel, 0.0, NEG).astype(bias_ref.dtype)
        return carry

    lax.fori_loop(0, nch, write_chunk, 0)

    def fill_chunk(c, carry):
        r0 = pl.multiple_of(c * kc, kc)
        bias_ref[pl.ds(r0, kc), :] = jnp.full((kc, qb), NEG, bias_ref.dtype)
        return carry

    lax.fori_loop(nch, nkc_total, fill_chunk, 0)


def dsa_index(ki, qi, wiT, topk, qb=256, kc=256):
    S = ki.shape[0]
    qb = min(qb, S)
    kc = min(kc, qb)
    return pl.pallas_call(
        functools.partial(_dsa_index_body, qb=qb, kc=kc, topk=topk, nkc_total=S // kc),
        grid=(S // qb,),
        in_specs=[pl.BlockSpec((S, IDX_HDIM), lambda i: (0, 0)),
                  pl.BlockSpec((qb, IDX_HEADS * IDX_HDIM), lambda i: (i, 0)),
                  pl.BlockSpec((IDX_HEADS, qb), lambda i: (0, i))],
        out_specs=pl.BlockSpec((S, qb), lambda i: (0, i)),
        out_shape=jax.ShapeDtypeStruct((S, S), BF16),
        scratch_shapes=[pltpu.VMEM((S, qb), jnp.int32)],
        compiler_params=_cparams(("parallel",)),
    )(ki, qi, wiT)


def _dsa_attn_body(qi_ref, kj_ref, q_ref, k_ref, vT_ref, bias_ref, oT_ref, m_ref, l_ref, acc_ref, *, qb, kc):
    step = pl.program_id(0)
    i = qi_ref[step]
    j = kj_ref[step]
    scale = C_HDIM ** -0.5
    heads = [slice(h * C_HDIM, (h + 1) * C_HDIM) for h in range(C_HEADS)]

    @pl.when(j == 0)
    def _():
        m_ref[...] = jnp.full_like(m_ref, NEG)
        l_ref[...] = jnp.zeros_like(l_ref)
        acc_ref[...] = jnp.zeros_like(acc_ref)

    bias = bias_ref[...].astype(F32)
    sT = [lax.dot_general(k_ref[:, cs], q_ref[:, cs], (((1,), (1,)), ((), ())),
                          preferred_element_type=F32) * scale + bias for cs in heads]
    m_old = m_ref[...]
    m_new = jnp.maximum(m_old, jnp.concatenate([jnp.max(s, axis=0, keepdims=True) for s in sT], axis=0))
    alpha = jnp.exp(m_old - m_new)
    p = [jnp.exp(s - m_new[h:h + 1, :]) for h, s in enumerate(sT)]
    l_ref[...] = alpha * l_ref[...] + jnp.concatenate([jnp.sum(x, axis=0, keepdims=True) for x in p], axis=0)
    m_ref[...] = m_new
    for h, cs in enumerate(heads):
        acc_ref[cs, :] = alpha[h:h + 1, :] * acc_ref[cs, :] + jnp.dot(vT_ref[cs, :], p[h].astype(BF16),
                                                                       preferred_element_type=F32)

    @pl.when(j == ((i + 1) * qb - 1) // kc)
    def _():
        l = l_ref[...]
        for h, cs in enumerate(heads):
            oT_ref[cs, :] = (acc_ref[cs, :] / l[h:h + 1, :]).astype(oT_ref.dtype)


def dsa_attention(qkv, vT, bias, qb=256, kc=512):
    S = qkv.shape[0]
    qb = min(qb, S)
    kc = min(kc, S)
    pairs = [(i, j) for i in range(S // qb) for j in range(((i + 1) * qb - 1) // kc + 1)]
    qi = jnp.asarray([p_[0] for p_ in pairs], jnp.int32)
    kj = jnp.asarray([p_[1] for p_ in pairs], jnp.int32)
    grid_spec = pltpu.PrefetchScalarGridSpec(
        num_scalar_prefetch=2,
        grid=(len(pairs),),
        in_specs=[pl.BlockSpec((qb, C_WIDTH), lambda s, qi, kj: (qi[s], 0)),
                  pl.BlockSpec((kc, C_WIDTH), lambda s, qi, kj: (kj[s], 1)),
                  pl.BlockSpec((C_WIDTH, kc), lambda s, qi, kj: (0, kj[s])),
                  pl.BlockSpec((kc, qb), lambda s, qi, kj: (kj[s], qi[s]))],
        out_specs=pl.BlockSpec((C_WIDTH, qb), lambda s, qi, kj: (0, qi[s])),
        scratch_shapes=[pltpu.VMEM((C_HEADS, qb), F32), pltpu.VMEM((C_HEADS, qb), F32),
                        pltpu.VMEM((C_WIDTH, qb), F32)])
    return pl.pallas_call(
        functools.partial(_dsa_attn_body, qb=qb, kc=kc),
        grid_spec=grid_spec,
        out_shape=jax.ShapeDtypeStruct((C_WIDTH, S), BF16),
        compiler_params=_cparams(("arbitrary",)),
    )(qi, kj, qkv, qkv, vT, bias)


def _sigmoid(x):
    return 1.0 / (1.0 + jnp.exp(-x))


def _shift_rows(cur, prev_ref, first_tile):
    prev_row = jnp.where(first_tile, 0.0, prev_ref[7:8, :].astype(F32))
    rolled = pltpu.roll(cur, 1, axis=0)
    row = lax.broadcasted_iota(jnp.int32, cur.shape, 0)
    return jnp.where(row == 0, prev_row, rolled)


def _rwkv_lora_body(*refs, has_v):
    if has_v:
        (h_ref, hp_ref, mu_ref, w1_ref, a1_ref, g1_ref, v1_ref, ow_ref, oa_ref, og_ref, ov_ref) = refs
    else:
        (h_ref, hp_ref, mu_ref, w1_ref, a1_ref, g1_ref, ow_ref, oa_ref, og_ref) = refs
    i = pl.program_id(0)
    h = h_ref[...].astype(F32)
    dh = _shift_rows(h, hp_ref, i == 0) - h

    def lora(row, w_ref):
        xm = (h + dh * mu_ref[row:row + 1, :]).astype(BF16)
        return jnp.dot(xm, w_ref[...], preferred_element_type=F32)

    ow_ref[...] = jnp.tanh(lora(0, w1_ref))
    oa_ref[...] = lora(1, a1_ref)
    og_ref[...] = _sigmoid(lora(2, g1_ref))
    if has_v:
        ov_ref[...] = lora(3, v1_ref)


def rwkv_lora(h, mu, w1, a1, g1, v1, tm=256):
    S, D = h.shape
    tm = min(tm, S)
    has_v = v1 is not None
    ws = [w1, a1, g1] + ([v1] if has_v else [])
    full = lambda a: pl.BlockSpec(a.shape, lambda i: (0, 0))
    return pl.pallas_call(
        functools.partial(_rwkv_lora_body, has_v=has_v),
        grid=(S // tm,),
        in_specs=[pl.BlockSpec((tm, D), lambda i: (i, 0)),
                  pl.BlockSpec((8, D), lambda i: (jnp.maximum(i * (tm // 8) - 1, 0), 0)),
                  full(mu)] + [full(w) for w in ws],
        out_specs=[pl.BlockSpec((tm, w.shape[1]), lambda i: (i, 0)) for w in ws],
        out_shape=[jax.ShapeDtypeStruct((S, w.shape[1]), F32) for w in ws],
        compiler_params=_cparams(("parallel",)),
    )(h, h, mu, *ws)


(V_MU_R, V_MU_K, V_MU_V, V_W0, V_A0, V_KK, V_KA, V_RK, V_V0) = range(9)
N_VEC_ROWS = 16


def _wkv_pre_body(*refs, has_v, tt):
    if has_v:
        (r_ref, k_ref, v_ref, rp_ref, kp_ref, vp_ref, hw_ref, ha_ref, hg_ref, w2_ref, a2_ref, g2_ref,
         vec_ref, hv_ref, v2_ref, vf_ref,
         m_out, g_out, q_out, z_out, vout_ref, gate_ref, bonus_ref) = refs
    else:
        (r_ref, k_ref, v_ref, rp_ref, kp_ref, vp_ref, hw_ref, ha_ref, hg_ref, w2_ref, a2_ref, g2_ref,
         vec_ref,
         m_out, g_out, q_out, z_out, vout_ref, gate_ref, bonus_ref) = refs
    i = pl.program_id(0)
    first = i == 0
    C = WKV_CHUNK
    N = B_HDIM
    vec = lambda row: vec_ref[row:row + 1, :]

    r = r_ref[...]
    k = k_ref[...]
    v = v_ref[...]
    r = r + (_shift_rows(r, rp_ref, first) - r) * vec(V_MU_R)
    k = k + (_shift_rows(k, kp_ref, first) - k) * vec(V_MU_K)
    v = v + (_shift_rows(v, vp_ref, first) - v) * vec(V_MU_V)

    wl = vec(V_W0) + jnp.dot(hw_ref[...], w2_ref[...], preferred_element_type=F32, precision=HI)
    z = -wl
    softplus = jnp.maximum(z, 0.0) + jnp.log(1.0 + jnp.exp(-jnp.abs(z)))
    logw = -jnp.exp(-softplus - 0.5)
    bdot = lambda a_ref, b_ref: jnp.dot(a_ref[...].astype(BF16), b_ref[...].astype(BF16), preferred_element_type=F32)
    a_sig = _sigmoid(vec(V_A0) + bdot(ha_ref, a2_ref))
    gate = bdot(hg_ref, g2_ref)
    if has_v:
        mix = _sigmoid(vec(V_V0) + bdot(hv_ref, v2_ref))
        v = v + (vf_ref[...] - v) * mix
    vout_ref[...] = v
    gate_ref[...] = gate

    kk = k * vec(V_KK)
    k_new = k * (1.0 + (a_sig - 1.0) * vec(V_KA))
    rk = r * k_new * vec(V_RK)

    rowc = lax.broadcasted_iota(jnp.int32, (C, C), 0)
    colc = lax.broadcasted_iota(jnp.int32, (C, C), 1)
    tril_incl = (rowc >= colc).astype(F32)
    row2 = lax.broadcasted_iota(jnp.int32, (2 * C, 2 * C), 0)
    col2 = lax.broadcasted_iota(jnp.int32, (2 * C, 2 * C), 1)
    rr = jnp.where(row2 >= C, row2 - C, row2)
    cc = jnp.where(col2 >= C, col2 - C, col2)
    keep = (rr - cc) >= jnp.where(row2 >= C, 0, 1)
    eye_n = (lax.broadcasted_iota(jnp.int32, (N, N), 0) == lax.broadcasted_iota(jnp.int32, (N, N), 1)).astype(F32)

    bonus_parts = []
    per_head = []
    for hh in range(LANES // N):
        hs = slice(hh * N, (hh + 1) * N)
        kkh = kk[:, hs]
        nrm = jnp.sqrt(jnp.sum(kkh * kkh, axis=-1, keepdims=True))
        kkn = kkh / jnp.maximum(nrm, 1e-12)
        bonus_parts.append(jnp.sum(rk[:, hs], axis=-1, keepdims=True) * v[:, hs])
        per_head.append((r[:, hs], k_new[:, hs], v[:, hs], -kkn, kkn * a_sig[:, hs]))
    bonus_ref[...] = jnp.concatenate(bonus_parts, axis=-1)

    nh = LANES // N
    units = [(c, hh) for c in range(tt // C) for hh in range(nh)]
    ts_of = lambda c: slice(c * C, (c + 1) * C)
    hs_of = lambda hh: slice(hh * N, (hh + 1) * N)
    decays = []
    for c in range(tt // C):
        lw = logw[ts_of(c)]
        cum = jnp.dot(tril_incl, lw, preferred_element_type=F32, precision=HI)
        cum_last = cum[C - 1:C, :]
        decays.append((jnp.exp(cum), jnp.exp(cum - lw), jnp.exp(-cum), jnp.exp(cum_last - cum), jnp.exp(cum_last)))
    st = []
    for (c, hh) in units:
        e_in, e_ex, e_neg, e_end, gam = (t_[:, hs_of(hh)] for t_ in decays[c])
        rh, kh, vh, ah, bh = (t_[ts_of(c)] for t_ in per_head[hh])
        st.append(dict(At=ah * e_ex, Rt=rh * e_in, Bt=bh * e_neg, Kt=kh * e_neg, Bg=bh * e_end, Kg=kh * e_end,
                       v=vh, gam=gam))
    for u in st:
        left = jnp.concatenate([u["At"], u["Rt"]], axis=0).astype(BF16)
        right = jnp.concatenate([u["Bt"], u["Kt"]], axis=0).astype(BF16)
        AA = lax.dot_general(left, right, (((1,), (1,)), ((), ())), preferred_element_type=F32)
        u["AA"] = jnp.where(keep, AA, 0.0)
    for u in st:
        u["akv"] = jnp.dot(u["AA"][:C, C:].astype(BF16), u["v"].astype(BF16), preferred_element_type=F32)
    for u in st:
        u["XA"] = jnp.concatenate([u["At"], u["akv"], u["AA"][:C, :C]], axis=-1)
    for step in range(6):
        for u in st:
            XA = u["XA"]
            Ap = XA[:, 2 * N:].astype(BF16)
            if step < 5:
                prod = jnp.dot(Ap, XA.astype(BF16), preferred_element_type=F32)
                u["XA"] = jnp.concatenate([XA[:, :2 * N] + prod[:, :2 * N], prod[:, 2 * N:]], axis=-1)
            else:
                u["X"] = XA[:, :2 * N] + jnp.dot(Ap, XA[:, :2 * N].astype(BF16), preferred_element_type=F32)
    for u in st:
        lower = jnp.concatenate([jnp.zeros((C, N), F32), u["v"]], axis=-1)
        u["W2"] = jnp.concatenate([u["X"], lower], axis=0).astype(BF16)
    for u in st:
        bk = jnp.concatenate([u["Bg"], u["Kg"]], axis=0).astype(BF16)
        u["MG"] = lax.dot_general(bk, u["W2"], (((0,), (0,)), ((), ())), preferred_element_type=F32)
    for u in st:
        u["QZ"] = jnp.dot(u["AA"][C:, :].astype(BF16), u["W2"], preferred_element_type=F32)
    zn = jnp.zeros((N, N), F32)
    blockdiag = lambda x0, x1: jnp.concatenate([jnp.concatenate([x0, zn], axis=-1),
                                                jnp.concatenate([zn, x1], axis=-1)], axis=0)
    for c in range(tt // C):
        u0, u1 = st[c * nh:(c + 1) * nh]
        rows = slice(c * LANES, (c + 1) * LANES)
        m_out[rows, :] = blockdiag(eye_n * u0["gam"] + u0["MG"][:, :N],
                                   eye_n * u1["gam"] + u1["MG"][:, :N]).astype(m_out.dtype)
        g_out[rows, :] = blockdiag(u0["MG"][:, N:], u1["MG"][:, N:]).astype(g_out.dtype)
        q_out[ts_of(c), :] = jnp.concatenate([u["Rt"] + u["QZ"][:, :N] for u in (u0, u1)], axis=-1).astype(q_out.dtype)
        z_out[ts_of(c), :] = jnp.concatenate([u["QZ"][:, N:] for u in (u0, u1)], axis=-1)


def wkv_pre(proj_b, hids, w2s, vecs, v_first, tt=256):
    S = proj_b.shape[0]
    tt = min(tt, S)
    has_v = v_first is not None
    nb = B_WIDTH // LANES
    nstate_rows = (tt // WKV_CHUNK) * LANES
    tok = lambda off: pl.BlockSpec((tt, LANES), lambda i, p: (i, off + p))
    prev = lambda off: pl.BlockSpec((8, LANES), lambda i, p: (jnp.maximum(i * (tt // 8) - 1, 0), off + p))
    hid = lambda a: pl.BlockSpec((tt, a.shape[1]), lambda i, p: (i, 0))
    wcol = lambda a: pl.BlockSpec((a.shape[0], LANES), lambda i, p: (0, p))
    in_specs = [tok(0), tok(nb), tok(2 * nb), prev(0), prev(nb), prev(2 * nb),
                hid(hids[0]), hid(hids[1]), hid(hids[2]), wcol(w2s[0]), wcol(w2s[1]), wcol(w2s[2]),
                pl.BlockSpec((N_VEC_ROWS, LANES), lambda i, p: (0, p))]
    args = [proj_b] * 6 + list(hids[:3]) + list(w2s[:3]) + [vecs]
    if has_v:
        in_specs += [hid(hids[3]), wcol(w2s[3]), tok(0)]
        args += [hids[3], w2s[3], v_first]
    state_shape = jax.ShapeDtypeStruct(((S // WKV_CHUNK) * LANES, B_WIDTH), BF16)
    out_tok = pl.BlockSpec((tt, LANES), lambda i, p: (i, p))
    out_st = pl.BlockSpec((nstate_rows, LANES), lambda i, p: (i, p))
    return pl.pallas_call(
        functools.partial(_wkv_pre_body, has_v=has_v, tt=tt),
        grid=(S // tt, nb),
        in_specs=in_specs,
        out_specs=[out_st, out_st, out_tok, out_tok, out_tok, out_tok, out_tok],
        out_shape=[state_shape, state_shape, jax.ShapeDtypeStruct((S, B_WIDTH), BF16)]
                  + [jax.ShapeDtypeStruct((S, B_WIDTH), F32)] * 4,
        compiler_params=_cparams(("parallel", "parallel")),
    )(*args)


def _wkv_scan_body(m_ref, g_ref, q_ref, z_ref, gate_ref, bonus_ref, ln_ref, o_ref, h_ref, *, tt):
    i = pl.program_id(0)
    C = WKV_CHUNK
    N = B_HDIM
    npair = B_WIDTH // LANES

    @pl.when(i == 0)
    def _():
        h_ref[...] = jnp.zeros_like(h_ref)

    row = lax.broadcasted_iota(jnp.int32, (LANES, LANES), 0)
    col = lax.broadcasted_iota(jnp.int32, (LANES, LANES), 1)
    head_avg = jnp.where((row // N) == (col // N), 1.0 / N, 0.0).astype(BF16)

    def head_mean(x):
        hi = x.astype(BF16)
        lo = (x - hi.astype(F32)).astype(BF16)
        return (jnp.dot(hi, head_avg, preferred_element_type=F32)
                + jnp.dot(lo, head_avg, preferred_element_type=F32))

    for c in range(tt // C):
        ts = slice(c * C, (c + 1) * C)
        ss = slice(c * LANES, (c + 1) * LANES)
        ys = []
        for p in range(npair):
            ps = slice(p * LANES, (p + 1) * LANES)
            Hb = h_ref[p].astype(BF16)
            ys.append(jnp.dot(q_ref[ts, ps], Hb, preferred_element_type=F32) + z_ref[ts, ps])
            h_ref[p] = jnp.dot(m_ref[ss, ps], Hb, preferred_element_type=F32) + g_ref[ss, ps].astype(F32)
        for p in range(npair):
            ps = slice(p * LANES, (p + 1) * LANES)
            yc = ys[p] - head_mean(ys[p])
            var = head_mean(yc * yc)
            yn = yc * lax.rsqrt(var + B_GN_EPS) * ln_ref[0:1, ps] + ln_ref[1:2, ps]
            o_ref[ts, ps] = ((yn + bonus_ref[ts, ps]) * gate_ref[ts, ps]).astype(o_ref.dtype)


def wkv_scan(m, g, q, z, gate, bonus, ln, tt=256):
    S = q.shape[0]
    tt = min(tt, S)
    npair = B_WIDTH // LANES
    nstate_rows = (tt // WKV_CHUNK) * LANES
    tok = pl.BlockSpec((tt, B_WIDTH), lambda i: (i, 0))
    st = pl.BlockSpec((nstate_rows, B_WIDTH), lambda i: (i, 0))
    return pl.pallas_call(
        functools.partial(_wkv_scan_body, tt=tt),
        grid=(S // tt,),
        in_specs=[st, st, tok, tok, tok, tok, pl.BlockSpec((8, B_WIDTH), lambda i: (0, 0))],
        out_specs=tok,
        out_shape=jax.ShapeDtypeStruct((S, B_WIDTH), BF16),
        scratch_shapes=[pltpu.VMEM((npair, LANES, LANES), F32)],
        compiler_params=_cparams(("arbitrary",)),
    )(m, g, q, z, gate, bonus, ln)


def rwkv7_mixer(h, proj_b, p, v_first):
    has_v = v_first is not None
    mu = p["mu_wag"] if not has_v else jnp.concatenate([p["mu_wag"], p["v_mu"][None]], axis=0)
    hids = rwkv_lora(h, mu, p["w1"].astype(BF16), p["a1"].astype(BF16), p["g1"].astype(BF16),
                     p["v1"].astype(BF16) if has_v else None)
    rows = [p["mu_rkv"][0], p["mu_rkv"][1], p["mu_rkv"][2], p["w0"], p["a0"], p["k_k"], p["k_a"],
            p["r_k"].reshape(-1), p["v0"] if has_v else jnp.zeros((B_WIDTH,), F32)]
    vecs = jnp.concatenate([jnp.stack(rows), jnp.zeros((N_VEC_ROWS - len(rows), B_WIDTH), F32)], axis=0)
    w2s = [p["w2"], p["a2"], p["g2"]] + ([p["v2"]] if has_v else [])
    m, g, q, z, v_out, gate, bonus = wkv_pre(proj_b, hids, w2s, vecs, v_first)
    ln = jnp.concatenate([p["ln_w"][None], p["ln_b"][None], jnp.zeros((6, B_WIDTH), F32)], axis=0)
    return wkv_scan(m, g, q, z, gate, bonus, ln), v_out


def kernel(x, norm_mix_g, w_in, lam_q1, lam_k1, lam_q2, lam_k2, diff_subln_g, rw_mu_rkv, rw_mu_wag, rw_w0, rw_w1, rw_w2, rw_a0, rw_a1, rw_a2, rw_g1, rw_g2, rw_k_k, rw_k_a, rw_r_k, rw_ln_w, rw_ln_b, rw_v_mu, rw_v0, rw_v1, rw_v2, w_out, norm_ffn_g, w_up, w_down, norm_final_g):
    Bsz, S, D = x.shape
    depth = w_in.shape[0]
    topk = min(TOPK_MAX, S // 4)
    outs = []
    for b in range(Bsz):
        xb = x[b]
        v_first = None
        for l in range(depth):
            h = rmsnorm(xb, norm_mix_g[l], BF16)
            oA, oB, oC = 3 * A_WIDTH, 3 * A_WIDTH + 3 * B_WIDTH, 3 * A_WIDTH + 3 * B_WIDTH + 3 * C_WIDTH
            proj_a = matmul_ws(h, w_in, l, 0, oA, BF16, tn=512)
            proj_b = matmul_ws(h, w_in, l, oA, oB - oA, F32, tn=512)
            proj_c = matmul_ws(h, w_in, l, oB, oC - oB, BF16, tn=512)
            proj_i = matmul_ws(h, w_in, l, oC, w_in.shape[2] - oC, F32, tn=384)

            lam_init = 0.8 - 0.6 * math.exp(-0.3 * l)
            lam_vecs = jnp.stack([lam_q1[l], lam_k1[l], lam_q2[l], lam_k2[l]])
            o_a = diff_attention(proj_a, lam_vecs, diff_subln_g[l], lam_init)

            p = dict(mu_rkv=rw_mu_rkv[l], mu_wag=rw_mu_wag[l], w0=rw_w0[l], w1=rw_w1[l], w2=rw_w2[l],
                     a0=rw_a0[l], a1=rw_a1[l], a2=rw_a2[l], g1=rw_g1[l], g2=rw_g2[l], k_k=rw_k_k[l],
                     k_a=rw_k_a[l], r_k=rw_r_k[l], ln_w=rw_ln_w[l], ln_b=rw_ln_b[l])
            if l > 0:
                p.update(v_mu=rw_v_mu[l - 1], v0=rw_v0[l - 1], v1=rw_v1[l - 1], v2=rw_v2[l - 1])
            o_b, v_out = rwkv7_mixer(h, proj_b, p, v_first if l > 0 else None)
            if l == 0:
                v_first = v_out

            nq = IDX_HEADS * IDX_HDIM
            qi = proj_i[:, :nq].astype(BF16)
            ki = proj_i[:, nq:nq + IDX_HDIM].astype(BF16)
            wiT = proj_i[:, nq + IDX_HDIM:nq + IDX_HDIM + IDX_HEADS].T
            bias = dsa_index(ki, qi, wiT, topk)
            vT = proj_c[:, 2 * C_WIDTH:].T
            o_c = dsa_attention(proj_c, vT, bias).T

            mixed = jnp.concatenate([o_a, o_b, o_c], axis=-1)
            xb = matmul_ws(mixed, w_out, l, 0, D, F32, tn=512, epilogue="residual", residual=xb)
            h2 = rmsnorm(xb, norm_ffn_g[l], BF16)
            up = matmul_ws(h2, w_up, l, 0, w_up.shape[2], BF16, tn=512, epilogue="relu2")
            xb = matmul(up, w_down[l].astype(BF16), F32, epilogue="residual", residual=xb)
        outs.append(rmsnorm(xb, norm_final_g, F32))
    return jnp.stack(outs)
```

```python
import functools
import math

import jax
import jax.numpy as jnp
from jax import lax
from jax.experimental import pallas as pl
from jax.experimental.pallas import tpu as pltpu

F32 = jnp.float32
BF16 = jnp.bfloat16

A_HEADS, A_HDIM = 8, 64
A_WIDTH = A_HEADS * 2 * A_HDIM
B_HDIM, B_WIDTH = 64, 2048
B_HEADS = B_WIDTH // B_HDIM
B_GN_EPS = 64e-5
C_HEADS, C_HDIM = 8, 128
C_WIDTH = C_HEADS * C_HDIM
IDX_HEADS, IDX_HDIM = 16, 64
TOPK_MAX = 256
EPS = 1e-6

LANES = 128
VMEM_LIMIT = 56 * 1024 * 1024
NEG = -1e30
WKV_CHUNK = 64
HI = lax.Precision.HIGHEST


def _cparams(sem):
    return pltpu.CompilerParams(dimension_semantics=sem, vmem_limit_bytes=VMEM_LIMIT)


def _rmsnorm_body(x_ref, g_ref, o_ref):
    x = x_ref[...]
    ms = jnp.mean(x * x, axis=-1, keepdims=True)
    o_ref[...] = (x * lax.rsqrt(ms + EPS) * g_ref[...]).astype(o_ref.dtype)


def rmsnorm(x, g, out_dtype, tm=256):
    S, D = x.shape
    tm = min(tm, S)
    return pl.pallas_call(
        _rmsnorm_body,
        grid=(S // tm,),
        in_specs=[pl.BlockSpec((tm, D), lambda i: (i, 0)),
                  pl.BlockSpec((1, D), lambda i: (0, 0))],
        out_specs=pl.BlockSpec((tm, D), lambda i: (i, 0)),
        out_shape=jax.ShapeDtypeStruct((S, D), out_dtype),
        compiler_params=_cparams(("parallel",)),
    )(x, g.reshape(1, D))


def _mm_body(a_ref, b_ref, *rest, nk, epilogue):
    if epilogue == "residual":
        res_ref, o_ref, acc_ref = rest
    else:
        o_ref, acc_ref = rest
    k = pl.program_id(2)

    @pl.when(k == 0)
    def _():
        acc_ref[...] = jnp.zeros_like(acc_ref)

    acc_ref[...] += jnp.dot(a_ref[...], b_ref[...], preferred_element_type=F32)

    @pl.when(k == nk - 1)
    def _():
        acc = acc_ref[...]
        if epilogue == "relu2":
            r = jnp.maximum(acc, 0.0)
            acc = r * r
        elif epilogue == "residual":
            acc = acc + res_ref[...]
        o_ref[...] = acc.astype(o_ref.dtype)


def _pick(n, pref):
    for t in pref:
        if n % t == 0:
            return t
    return n


def matmul(a, b3, layer, out_dtype, epilogue="none", residual=None):
    M, K = a.shape
    _, _, N = b3.shape
    tm = _pick(M, (1024, 512, 256))
    tn = _pick(N, (1024, 768, 512, 384, 256, 128))
    tk = _pick(K, (2048, 1024, 512))
    nk = K // tk
    in_specs = [pl.BlockSpec((tm, tk), lambda i, j, k: (i, k)),
                pl.BlockSpec((None, tk, tn), lambda i, j, k: (layer, k, j))]
    args = [a, b3]
    if epilogue == "residual":
        in_specs.append(pl.BlockSpec((tm, tn), lambda i, j, k: (i, j)))
        args.append(residual)
    return pl.pallas_call(
        functools.partial(_mm_body, nk=nk, epilogue=epilogue),
        grid=(M // tm, N // tn, nk),
        in_specs=in_specs,
        out_specs=pl.BlockSpec((tm, tn), lambda i, j, k: (i, j)),
        out_shape=jax.ShapeDtypeStruct((M, N), out_dtype),
        scratch_shapes=[pltpu.VMEM((tm, tn), F32)],
        compiler_params=_cparams(("parallel", "parallel", "arbitrary")),
    )(*args)


def _mm_ws_body(x_ref, w_ref, *rest, epilogue, w_transposed):
    if epilogue == "residual":
        res_ref, o_ref, wb_ref = rest
    else:
        o_ref, wb_ref = rest

    @pl.when(pl.program_id(1) == 0)
    def _():
        wb_ref[...] = w_ref[...].astype(BF16)

    contract = (((1,), (1,)), ((), ())) if w_transposed else (((1,), (0,)), ((), ()))
    acc = lax.dot_general(x_ref[...], wb_ref[...], contract, preferred_element_type=F32)
    if epilogue == "relu2":
        r = jnp.maximum(acc, 0.0)
        acc = r * r
    elif epilogue == "residual":
        acc = acc + res_ref[...]
    o_ref[...] = acc.astype(o_ref.dtype)


def matmul_ws(x, w3, layer, col0, ncols, out_dtype, tn, epilogue="none", residual=None, tm=1024, w_transposed=False):
    M, K = x.shape
    tm = min(tm, M)
    nj = -(-ncols // tn)
    jb = col0 // tn
    assert col0 % tn == 0 and M % tm == 0
    if w_transposed:
        w_spec = pl.BlockSpec((None, tn, K), lambda j, i: (layer, jb + j, 0))
        wb_shape = (tn, K)
    else:
        w_spec = pl.BlockSpec((None, K, tn), lambda j, i: (layer, 0, jb + j))
        wb_shape = (K, tn)
    in_specs = [pl.BlockSpec((tm, K), lambda j, i: (i, 0)), w_spec]
    args = [x, w3]
    if epilogue == "residual":
        in_specs.append(pl.BlockSpec((tm, tn), lambda j, i: (i, j)))
        args.append(residual)
    return pl.pallas_call(
        functools.partial(_mm_ws_body, epilogue=epilogue, w_transposed=w_transposed),
        grid=(nj, M // tm),
        in_specs=in_specs,
        out_specs=pl.BlockSpec((tm, tn), lambda j, i: (i, j)),
        out_shape=jax.ShapeDtypeStruct((M, nj * tn), out_dtype),
        scratch_shapes=[pltpu.VMEM(wb_shape, BF16)],
        compiler_params=_cparams(("parallel", "arbitrary")),
    )(*args)


def _diffattn_body(q_ref, k_ref, v_ref, lam_ref, g_ref, o_ref, m_ref, acc_ref, *, tq, lam_init):
    tk = tq
    i = pl.program_id(1)
    hw = 2 * A_HDIM
    q = q_ref[...] * (A_HDIM ** -0.5)
    lane = lax.broadcasted_iota(jnp.int32, (tq, hw), 1)
    zero = jnp.zeros_like(q)
    qz = jnp.concatenate([jnp.where(lane < A_HDIM, q, zero), jnp.where(lane >= A_HDIM, q, zero)], axis=0)
    m_ref[...] = jnp.full_like(m_ref, NEG)
    acc_ref[...] = jnp.zeros_like(acc_ref)
    ones = jnp.ones((tk, hw), BF16)

    def scores(j):
        k = k_ref[pl.ds(pl.multiple_of(j * tk, tk), tk), :]
        return lax.dot_general(qz, k, (((1,), (1,)), ((), ())), preferred_element_type=F32)

    def update(j, s, masked):
        r0 = pl.multiple_of(j * tk, tk)
        if masked:
            row = lax.broadcasted_iota(jnp.int32, (2 * tq, tk), 0)
            qpos = i * tq + jnp.where(row >= tq, row - tq, row)
            kpos = r0 + lax.broadcasted_iota(jnp.int32, (2 * tq, tk), 1)
            s = jnp.where(kpos <= qpos, s, NEG)
        m_old = m_ref[...]
        m_new = jnp.maximum(m_old, jnp.max(s, axis=-1, keepdims=True))
        alpha = jnp.exp(m_old - m_new)
        p = jnp.exp(s - m_new)
        v_ext = jnp.concatenate([v_ref[pl.ds(r0, tk), :], ones], axis=-1)
        acc_ref[...] = alpha * acc_ref[...] + jnp.dot(p.astype(BF16), v_ext, preferred_element_type=F32)
        m_ref[...] = m_new

    def body(j, s):
        s_next = scores(j + 1)
        update(j, s, False)
        return s_next

    s_diag = lax.fori_loop(0, i, body, scores(0))
    update(i, s_diag, True)

    lv = lam_ref[...]
    lam = (jnp.exp(jnp.sum(lv[0:1] * lv[1:2], axis=-1, keepdims=True))
           - jnp.exp(jnp.sum(lv[2:3] * lv[3:4], axis=-1, keepdims=True)) + lam_init)
    acc = acc_ref[...]
    on = acc[:, :hw] / acc[:, hw:]
    o = on[:tq] - lam * on[tq:]
    ms = jnp.mean(o * o, axis=-1, keepdims=True)
    o_ref[...] = (o * lax.rsqrt(ms + EPS) * g_ref[...] * (1.0 - lam_init)).astype(o_ref.dtype)


def diff_attention(qkv, lam_vecs, subln_g, lam_init, tq=512):
    S = qkv.shape[0]
    tq = min(tq, S)
    hw = 2 * A_HDIM
    return pl.pallas_call(
        functools.partial(_diffattn_body, tq=tq, lam_init=lam_init),
        grid=(A_HEADS, S // tq),
        in_specs=[pl.BlockSpec((tq, hw), lambda h, i: (i, h)),
                  pl.BlockSpec((S, hw), lambda h, i: (0, A_HEADS + h)),
                  pl.BlockSpec((S, hw), lambda h, i: (0, 2 * A_HEADS + h)),
                  pl.BlockSpec((4, A_HDIM), lambda h, i: (0, 0)),
                  pl.BlockSpec((1, hw), lambda h, i: (0, 0))],
        out_specs=pl.BlockSpec((tq, hw), lambda h, i: (i, h)),
        out_shape=jax.ShapeDtypeStruct((S, A_WIDTH), BF16),
        scratch_shapes=[pltpu.VMEM((2 * tq, 1), F32), pltpu.VMEM((2 * tq, 2 * hw), F32)],
        compiler_params=_cparams(("parallel", "parallel")),
    )(qkv, qkv, qkv, lam_vecs, subln_g.reshape(1, hw))


KEY_NEG_INF = -2139095041


def _float_key(s):
    b = pltpu.bitcast(s, jnp.int32)
    return b ^ ((b >> 31) & jnp.int32(0x7FFFFFFF))


def _dsa_index_body(ki_ref, qi_ref, wiT_ref, bias_ref, key_ref, *, qb, kc, topk, nkc_total):
    i = pl.program_id(0)
    nch = ((i + 1) * qb) // kc
    wi = wiT_ref[...] * (IDX_HEADS ** -0.5 * IDX_HDIM ** -0.5)
    qi = qi_ref[...]
    tpos = i * qb + lax.broadcasted_iota(jnp.int32, (kc, qb), 1)
    srow = lax.broadcasted_iota(jnp.int32, (kc, qb), 0)

    def score_chunk(c, carry):
        r0 = pl.multiple_of(c * kc, kc)
        kic = ki_ref[pl.ds(r0, kc), :]
        acc = jnp.zeros((kc, qb), F32)
        for h in range(IDX_HEADS):
            d = lax.dot_general(kic, qi[:, h * IDX_HDIM:(h + 1) * IDX_HDIM],
                                (((1,), (1,)), ((), ())), preferred_element_type=F32)
            acc = acc + jnp.maximum(d, 0.0) * wi[h:h + 1, :]
        acc = jnp.where(r0 + srow <= tpos, acc, -jnp.inf)
        key_ref[pl.ds(r0, kc), :] = _float_key(acc)
        return carry

    lax.fori_loop(0, nch, score_chunk, 0)

    def count_ge(cand):
        def body(c, cnt):
            r0 = pl.multiple_of(c * kc, kc)
            ge = (key_ref[pl.ds(r0, kc), :] >= cand).astype(jnp.int32)
            return cnt + jnp.sum(ge.reshape(kc // 8, 8, qb), axis=0)
        cnt8 = lax.fori_loop(0, nch, body, jnp.zeros((8, qb), jnp.int32))
        return jnp.sum(cnt8, axis=0, keepdims=True)

    def bit_step(it, tau):
        cand = tau + (jnp.int32(1) << (31 - it))
        return jnp.where(count_ge(cand) >= topk, cand, tau)

    tau = lax.fori_loop(0, 32, bit_step, jnp.full((1, qb), jnp.iinfo(jnp.int32).min, jnp.int32))
    tau = jnp.maximum(tau, KEY_NEG_INF + 1)

    def write_chunk(c, carry):
        r0 = pl.multiple_of(c * kc, kc)
        sel = key_ref[pl.ds(r0, kc), :] >= tau
        bias_ref[pl.ds(r0, kc), :] = jnp.where(sel, 0.0, NEG).astype(bias_ref.dtype)
        return carry

    lax.fori_loop(0, nch, write_chunk, 0)

    def fill_chunk(c, carry):
        r0 = pl.multiple_of(c * kc, kc)
        bias_ref[pl.ds(r0, kc), :] = jnp.full((kc, qb), NEG, bias_ref.dtype)
        return carry

    lax.fori_loop(nch, nkc_total, fill_chunk, 0)


def dsa_index(ki, qi, wiT, topk, qb=256, kc=256):
    S = ki.shape[0]
    qb = min(qb, S)
    kc = min(kc, qb)
    return pl.pallas_call(
        functools.partial(_dsa_index_body, qb=qb, kc=kc, topk=topk, nkc_total=S // kc),
        grid=(S // qb,),
        in_specs=[pl.BlockSpec((S, IDX_HDIM), lambda i: (0, 0)),
                  pl.BlockSpec((qb, IDX_HEADS * IDX_HDIM), lambda i: (i, 0)),
                  pl.BlockSpec((IDX_HEADS, qb), lambda i: (0, i))],
        out_specs=pl.BlockSpec((S, qb), lambda i: (0, i)),
        out_shape=jax.ShapeDtypeStruct((S, S), BF16),
        scratch_shapes=[pltpu.VMEM((S, qb), jnp.int32)],
        compiler_params=_cparams(("parallel",)),
    )(ki, qi, wiT)


def _dsa_attn_body(qi_ref, kj_ref, q_ref, k_ref, vT_ref, bias_ref, oT_ref, m_ref, l_ref, acc_ref, *, qb, kc):
    step = pl.program_id(0)
    i = qi_ref[step]
    j = kj_ref[step]
    scale = C_HDIM ** -0.5
    heads = [slice(h * C_HDIM, (h + 1) * C_HDIM) for h in range(C_HEADS)]

    @pl.when(j == 0)
    def _():
        m_ref[...] = jnp.full_like(m_ref, NEG)
        l_ref[...] = jnp.zeros_like(l_ref)
        acc_ref[...] = jnp.zeros_like(acc_ref)

    bias = bias_ref[...].astype(F32)
    sT = [lax.dot_general(k_ref[:, cs], q_ref[:, cs], (((1,), (1,)), ((), ())),
                          preferred_element_type=F32) * scale + bias for cs in heads]
    m_old = m_ref[...]
    m_new = jnp.maximum(m_old, jnp.concatenate([jnp.max(s, axis=0, keepdims=True) for s in sT], axis=0))
    alpha = jnp.exp(m_old - m_new)
    p = [jnp.exp(s - m_new[h:h + 1, :]) for h, s in enumerate(sT)]
    l_ref[...] = alpha * l_ref[...] + jnp.concatenate([jnp.sum(x, axis=0, keepdims=True) for x in p], axis=0)
    m_ref[...] = m_new
    for h, cs in enumerate(heads):
        acc_ref[cs, :] = alpha[h:h + 1, :] * acc_ref[cs, :] + jnp.dot(vT_ref[cs, :], p[h].astype(BF16),
                                                                       preferred_element_type=F32)

    @pl.when(j == ((i + 1) * qb - 1) // kc)
    def _():
        l = l_ref[...]
        for h, cs in enumerate(heads):
            oT_ref[cs, :] = (acc_ref[cs, :] / l[h:h + 1, :]).astype(oT_ref.dtype)


def dsa_attention(qkv, vT, bias, qb=256, kc=512):
    S = qkv.shape[0]
    qb = min(qb, S)
    kc = min(kc, S)
    pairs = [(i, j) for i in range(S // qb) for j in range(((i + 1) * qb - 1) // kc + 1)]
    qi = jnp.asarray([p_[0] for p_ in pairs], jnp.int32)
    kj = jnp.asarray([p_[1] for p_ in pairs], jnp.int32)
    grid_spec = pltpu.PrefetchScalarGridSpec(
        num_scalar_prefetch=2,
        grid=(len(pairs),),
        in_specs=[pl.BlockSpec((qb, C_WIDTH), lambda s, qi, kj: (qi[s], 0)),
                  pl.BlockSpec((kc, C_WIDTH), lambda s, qi, kj: (kj[s], 1)),
                  pl.BlockSpec((C_WIDTH, kc), lambda s, qi, kj: (0, kj[s])),
                  pl.BlockSpec((kc, qb), lambda s, qi, kj: (kj[s], qi[s]))],
        out_specs=pl.BlockSpec((C_WIDTH, qb), lambda s, qi, kj: (0, qi[s])),
        scratch_shapes=[pltpu.VMEM((C_HEADS, qb), F32), pltpu.VMEM((C_HEADS, qb), F32),
                        pltpu.VMEM((C_WIDTH, qb), F32)])
    return pl.pallas_call(
        functools.partial(_dsa_attn_body, qb=qb, kc=kc),
        grid_spec=grid_spec,
        out_shape=jax.ShapeDtypeStruct((C_WIDTH, S), BF16),
        compiler_params=_cparams(("arbitrary",)),
    )(qi, kj, qkv, qkv, vT, bias)


def _sigmoid(x):
    return 1.0 / (1.0 + jnp.exp(-x))


def _shift_rows(cur, prev_ref, first_tile):
    prev_row = jnp.where(first_tile, 0.0, prev_ref[7:8, :].astype(F32))
    rolled = pltpu.roll(cur, 1, axis=0)
    row = lax.broadcasted_iota(jnp.int32, cur.shape, 0)
    return jnp.where(row == 0, prev_row, rolled)


def _rwkv_lora_body(*refs, has_v):
    if has_v:
        (h_ref, hp_ref, mu_ref, w1_ref, a1_ref, g1_ref, v1_ref, ow_ref, oa_ref, og_ref, ov_ref) = refs
    else:
        (h_ref, hp_ref, mu_ref, w1_ref, a1_ref, g1_ref, ow_ref, oa_ref, og_ref) = refs
    i = pl.program_id(0)
    h = h_ref[...].astype(F32)
    dh = _shift_rows(h, hp_ref, i == 0) - h

    def lora(row, w_ref):
        xm = (h + dh * mu_ref[row:row + 1, :]).astype(BF16)
        return jnp.dot(xm, w_ref[...], preferred_element_type=F32)

    ow_ref[...] = jnp.tanh(lora(0, w1_ref))
    oa_ref[...] = lora(1, a1_ref)
    og_ref[...] = _sigmoid(lora(2, g1_ref))
    if has_v:
        ov_ref[...] = lora(3, v1_ref)


def rwkv_lora(h, mu, w1, a1, g1, v1, tm=256):
    S, D = h.shape
    tm = min(tm, S)
    has_v = v1 is not None
    ws = [w1, a1, g1] + ([v1] if has_v else [])
    full = lambda a: pl.BlockSpec(a.shape, lambda i: (0, 0))
    return pl.pallas_call(
        functools.partial(_rwkv_lora_body, has_v=has_v),
        grid=(S // tm,),
        in_specs=[pl.BlockSpec((tm, D), lambda i: (i, 0)),
                  pl.BlockSpec((8, D), lambda i: (jnp.maximum(i * (tm // 8) - 1, 0), 0)),
                  full(mu)] + [full(w) for w in ws],
        out_specs=[pl.BlockSpec((tm, w.shape[1]), lambda i: (i, 0)) for w in ws],
        out_shape=[jax.ShapeDtypeStruct((S, w.shape[1]), F32) for w in ws],
        compiler_params=_cparams(("parallel",)),
    )(h, h, mu, *ws)


(V_MU_R, V_MU_K, V_MU_V, V_W0, V_A0, V_KK, V_KA, V_RK, V_V0) = range(9)
N_VEC_ROWS = 16


def _wkv_pre_body(*refs, has_v, tt):
    if has_v:
        (r_ref, k_ref, v_ref, rp_ref, kp_ref, vp_ref, hw_ref, ha_ref, hg_ref, w2_ref, a2_ref, g2_ref,
         vec_ref, hv_ref, v2_ref, vf_ref,
         m_out, g_out, q_out, z_out, vout_ref, gate_ref, bonus_ref) = refs
    else:
        (r_ref, k_ref, v_ref, rp_ref, kp_ref, vp_ref, hw_ref, ha_ref, hg_ref, w2_ref, a2_ref, g2_ref,
         vec_ref,
         m_out, g_out, q_out, z_out, vout_ref, gate_ref, bonus_ref) = refs
    i = pl.program_id(0)
    first = i == 0
    C = WKV_CHUNK
    N = B_HDIM
    vec = lambda row: vec_ref[row:row + 1, :]

    r = r_ref[...]
    k = k_ref[...]
    v = v_ref[...]
    r = r + (_shift_rows(r, rp_ref, first) - r) * vec(V_MU_R)
    k = k + (_shift_rows(k, kp_ref, first) - k) * vec(V_MU_K)
    v = v + (_shift_rows(v, vp_ref, first) - v) * vec(V_MU_V)

    wl = vec(V_W0) + jnp.dot(hw_ref[...], w2_ref[...], preferred_element_type=F32, precision=HI)
    z = -wl
    softplus = jnp.maximum(z, 0.0) + jnp.log(1.0 + jnp.exp(-jnp.abs(z)))
    logw = -jnp.exp(-softplus - 0.5)
    bdot = lambda a_ref, b_ref: jnp.dot(a_ref[...].astype(BF16), b_ref[...].astype(BF16), preferred_element_type=F32)
    a_sig = _sigmoid(vec(V_A0) + bdot(ha_ref, a2_ref))
    gate = bdot(hg_ref, g2_ref)
    if has_v:
        mix = _sigmoid(vec(V_V0) + bdot(hv_ref, v2_ref))
        v = v + (vf_ref[...] - v) * mix
    vout_ref[...] = v
    gate_ref[...] = gate

    kk = k * vec(V_KK)
    k_new = k * (1.0 + (a_sig - 1.0) * vec(V_KA))
    rk = r * k_new * vec(V_RK)

    rowc = lax.broadcasted_iota(jnp.int32, (C, C), 0)
    colc = lax.broadcasted_iota(jnp.int32, (C, C), 1)
    tril_incl = (rowc >= colc).astype(F32)
    prow = lax.broadcasted_iota(jnp.int32, (2 * C, 2 * C), 0)
    pcol = lax.broadcasted_iota(jnp.int32, (2 * C, 2 * C), 1)
    same_head = (prow >= C) == (pcol >= C)
    dstep = jnp.where(prow >= C, prow - C, prow) - jnp.where(pcol >= C, pcol - C, pcol)
    strict_bd = jnp.where(same_head, dstep, -1) > 0
    incl_bd = jnp.where(same_head, dstep, -1) >= 0
    eye_p = (prow == pcol).astype(F32)
    head_ones = jnp.where(same_head, 1.0, 0.0).astype(BF16)

    def head_sum(x):
        hi = x.astype(BF16)
        lo = (x - hi.astype(F32)).astype(BF16)
        return (jnp.dot(hi, head_ones, preferred_element_type=F32)
                + jnp.dot(lo, head_ones, preferred_element_type=F32))

    kkn = kk / jnp.maximum(jnp.sqrt(head_sum(kk * kk)), 1e-12)
    bonus_ref[...] = head_sum(rk) * v
    a_scan = -kkn
    b_scan = kkn * a_sig

    head0 = lax.broadcasted_iota(jnp.int32, (C, LANES), 1) < N

    def stack2(x):
        return jnp.concatenate([jnp.where(head0, x, 0.0), jnp.where(head0, 0.0, x)], axis=0)

    ts_of = lambda c: slice(c * C, (c + 1) * C)
    st = []
    for c in range(tt // C):
        ts = ts_of(c)
        lw = logw[ts]
        cum = jnp.dot(tril_incl, lw, preferred_element_type=F32, precision=HI)
        cum_last = cum[C - 1:C, :]
        e_neg = jnp.exp(-cum)
        e_end = jnp.exp(cum_last - cum)
        st.append(dict(At=stack2(a_scan[ts] * jnp.exp(cum - lw)), Rt=stack2(r[ts] * jnp.exp(cum)),
                       Bt=b_scan[ts] * e_neg, Kt=k_new[ts] * e_neg,
                       Bg=stack2(b_scan[ts] * e_end), Kg=stack2(k_new[ts] * e_end),
                       V=stack2(v[ts]).astype(BF16), gam=jnp.exp(cum_last)))
    for u in st:
        left = jnp.concatenate([u["At"], u["Rt"]], axis=0).astype(BF16)
        right = jnp.concatenate([u["Bt"], u["Bt"], u["Kt"], u["Kt"]], axis=0).astype(BF16)
        AA = lax.dot_general(left, right, (((1,), (1,)), ((), ())), preferred_element_type=F32)
        u["A_ab"] = jnp.where(strict_bd, AA[:2 * C, :2 * C], 0.0)
        u["A_ak"] = jnp.where(strict_bd, AA[:2 * C, 2 * C:], 0.0)
        u["A_r"] = jnp.concatenate([jnp.where(incl_bd, AA[2 * C:, :2 * C], 0.0),
                                    jnp.where(incl_bd, AA[2 * C:, 2 * C:], 0.0)], axis=-1).astype(BF16)
    for u in st:
        akv = jnp.dot(u["A_ak"].astype(BF16), u["V"], preferred_element_type=F32)
        u["XA"] = jnp.concatenate([u["At"] + pltpu.roll(akv, N, axis=1), u["A_ab"]], axis=-1)
    for step in range(6):
        for u in st:
            XA = u["XA"]
            Ap = XA[:, LANES:].astype(BF16)
            if step < 5:
                prod = jnp.dot(Ap, XA.astype(BF16), preferred_element_type=F32)
                u["XA"] = jnp.concatenate([XA[:, :LANES] + prod[:, :LANES], prod[:, LANES:]], axis=-1)
            else:
                u["X"] = XA[:, :LANES] + jnp.dot(Ap, XA[:, :LANES].astype(BF16), preferred_element_type=F32)
    own = (lax.broadcasted_iota(jnp.int32, (2 * C, LANES), 0) >= C) == (lax.broadcasted_iota(jnp.int32, (2 * C, LANES), 1) >= N)
    for u in st:
        p1 = jnp.where(own, u["X"], 0.0).astype(BF16)
        p2 = pltpu.roll(jnp.where(own, 0.0, u["X"]), N, axis=1).astype(BF16)
        upper = jnp.concatenate([p1, p2], axis=-1)
        lower = jnp.concatenate([jnp.zeros((2 * C, LANES), BF16), u["V"]], axis=-1)
        u["W2"] = jnp.concatenate([upper, lower], axis=0)
    for u in st:
        bk = jnp.concatenate([u["Bg"], u["Kg"]], axis=0).astype(BF16)
        u["MG"] = lax.dot_general(bk, u["W2"], (((0,), (0,)), ((), ())), preferred_element_type=F32)
    for u in st:
        u["QZ"] = jnp.dot(u["A_r"], u["W2"], preferred_element_type=F32)
    for c, u in enumerate(st):
        rows = slice(c * LANES, (c + 1) * LANES)
        m_out[rows, :] = (eye_p * u["gam"] + u["MG"][:, :LANES]).astype(m_out.dtype)
        g_out[rows, :] = u["MG"][:, LANES:].astype(g_out.dtype)
        qs = u["Rt"] + u["QZ"][:, :LANES]
        zs = u["QZ"][:, LANES:]
        q_out[ts_of(c), :] = (qs[:C] + qs[C:]).astype(q_out.dtype)
        z_out[ts_of(c), :] = zs[:C] + zs[C:]


def wkv_pre(proj_b, hids, w2s, vecs, v_first, tt=1024):
    S = proj_b.shape[0]
    tt = min(tt, S)
    has_v = v_first is not None
    nb = B_WIDTH // LANES
    nstate_rows = (tt // WKV_CHUNK) * LANES
    tok = lambda off: pl.BlockSpec((tt, LANES), lambda i, p: (i, off + p))
    prev = lambda off: pl.BlockSpec((8, LANES), lambda i, p: (jnp.maximum(i * (tt // 8) - 1, 0), off + p))
    hid = lambda a: pl.BlockSpec((tt, a.shape[1]), lambda i, p: (i, 0))
    wcol = lambda a: pl.BlockSpec((a.shape[0], LANES), lambda i, p: (0, p))
    in_specs = [tok(0), tok(nb), tok(2 * nb), prev(0), prev(nb), prev(2 * nb),
                hid(hids[0]), hid(hids[1]), hid(hids[2]), wcol(w2s[0]), wcol(w2s[1]), wcol(w2s[2]),
                pl.BlockSpec((N_VEC_ROWS, LANES), lambda i, p: (0, p))]
    args = [proj_b] * 6 + list(hids[:3]) + list(w2s[:3]) + [vecs]
    if has_v:
        in_specs += [hid(hids[3]), wcol(w2s[3]), tok(0)]
        args += [hids[3], w2s[3], v_first]
    state_shape = jax.ShapeDtypeStruct(((S // WKV_CHUNK) * LANES, B_WIDTH), BF16)
    out_tok = pl.BlockSpec((tt, LANES), lambda i, p: (i, p))
    out_st = pl.BlockSpec((nstate_rows, LANES), lambda i, p: (i, p))
    return pl.pallas_call(
        functools.partial(_wkv_pre_body, has_v=has_v, tt=tt),
        grid=(S // tt, nb),
        in_specs=in_specs,
        out_specs=[out_st, out_st, out_tok, out_tok, out_tok, out_tok, out_tok],
        out_shape=[state_shape, state_shape, jax.ShapeDtypeStruct((S, B_WIDTH), BF16)]
                  + [jax.ShapeDtypeStruct((S, B_WIDTH), F32)] * 4,
        compiler_params=_cparams(("parallel", "parallel")),
    )(*args)


def _wkv_scan_body(m_ref, g_ref, q_ref, z_ref, gate_ref, bonus_ref, ln_ref, o_ref, h_ref, *, tt):
    i = pl.program_id(0)
    C = WKV_CHUNK
    N = B_HDIM
    npair = B_WIDTH // LANES

    @pl.when(i == 0)
    def _():
        h_ref[...] = jnp.zeros_like(h_ref)

    row = lax.broadcasted_iota(jnp.int32, (LANES, LANES), 0)
    col = lax.broadcasted_iota(jnp.int32, (LANES, LANES), 1)
    head_avg = jnp.where((row // N) == (col // N), 1.0 / N, 0.0).astype(BF16)

    def head_mean(x):
        hi = x.astype(BF16)
        lo = (x - hi.astype(F32)).astype(BF16)
        return (jnp.dot(hi, head_avg, preferred_element_type=F32)
                + jnp.dot(lo, head_avg, preferred_element_type=F32))

    for c in range(tt // C):
        ts = slice(c * C, (c + 1) * C)
        ss = slice(c * LANES, (c + 1) * LANES)
        ys = []
        for p in range(npair):
            ps = slice(p * LANES, (p + 1) * LANES)
            Hb = h_ref[p].astype(BF16)
            ys.append(jnp.dot(q_ref[ts, ps], Hb, preferred_element_type=F32) + z_ref[ts, ps])
            h_ref[p] = jnp.dot(m_ref[ss, ps], Hb, preferred_element_type=F32) + g_ref[ss, ps].astype(F32)
        for p in range(npair):
            ps = slice(p * LANES, (p + 1) * LANES)
            yc = ys[p] - head_mean(ys[p])
            var = head_mean(yc * yc)
            yn = yc * lax.rsqrt(var + B_GN_EPS) * ln_ref[0:1, ps] + ln_ref[1:2, ps]
            o_ref[ts, ps] = ((yn + bonus_ref[ts, ps]) * gate_ref[ts, ps]).astype(o_ref.dtype)


def wkv_scan(m, g, q, z, gate, bonus, ln, tt=256):
    S = q.shape[0]
    tt = min(tt, S)
    npair = B_WIDTH // LANES
    nstate_rows = (tt // WKV_CHUNK) * LANES
    tok = pl.BlockSpec((tt, B_WIDTH), lambda i: (i, 0))
    st = pl.BlockSpec((nstate_rows, B_WIDTH), lambda i: (i, 0))
    return pl.pallas_call(
        functools.partial(_wkv_scan_body, tt=tt),
        grid=(S // tt,),
        in_specs=[st, st, tok, tok, tok, tok, pl.BlockSpec((8, B_WIDTH), lambda i: (0, 0))],
        out_specs=tok,
        out_shape=jax.ShapeDtypeStruct((S, B_WIDTH), BF16),
        scratch_shapes=[pltpu.VMEM((npair, LANES, LANES), F32)],
        compiler_params=_cparams(("arbitrary",)),
    )(m, g, q, z, gate, bonus, ln)


def rwkv7_mixer(h, proj_b, p, v_first):
    has_v = v_first is not None
    mu = p["mu_wag"] if not has_v else jnp.concatenate([p["mu_wag"], p["v_mu"][None]], axis=0)
    hids = rwkv_lora(h, mu, p["w1"].astype(BF16), p["a1"].astype(BF16), p["g1"].astype(BF16),
                     p["v1"].astype(BF16) if has_v else None)
    rows = [p["mu_rkv"][0], p["mu_rkv"][1], p["mu_rkv"][2], p["w0"], p["a0"], p["k_k"], p["k_a"],
            p["r_k"].reshape(-1), p["v0"] if has_v else jnp.zeros((B_WIDTH,), F32)]
    vecs = jnp.concatenate([jnp.stack(rows), jnp.zeros((N_VEC_ROWS - len(rows), B_WIDTH), F32)], axis=0)
    w2s = [p["w2"], p["a2"], p["g2"]] + ([p["v2"]] if has_v else [])
    m, g, q, z, v_out, gate, bonus = wkv_pre(proj_b, hids, w2s, vecs, v_first)
    ln = jnp.concatenate([p["ln_w"][None], p["ln_b"][None], jnp.zeros((6, B_WIDTH), F32)], axis=0)
    return wkv_scan(m, g, q, z, gate, bonus, ln), v_out


def kernel(x, norm_mix_g, w_in, lam_q1, lam_k1, lam_q2, lam_k2, diff_subln_g, rw_mu_rkv, rw_mu_wag, rw_w0, rw_w1, rw_w2, rw_a0, rw_a1, rw_a2, rw_g1, rw_g2, rw_k_k, rw_k_a, rw_r_k, rw_ln_w, rw_ln_b, rw_v_mu, rw_v0, rw_v1, rw_v2, w_out, norm_ffn_g, w_up, w_down, norm_final_g):
    Bsz, S, D = x.shape
    depth = w_in.shape[0]
    topk = min(TOPK_MAX, S // 4)
    w_in_t = jnp.swapaxes(w_in, 1, 2)
    w_down_bf = w_down.astype(BF16)
    outs = []
    for b in range(Bsz):
        xb = x[b]
        v_first = None
        for l in range(depth):
            h = rmsnorm(xb, norm_mix_g[l], BF16)
            oA, oB, oC = 3 * A_WIDTH, 3 * A_WIDTH + 3 * B_WIDTH, 3 * A_WIDTH + 3 * B_WIDTH + 3 * C_WIDTH
            proj_a = matmul_ws(h, w_in_t, l, 0, oA, BF16, tn=512, w_transposed=True)
            proj_b = matmul_ws(h, w_in_t, l, oA, oB - oA, F32, tn=512, w_transposed=True)
            proj_c = matmul_ws(h, w_in_t, l, oB, oC - oB, BF16, tn=512, w_transposed=True)
            proj_i = matmul_ws(h, w_in_t, l, oC, w_in.shape[2] - oC, F32, tn=384, w_transposed=True)

            lam_init = 0.8 - 0.6 * math.exp(-0.3 * l)
            lam_vecs = jnp.stack([lam_q1[l], lam_k1[l], lam_q2[l], lam_k2[l]])
            o_a = diff_attention(proj_a, lam_vecs, diff_subln_g[l], lam_init)

            p = dict(mu_rkv=rw_mu_rkv[l], mu_wag=rw_mu_wag[l], w0=rw_w0[l], w1=rw_w1[l], w2=rw_w2[l],
                     a0=rw_a0[l], a1=rw_a1[l], a2=rw_a2[l], g1=rw_g1[l], g2=rw_g2[l], k_k=rw_k_k[l],
                     k_a=rw_k_a[l], r_k=rw_r_k[l], ln_w=rw_ln_w[l], ln_b=rw_ln_b[l])
            if l > 0:
                p.update(v_mu=rw_v_mu[l - 1], v0=rw_v0[l - 1], v1=rw_v1[l - 1], v2=rw_v2[l - 1])
            o_b, v_out = rwkv7_mixer(h, proj_b, p, v_first if l > 0 else None)
            if l == 0:
                v_first = v_out

            nq = IDX_HEADS * IDX_HDIM
            qi = proj_i[:, :nq].astype(BF16)
            ki = proj_i[:, nq:nq + IDX_HDIM].astype(BF16)
            wiT = proj_i[:, nq + IDX_HDIM:nq + IDX_HDIM + IDX_HEADS].T
            bias = dsa_index(ki, qi, wiT, topk)
            vT = proj_c[:, 2 * C_WIDTH:].T
            o_c = dsa_attention(proj_c, vT, bias).T

            mixed = jnp.concatenate([o_a, o_b, o_c], axis=-1)
            xb = matmul_ws(mixed, w_out, l, 0, D, F32, tn=512, epilogue="residual", residual=xb)
            h2 = rmsnorm(xb, norm_ffn_g[l], BF16)
            up = matmul_ws(h2, w_up, l, 0, w_up.shape[2], BF16, tn=512, epilogue="relu2")
            xb = matmul(up, w_down_bf, l, F32, epilogue="residual", residual=xb)
        outs.append(rmsnorm(xb, norm_final_g, F32))
    return jnp.stack(outs)
```

```python
import functools
import math

import jax
import jax.numpy as jnp
from jax import lax
from jax.experimental import pallas as pl
from jax.experimental.pallas import tpu as pltpu

F32 = jnp.float32
BF16 = jnp.bfloat16

A_HEADS, A_HDIM = 8, 64
A_WIDTH = A_HEADS * 2 * A_HDIM
B_HDIM, B_WIDTH = 64, 2048
B_HEADS = B_WIDTH // B_HDIM
B_GN_EPS = 64e-5
C_HEADS, C_HDIM = 8, 128
C_WIDTH = C_HEADS * C_HDIM
IDX_HEADS, IDX_HDIM = 16, 64
TOPK_MAX = 256
EPS = 1e-6

LANES = 128
VMEM_LIMIT = 56 * 1024 * 1024
NEG = -1e30
WKV_CHUNK = 64
SOFTMAX_ROWS = 64
HI = lax.Precision.HIGHEST


def _cparams(sem):
    return pltpu.CompilerParams(dimension_semantics=sem, vmem_limit_bytes=VMEM_LIMIT)


def _rmsnorm_body(x_ref, g_ref, o_ref):
    x = x_ref[...]
    ms = jnp.mean(x * x, axis=-1, keepdims=True)
    o_ref[...] = (x * lax.rsqrt(ms + EPS) * g_ref[...]).astype(o_ref.dtype)


def rmsnorm(x, g, out_dtype, tm=256):
    S, D = x.shape
    tm = min(tm, S)
    return pl.pallas_call(
        _rmsnorm_body,
        grid=(S // tm,),
        in_specs=[pl.BlockSpec((tm, D), lambda i: (i, 0)),
                  pl.BlockSpec((1, D), lambda i: (0, 0))],
        out_specs=pl.BlockSpec((tm, D), lambda i: (i, 0)),
        out_shape=jax.ShapeDtypeStruct((S, D), out_dtype),
        compiler_params=_cparams(("parallel",)),
    )(x, g.reshape(1, D))


def _mm_body(a_ref, b_ref, *rest, nk, epilogue):
    if epilogue == "residual":
        res_ref, o_ref, acc_ref = rest
    else:
        o_ref, acc_ref = rest
    k = pl.program_id(2)

    @pl.when(k == 0)
    def _():
        acc_ref[...] = jnp.zeros_like(acc_ref)

    acc_ref[...] += jnp.dot(a_ref[...], b_ref[...], preferred_element_type=F32)

    @pl.when(k == nk - 1)
    def _():
        acc = acc_ref[...]
        if epilogue == "relu2":
            r = jnp.maximum(acc, 0.0)
            acc = r * r
        elif epilogue == "residual":
            acc = acc + res_ref[...]
        o_ref[...] = acc.astype(o_ref.dtype)


def _pick(n, pref):
    for t in pref:
        if n % t == 0:
            return t
    return n


def matmul(a, b3, layer, out_dtype, epilogue="none", residual=None):
    M, K = a.shape
    _, _, N = b3.shape
    tm = _pick(M, (1024, 512, 256))
    tn = _pick(N, (1024, 768, 512, 384, 256, 128))
    tk = _pick(K, (2048, 1024, 512))
    nk = K // tk
    in_specs = [pl.BlockSpec((tm, tk), lambda i, j, k: (i, k)),
                pl.BlockSpec((None, tk, tn), lambda i, j, k: (layer, k, j))]
    args = [a, b3]
    if epilogue == "residual":
        in_specs.append(pl.BlockSpec((tm, tn), lambda i, j, k: (i, j)))
        args.append(residual)
    return pl.pallas_call(
        functools.partial(_mm_body, nk=nk, epilogue=epilogue),
        grid=(M // tm, N // tn, nk),
        in_specs=in_specs,
        out_specs=pl.BlockSpec((tm, tn), lambda i, j, k: (i, j)),
        out_shape=jax.ShapeDtypeStruct((M, N), out_dtype),
        scratch_shapes=[pltpu.VMEM((tm, tn), F32)],
        compiler_params=_cparams(("parallel", "parallel", "arbitrary")),
    )(*args)


def _mm_ws_body(x_ref, w_ref, *rest, epilogue, w_transposed):
    if epilogue == "residual":
        res_ref, o_ref, wb_ref = rest
    else:
        o_ref, wb_ref = rest

    @pl.when(pl.program_id(1) == 0)
    def _():
        wb_ref[...] = w_ref[...].astype(BF16)

    contract = (((1,), (1,)), ((), ())) if w_transposed else (((1,), (0,)), ((), ()))
    acc = lax.dot_general(x_ref[...], wb_ref[...], contract, preferred_element_type=F32)
    if epilogue == "relu2":
        r = jnp.maximum(acc, 0.0)
        acc = r * r
    elif epilogue == "residual":
        acc = acc + res_ref[...]
    o_ref[...] = acc.astype(o_ref.dtype)


def matmul_ws(x, w3, layer, col0, ncols, out_dtype, tn, epilogue="none", residual=None, tm=1024, w_transposed=False):
    M, K = x.shape
    tm = min(tm, M)
    nj = -(-ncols // tn)
    jb = col0 // tn
    assert col0 % tn == 0 and M % tm == 0
    if w_transposed:
        w_spec = pl.BlockSpec((None, tn, K), lambda j, i: (layer, jb + j, 0))
        wb_shape = (tn, K)
    else:
        w_spec = pl.BlockSpec((None, K, tn), lambda j, i: (layer, 0, jb + j))
        wb_shape = (K, tn)
    in_specs = [pl.BlockSpec((tm, K), lambda j, i: (i, 0)), w_spec]
    args = [x, w3]
    if epilogue == "residual":
        in_specs.append(pl.BlockSpec((tm, tn), lambda j, i: (i, j)))
        args.append(residual)
    return pl.pallas_call(
        functools.partial(_mm_ws_body, epilogue=epilogue, w_transposed=w_transposed),
        grid=(nj, M // tm),
        in_specs=in_specs,
        out_specs=pl.BlockSpec((tm, tn), lambda j, i: (i, j)),
        out_shape=jax.ShapeDtypeStruct((M, nj * tn), out_dtype),
        scratch_shapes=[pltpu.VMEM(wb_shape, BF16)],
        compiler_params=_cparams(("parallel", "arbitrary")),
    )(*args)


def _diffattn_body(q_ref, k_ref, v_ref, lam_ref, g_ref, o_ref, m_ref, acc_ref,
                   sa_ref, sb_ref, pa_ref, pb_ref, aa_ref, ab_ref, *, tq, lam_init):
    tk = tq
    i = pl.program_id(1)
    hw = 2 * A_HDIM
    q = q_ref[...] * (A_HDIM ** -0.5)
    lane = lax.broadcasted_iota(jnp.int32, (tq, hw), 1)
    zero = jnp.zeros_like(q)
    qz = jnp.concatenate([jnp.where(lane < A_HDIM, q, zero), jnp.where(lane >= A_HDIM, q, zero)], axis=0)
    m_ref[...] = jnp.full_like(m_ref, NEG)
    acc_ref[...] = jnp.zeros_like(acc_ref)
    ones = jnp.ones((tk, hw), BF16)

    def scores(j):
        k = k_ref[pl.ds(pl.multiple_of(j * tk, tk), tk), :]
        return lax.dot_general(qz, k, (((1,), (1,)), ((), ())), preferred_element_type=F32)

    nt = pl.num_programs(1)
    key_minus_row = (lax.broadcasted_iota(jnp.int32, (SOFTMAX_ROWS, tk), 1)
                     - lax.broadcasted_iota(jnp.int32, (SOFTMAX_ROWS, tk), 0))
    tile_rows = lambda j: pl.ds(pl.multiple_of(jnp.minimum(j, nt - 1) * tk, tk), tk)

    def scores(j, s_out):
        s_out[...] = lax.dot_general(qz, k_ref[tile_rows(j), :], (((1,), (1,)), ((), ())),
                                     preferred_element_type=F32)

    def softmax(j, s_in, p_out, alpha_out):
        for r0 in range(0, 2 * tq, SOFTMAX_ROWS):
            rs = slice(r0, r0 + SOFTMAX_ROWS)
            causal = key_minus_row <= (i * tq + r0 % tq) - j * tk
            s = jnp.where(causal, s_in[rs, :], NEG)
            m_old = m_ref[rs, :]
            m_new = jnp.maximum(m_old, jnp.max(s, axis=-1, keepdims=True))
            m_ref[rs, :] = m_new
            p_out[rs, :] = jnp.exp(s - m_new).astype(BF16)
            alpha_out[rs, :] = jnp.exp(m_old - m_new)

    def accumulate(j, p_in, alpha_in):
        v_ext = jnp.concatenate([v_ref[tile_rows(j), :], ones], axis=-1)
        acc_ref[...] = alpha_in[...] * acc_ref[...] + jnp.dot(p_in[...], v_ext, preferred_element_type=F32)

    scores(0, sa_ref)
    pb_ref[...] = jnp.zeros_like(pb_ref)
    ab_ref[...] = jnp.ones_like(ab_ref)

    def body(t, carry):
        j = 2 * t
        accumulate(jnp.maximum(j - 1, 0), pb_ref, ab_ref)
        softmax(j, sa_ref, pa_ref, aa_ref)
        scores(j + 1, sb_ref)
        accumulate(j, pa_ref, aa_ref)
        softmax(j + 1, sb_ref, pb_ref, ab_ref)
        scores(j + 2, sa_ref)
        return carry

    npairs = (i + 2) // 2
    lax.fori_loop(0, npairs, body, 0)
    accumulate(2 * npairs - 1, pb_ref, ab_ref)

    lv = lam_ref[...]
    lam = (jnp.exp(jnp.sum(lv[0:1] * lv[1:2], axis=-1, keepdims=True))
           - jnp.exp(jnp.sum(lv[2:3] * lv[3:4], axis=-1, keepdims=True)) + lam_init)
    acc = acc_ref[...]
    on = acc[:, :hw] / acc[:, hw:]
    o = on[:tq] - lam * on[tq:]
    ms = jnp.mean(o * o, axis=-1, keepdims=True)
    o_ref[...] = (o * lax.rsqrt(ms + EPS) * g_ref[...] * (1.0 - lam_init)).astype(o_ref.dtype)


def diff_attention(qkv, lam_vecs, subln_g, lam_init, tq=512):
    S = qkv.shape[0]
    tq = min(tq, S)
    hw = 2 * A_HDIM
    return pl.pallas_call(
        functools.partial(_diffattn_body, tq=tq, lam_init=lam_init),
        grid=(A_HEADS, S // tq),
        in_specs=[pl.BlockSpec((tq, hw), lambda h, i: (i, h)),
                  pl.BlockSpec((S, hw), lambda h, i: (0, A_HEADS + h)),
                  pl.BlockSpec((S, hw), lambda h, i: (0, 2 * A_HEADS + h)),
                  pl.BlockSpec((4, A_HDIM), lambda h, i: (0, 0)),
                  pl.BlockSpec((1, hw), lambda h, i: (0, 0))],
        out_specs=pl.BlockSpec((tq, hw), lambda h, i: (i, h)),
        out_shape=jax.ShapeDtypeStruct((S, A_WIDTH), BF16),
        scratch_shapes=[pltpu.VMEM((2 * tq, 1), F32), pltpu.VMEM((2 * tq, 2 * hw), F32),
                        pltpu.VMEM((2 * tq, tq), F32), pltpu.VMEM((2 * tq, tq), F32),
                        pltpu.VMEM((2 * tq, tq), BF16), pltpu.VMEM((2 * tq, tq), BF16),
                        pltpu.VMEM((2 * tq, 1), F32), pltpu.VMEM((2 * tq, 1), F32)],
        compiler_params=_cparams(("parallel", "parallel")),
    )(qkv, qkv, qkv, lam_vecs, subln_g.reshape(1, hw))


KEY_NEG_INF = -2139095041


def _float_key(s):
    b = pltpu.bitcast(s, jnp.int32)
    return b ^ ((b >> 31) & jnp.int32(0x7FFFFFFF))


def _dsa_index_body(ki_ref, qi_ref, wiT_ref, bias_ref, key_ref, *, qb, kc, topk, nkc_total):
    i = pl.program_id(0)
    nch = ((i + 1) * qb) // kc
    wi = wiT_ref[...] * (IDX_HEADS ** -0.5 * IDX_HDIM ** -0.5)
    qi = qi_ref[...]
    tpos = i * qb + lax.broadcasted_iota(jnp.int32, (kc, qb), 1)
    srow = lax.broadcasted_iota(jnp.int32, (kc, qb), 0)

    def score_chunk(c, carry):
        r0 = pl.multiple_of(c * kc, kc)
        kic = ki_ref[pl.ds(r0, kc), :]
        acc = jnp.zeros((kc, qb), F32)
        for h in range(IDX_HEADS):
            d = lax.dot_general(kic, qi[:, h * IDX_HDIM:(h + 1) * IDX_HDIM],
                                (((1,), (1,)), ((), ())), preferred_element_type=F32)
            acc = acc + jnp.maximum(d, 0.0) * wi[h:h + 1, :]
        acc = jnp.where(r0 + srow <= tpos, acc, -jnp.inf)
        key_ref[pl.ds(r0, kc), :] = _float_key(acc)
        return carry

    lax.fori_loop(0, nch, score_chunk, 0)

    def count_ge(cand):
        def body(c, cnt):
            r0 = pl.multiple_of(c * kc, kc)
            ge = (key_ref[pl.ds(r0, kc), :] >= cand).astype(jnp.int32)
            return cnt + jnp.sum(ge.reshape(kc // 8, 8, qb), axis=0)
        cnt8 = lax.fori_loop(0, nch, body, jnp.zeros((8, qb), jnp.int32))
        return jnp.sum(cnt8, axis=0, keepdims=True)

    def bit_step(it, tau):
        cand = tau + (jnp.int32(1) << (31 - it))
        return jnp.where(count_ge(cand) >= topk, cand, tau)

    tau = lax.fori_loop(0, 32, bit_step, jnp.full((1, qb), jnp.iinfo(jnp.int32).min, jnp.int32))
    tau = jnp.maximum(tau, KEY_NEG_INF + 1)

    def write_chunk(c, carry):
        r0 = pl.multiple_of(c * kc, kc)
        sel = key_ref[pl.ds(r0, kc), :] >= tau
        bias_ref[pl.ds(r0, kc), :] = jnp.where(sel, 0.0, NEG).astype(bias_ref.dtype)
        return carry

    lax.fori_loop(0, nch, write_chunk, 0)

    def fill_chunk(c, carry):
        r0 = pl.multiple_of(c * kc, kc)
        bias_ref[pl.ds(r0, kc), :] = jnp.full((kc, qb), NEG, bias_ref.dtype)
        return carry

    lax.fori_loop(nch, nkc_total, fill_chunk, 0)


def dsa_index(ki, qi, wiT, topk, qb=256, kc=256):
    S = ki.shape[0]
    qb = min(qb, S)
    kc = min(kc, qb)
    return pl.pallas_call(
        functools.partial(_dsa_index_body, qb=qb, kc=kc, topk=topk, nkc_total=S // kc),
        grid=(S // qb,),
        in_specs=[pl.BlockSpec((S, IDX_HDIM), lambda i: (0, 0)),
                  pl.BlockSpec((qb, IDX_HEADS * IDX_HDIM), lambda i: (i, 0)),
                  pl.BlockSpec((IDX_HEADS, qb), lambda i: (0, i))],
        out_specs=pl.BlockSpec((S, qb), lambda i: (0, i)),
        out_shape=jax.ShapeDtypeStruct((S, S), BF16),
        scratch_shapes=[pltpu.VMEM((S, qb), jnp.int32)],
        compiler_params=_cparams(("parallel",)),
    )(ki, qi, wiT)


def _dsa_attn_body(qi_ref, kj_ref, q_ref, k_ref, vT_ref, bias_ref, oT_ref, m_ref, l_ref, acc_ref, *, qb, kc):
    step = pl.program_id(0)
    i = qi_ref[step]
    j = kj_ref[step]
    scale = C_HDIM ** -0.5
    heads = [slice(h * C_HDIM, (h + 1) * C_HDIM) for h in range(C_HEADS)]

    @pl.when(j == 0)
    def _():
        m_ref[...] = jnp.full_like(m_ref, NEG)
        l_ref[...] = jnp.zeros_like(l_ref)
        acc_ref[...] = jnp.zeros_like(acc_ref)

    bias = bias_ref[...].astype(F32)
    sT = [lax.dot_general(k_ref[:, cs], q_ref[:, cs], (((1,), (1,)), ((), ())),
                          preferred_element_type=F32) * scale + bias for cs in heads]
    m_old = m_ref[...]
    m_new = jnp.maximum(m_old, jnp.concatenate([jnp.max(s, axis=0, keepdims=True) for s in sT], axis=0))
    alpha = jnp.exp(m_old - m_new)
    p = [jnp.exp(s - m_new[h:h + 1, :]) for h, s in enumerate(sT)]
    l_ref[...] = alpha * l_ref[...] + jnp.concatenate([jnp.sum(x, axis=0, keepdims=True) for x in p], axis=0)
    m_ref[...] = m_new
    for h, cs in enumerate(heads):
        acc_ref[cs, :] = alpha[h:h + 1, :] * acc_ref[cs, :] + jnp.dot(vT_ref[cs, :], p[h].astype(BF16),
                                                                       preferred_element_type=F32)

    @pl.when(j == ((i + 1) * qb - 1) // kc)
    def _():
        l = l_ref[...]
        for h, cs in enumerate(heads):
            oT_ref[cs, :] = (acc_ref[cs, :] / l[h:h + 1, :]).astype(oT_ref.dtype)


def dsa_attention(qkv, vT, bias, qb=256, kc=512):
    S = qkv.shape[0]
    qb = min(qb, S)
    kc = min(kc, S)
    pairs = [(i, j) for i in range(S // qb) for j in range(((i + 1) * qb - 1) // kc + 1)]
    qi = jnp.asarray([p_[0] for p_ in pairs], jnp.int32)
    kj = jnp.asarray([p_[1] for p_ in pairs], jnp.int32)
    grid_spec = pltpu.PrefetchScalarGridSpec(
        num_scalar_prefetch=2,
        grid=(len(pairs),),
        in_specs=[pl.BlockSpec((qb, C_WIDTH), lambda s, qi, kj: (qi[s], 0)),
                  pl.BlockSpec((kc, C_WIDTH), lambda s, qi, kj: (kj[s], 1)),
                  pl.BlockSpec((C_WIDTH, kc), lambda s, qi, kj: (0, kj[s])),
                  pl.BlockSpec((kc, qb), lambda s, qi, kj: (kj[s], qi[s]))],
        out_specs=pl.BlockSpec((C_WIDTH, qb), lambda s, qi, kj: (0, qi[s])),
        scratch_shapes=[pltpu.VMEM((C_HEADS, qb), F32), pltpu.VMEM((C_HEADS, qb), F32),
                        pltpu.VMEM((C_WIDTH, qb), F32)])
    return pl.pallas_call(
        functools.partial(_dsa_attn_body, qb=qb, kc=kc),
        grid_spec=grid_spec,
        out_shape=jax.ShapeDtypeStruct((C_WIDTH, S), BF16),
        compiler_params=_cparams(("arbitrary",)),
    )(qi, kj, qkv, qkv, vT, bias)


def _sigmoid(x):
    return 1.0 / (1.0 + jnp.exp(-x))


def _shift_rows(cur, prev_ref, first_tile):
    prev_row = jnp.where(first_tile, 0.0, prev_ref[7:8, :].astype(F32))
    rolled = pltpu.roll(cur, 1, axis=0)
    row = lax.broadcasted_iota(jnp.int32, cur.shape, 0)
    return jnp.where(row == 0, prev_row, rolled)


def _rwkv_lora_body(*refs, has_v):
    if has_v:
        (h_ref, hp_ref, mu_ref, w1_ref, a1_ref, g1_ref, v1_ref, ow_ref, oa_ref, og_ref, ov_ref) = refs
    else:
        (h_ref, hp_ref, mu_ref, w1_ref, a1_ref, g1_ref, ow_ref, oa_ref, og_ref) = refs
    i = pl.program_id(0)
    h = h_ref[...].astype(F32)
    dh = _shift_rows(h, hp_ref, i == 0) - h

    def lora(row, w_ref):
        xm = (h + dh * mu_ref[row:row + 1, :]).astype(BF16)
        return jnp.dot(xm, w_ref[...], preferred_element_type=F32)

    ow_ref[...] = jnp.tanh(lora(0, w1_ref))
    oa_ref[...] = lora(1, a1_ref)
    og_ref[...] = _sigmoid(lora(2, g1_ref))
    if has_v:
        ov_ref[...] = lora(3, v1_ref)


def rwkv_lora(h, mu, w1, a1, g1, v1, tm=256):
    S, D = h.shape
    tm = min(tm, S)
    has_v = v1 is not None
    ws = [w1, a1, g1] + ([v1] if has_v else [])
    full = lambda a: pl.BlockSpec(a.shape, lambda i: (0, 0))
    return pl.pallas_call(
        functools.partial(_rwkv_lora_body, has_v=has_v),
        grid=(S // tm,),
        in_specs=[pl.BlockSpec((tm, D), lambda i: (i, 0)),
                  pl.BlockSpec((8, D), lambda i: (jnp.maximum(i * (tm // 8) - 1, 0), 0)),
                  full(mu)] + [full(w) for w in ws],
        out_specs=[pl.BlockSpec((tm, w.shape[1]), lambda i: (i, 0)) for w in ws],
        out_shape=[jax.ShapeDtypeStruct((S, w.shape[1]), F32) for w in ws],
        compiler_params=_cparams(("parallel",)),
    )(h, h, mu, *ws)


(V_MU_R, V_MU_K, V_MU_V, V_W0, V_A0, V_KK, V_KA, V_RK, V_V0) = range(9)
N_VEC_ROWS = 16


def _wkv_pre_body(*refs, has_v, tt):
    if has_v:
        (r_ref, k_ref, v_ref, rp_ref, kp_ref, vp_ref, hw_ref, ha_ref, hg_ref, w2_ref, a2_ref, g2_ref,
         vec_ref, hv_ref, v2_ref, vf_ref,
         m_out, g_out, q_out, z_out, vout_ref, gate_ref, bonus_ref) = refs
    else:
        (r_ref, k_ref, v_ref, rp_ref, kp_ref, vp_ref, hw_ref, ha_ref, hg_ref, w2_ref, a2_ref, g2_ref,
         vec_ref,
         m_out, g_out, q_out, z_out, vout_ref, gate_ref, bonus_ref) = refs
    i = pl.program_id(0)
    first = i == 0
    C = WKV_CHUNK
    N = B_HDIM
    vec = lambda row: vec_ref[row:row + 1, :]

    r = r_ref[...]
    k = k_ref[...]
    v = v_ref[...]
    r = r + (_shift_rows(r, rp_ref, first) - r) * vec(V_MU_R)
    k = k + (_shift_rows(k, kp_ref, first) - k) * vec(V_MU_K)
    v = v + (_shift_rows(v, vp_ref, first) - v) * vec(V_MU_V)

    wl = vec(V_W0) + jnp.dot(hw_ref[...], w2_ref[...], preferred_element_type=F32, precision=HI)
    z = -wl
    softplus = jnp.maximum(z, 0.0) + jnp.log(1.0 + jnp.exp(-jnp.abs(z)))
    logw = -jnp.exp(-softplus - 0.5)
    bdot = lambda a_ref, b_ref: jnp.dot(a_ref[...].astype(BF16), b_ref[...].astype(BF16), preferred_element_type=F32)
    a_sig = _sigmoid(vec(V_A0) + bdot(ha_ref, a2_ref))
    gate = bdot(hg_ref, g2_ref)
    if has_v:
        mix = _sigmoid(vec(V_V0) + bdot(hv_ref, v2_ref))
        v = v + (vf_ref[...] - v) * mix
    vout_ref[...] = v
    gate_ref[...] = gate

    kk = k * vec(V_KK)
    k_new = k * (1.0 + (a_sig - 1.0) * vec(V_KA))
    rk = r * k_new * vec(V_RK)

    rowc = lax.broadcasted_iota(jnp.int32, (C, C), 0)
    colc = lax.broadcasted_iota(jnp.int32, (C, C), 1)
    tril_incl = (rowc >= colc).astype(F32)
    prow = lax.broadcasted_iota(jnp.int32, (2 * C, 2 * C), 0)
    pcol = lax.broadcasted_iota(jnp.int32, (2 * C, 2 * C), 1)
    same_head = (prow >= C) == (pcol >= C)
    dstep = jnp.where(prow >= C, prow - C, prow) - jnp.where(pcol >= C, pcol - C, pcol)
    strict_bd = jnp.where(same_head, dstep, -1) > 0
    incl_bd = jnp.where(same_head, dstep, -1) >= 0
    eye_p = (prow == pcol).astype(F32)
    head_ones = jnp.where(same_head, 1.0, 0.0).astype(BF16)

    def head_sum(x):
        hi = x.astype(BF16)
        lo = (x - hi.astype(F32)).astype(BF16)
        return (jnp.dot(hi, head_ones, preferred_element_type=F32)
                + jnp.dot(lo, head_ones, preferred_element_type=F32))

    kkn = kk / jnp.maximum(jnp.sqrt(head_sum(kk * kk)), 1e-12)
    bonus_ref[...] = head_sum(rk) * v
    a_scan = -kkn
    b_scan = kkn * a_sig

    head0 = lax.broadcasted_iota(jnp.int32, (C, LANES), 1) < N

    def stack2(x):
        return jnp.concatenate([jnp.where(head0, x, 0.0), jnp.where(head0, 0.0, x)], axis=0)

    ts_of = lambda c: slice(c * C, (c + 1) * C)
    st = []
    for c in range(tt // C):
        ts = ts_of(c)
        lw = logw[ts]
        cum = jnp.dot(tril_incl, lw, preferred_element_type=F32, precision=HI)
        cum_last = cum[C - 1:C, :]
        e_neg = jnp.exp(-cum)
        e_end = jnp.exp(cum_last - cum)
        st.append(dict(At=stack2(a_scan[ts] * jnp.exp(cum - lw)), Rt=stack2(r[ts] * jnp.exp(cum)),
                       Bt=b_scan[ts] * e_neg, Kt=k_new[ts] * e_neg,
                       Bg=stack2(b_scan[ts] * e_end), Kg=stack2(k_new[ts] * e_end),
                       V=stack2(v[ts]).astype(BF16), gam=jnp.exp(cum_last)))
    for u in st:
        left = jnp.concatenate([u["At"], u["Rt"]], axis=0).astype(BF16)
        right = jnp.concatenate([u["Bt"], u["Bt"], u["Kt"], u["Kt"]], axis=0).astype(BF16)
        AA = lax.dot_general(left, right, (((1,), (1,)), ((), ())), preferred_element_type=F32)
        u["A_ab"] = jnp.where(strict_bd, AA[:2 * C, :2 * C], 0.0)
        u["A_ak"] = jnp.where(strict_bd, AA[:2 * C, 2 * C:], 0.0)
        u["A_r"] = jnp.concatenate([jnp.where(incl_bd, AA[2 * C:, :2 * C], 0.0),
                                    jnp.where(incl_bd, AA[2 * C:, 2 * C:], 0.0)], axis=-1).astype(BF16)
    for u in st:
        akv = jnp.dot(u["A_ak"].astype(BF16), u["V"], preferred_element_type=F32)
        u["XA"] = jnp.concatenate([u["At"] + pltpu.roll(akv, N, axis=1), u["A_ab"]], axis=-1)
    for step in range(6):
        for u in st:
            XA = u["XA"]
            Ap = XA[:, LANES:].astype(BF16)
            if step < 5:
                prod = jnp.dot(Ap, XA.astype(BF16), preferred_element_type=F32)
                u["XA"] = jnp.concatenate([XA[:, :LANES] + prod[:, :LANES], prod[:, LANES:]], axis=-1)
            else:
                u["X"] = XA[:, :LANES] + jnp.dot(Ap, XA[:, :LANES].astype(BF16), preferred_element_type=F32)
    own = (lax.broadcasted_iota(jnp.int32, (2 * C, LANES), 0) >= C) == (lax.broadcasted_iota(jnp.int32, (2 * C, LANES), 1) >= N)
    for u in st:
        p1 = jnp.where(own, u["X"], 0.0).astype(BF16)
        p2 = pltpu.roll(jnp.where(own, 0.0, u["X"]), N, axis=1).astype(BF16)
        upper = jnp.concatenate([p1, p2], axis=-1)
        lower = jnp.concatenate([jnp.zeros((2 * C, LANES), BF16), u["V"]], axis=-1)
        u["W2"] = jnp.concatenate([upper, lower], axis=0)
    for u in st:
        bk = jnp.concatenate([u["Bg"], u["Kg"]], axis=0).astype(BF16)
        u["MG"] = lax.dot_general(bk, u["W2"], (((0,), (0,)), ((), ())), preferred_element_type=F32)
    for u in st:
        u["QZ"] = jnp.dot(u["A_r"], u["W2"], preferred_element_type=F32)
    for c, u in enumerate(st):
        rows = slice(c * LANES, (c + 1) * LANES)
        m_out[rows, :] = (eye_p * u["gam"] + u["MG"][:, :LANES]).astype(m_out.dtype)
        g_out[rows, :] = u["MG"][:, LANES:].astype(g_out.dtype)
        qs = u["Rt"] + u["QZ"][:, :LANES]
        zs = u["QZ"][:, LANES:]
        q_out[ts_of(c), :] = (qs[:C] + qs[C:]).astype(q_out.dtype)
        z_out[ts_of(c), :] = zs[:C] + zs[C:]


def wkv_pre(proj_b, hids, w2s, vecs, v_first, tt=1024):
    S = proj_b.shape[0]
    tt = min(tt, S)
    has_v = v_first is not None
    nb = B_WIDTH // LANES
    nstate_rows = (tt // WKV_CHUNK) * LANES
    tok = lambda off: pl.BlockSpec((tt, LANES), lambda i, p: (i, off + p))
    prev = lambda off: pl.BlockSpec((8, LANES), lambda i, p: (jnp.maximum(i * (tt // 8) - 1, 0), off + p))
    hid = lambda a: pl.BlockSpec((tt, a.shape[1]), lambda i, p: (i, 0))
    wcol = lambda a: pl.BlockSpec((a.shape[0], LANES), lambda i, p: (0, p))
    in_specs = [tok(0), tok(nb), tok(2 * nb), prev(0), prev(nb), prev(2 * nb),
                hid(hids[0]), hid(hids[1]), hid(hids[2]), wcol(w2s[0]), wcol(w2s[1]), wcol(w2s[2]),
                pl.BlockSpec((N_VEC_ROWS, LANES), lambda i, p: (0, p))]
    args = [proj_b] * 6 + list(hids[:3]) + list(w2s[:3]) + [vecs]
    if has_v:
        in_specs += [hid(hids[3]), wcol(w2s[3]), tok(0)]
        args += [hids[3], w2s[3], v_first]
    state_shape = jax.ShapeDtypeStruct(((S // WKV_CHUNK) * LANES, B_WIDTH), BF16)
    out_tok = pl.BlockSpec((tt, LANES), lambda i, p: (i, p))
    out_st = pl.BlockSpec((nstate_rows, LANES), lambda i, p: (i, p))
    return pl.pallas_call(
        functools.partial(_wkv_pre_body, has_v=has_v, tt=tt),
        grid=(S // tt, nb),
        in_specs=in_specs,
        out_specs=[out_st, out_st, out_tok, out_tok, out_tok, out_tok, out_tok],
        out_shape=[state_shape, state_shape, jax.ShapeDtypeStruct((S, B_WIDTH), BF16)]
                  + [jax.ShapeDtypeStruct((S, B_WIDTH), F32)] * 4,
        compiler_params=_cparams(("parallel", "parallel")),
    )(*args)


def _wkv_scan_body(m_ref, g_ref, q_ref, z_ref, gate_ref, bonus_ref, ln_ref, o_ref, h_ref, *, tt):
    i = pl.program_id(0)
    C = WKV_CHUNK
    N = B_HDIM
    npair = B_WIDTH // LANES

    @pl.when(i == 0)
    def _():
        h_ref[...] = jnp.zeros_like(h_ref)

    row = lax.broadcasted_iota(jnp.int32, (LANES, LANES), 0)
    col = lax.broadcasted_iota(jnp.int32, (LANES, LANES), 1)
    head_avg = jnp.where((row // N) == (col // N), 1.0 / N, 0.0).astype(BF16)

    def head_mean(x):
        hi = x.astype(BF16)
        lo = (x - hi.astype(F32)).astype(BF16)
        return (jnp.dot(hi, head_avg, preferred_element_type=F32)
                + jnp.dot(lo, head_avg, preferred_element_type=F32))

    for c in range(tt // C):
        ts = slice(c * C, (c + 1) * C)
        ss = slice(c * LANES, (c + 1) * LANES)
        ys = []
        for p in range(npair):
            ps = slice(p * LANES, (p + 1) * LANES)
            Hb = h_ref[p].astype(BF16)
            ys.append(jnp.dot(q_ref[ts, ps], Hb, preferred_element_type=F32) + z_ref[ts, ps])
            h_ref[p] = jnp.dot(m_ref[ss, ps], Hb, preferred_element_type=F32) + g_ref[ss, ps].astype(F32)
        for p in range(npair):
            ps = slice(p * LANES, (p + 1) * LANES)
            yc = ys[p] - head_mean(ys[p])
            var = head_mean(yc * yc)
            yn = yc * lax.rsqrt(var + B_GN_EPS) * ln_ref[0:1, ps] + ln_ref[1:2, ps]
            o_ref[ts, ps] = ((yn + bonus_ref[ts, ps]) * gate_ref[ts, ps]).astype(o_ref.dtype)


def wkv_scan(m, g, q, z, gate, bonus, ln, tt=256):
    S = q.shape[0]
    tt = min(tt, S)
    npair = B_WIDTH // LANES
    nstate_rows = (tt // WKV_CHUNK) * LANES
    tok = pl.BlockSpec((tt, B_WIDTH), lambda i: (i, 0))
    st = pl.BlockSpec((nstate_rows, B_WIDTH), lambda i: (i, 0))
    return pl.pallas_call(
        functools.partial(_wkv_scan_body, tt=tt),
        grid=(S // tt,),
        in_specs=[st, st, tok, tok, tok, tok, pl.BlockSpec((8, B_WIDTH), lambda i: (0, 0))],
        out_specs=tok,
        out_shape=jax.ShapeDtypeStruct((S, B_WIDTH), BF16),
        scratch_shapes=[pltpu.VMEM((npair, LANES, LANES), F32)],
        compiler_params=_cparams(("arbitrary",)),
    )(m, g, q, z, gate, bonus, ln)


def rwkv7_mixer(h, proj_b, p, v_first):
    has_v = v_first is not None
    mu = p["mu_wag"] if not has_v else jnp.concatenate([p["mu_wag"], p["v_mu"][None]], axis=0)
    hids = rwkv_lora(h, mu, p["w1"].astype(BF16), p["a1"].astype(BF16), p["g1"].astype(BF16),
                     p["v1"].astype(BF16) if has_v else None)
    rows = [p["mu_rkv"][0], p["mu_rkv"][1], p["mu_rkv"][2], p["w0"], p["a0"], p["k_k"], p["k_a"],
            p["r_k"].reshape(-1), p["v0"] if has_v else jnp.zeros((B_WIDTH,), F32)]
    vecs = jnp.concatenate([jnp.stack(rows), jnp.zeros((N_VEC_ROWS - len(rows), B_WIDTH), F32)], axis=0)
    w2s = [p["w2"], p["a2"], p["g2"]] + ([p["v2"]] if has_v else [])
    m, g, q, z, v_out, gate, bonus = wkv_pre(proj_b, hids, w2s, vecs, v_first)
    ln = jnp.concatenate([p["ln_w"][None], p["ln_b"][None], jnp.zeros((6, B_WIDTH), F32)], axis=0)
    return wkv_scan(m, g, q, z, gate, bonus, ln), v_out


def kernel(x, norm_mix_g, w_in, lam_q1, lam_k1, lam_q2, lam_k2, diff_subln_g, rw_mu_rkv, rw_mu_wag, rw_w0, rw_w1, rw_w2, rw_a0, rw_a1, rw_a2, rw_g1, rw_g2, rw_k_k, rw_k_a, rw_r_k, rw_ln_w, rw_ln_b, rw_v_mu, rw_v0, rw_v1, rw_v2, w_out, norm_ffn_g, w_up, w_down, norm_final_g):
    Bsz, S, D = x.shape
    depth = w_in.shape[0]
    topk = min(TOPK_MAX, S // 4)
    w_in_t = jnp.swapaxes(w_in, 1, 2)
    w_down_bf = w_down.astype(BF16)
    outs = []
    for b in range(Bsz):
        xb = x[b]
        v_first = None
        for l in range(depth):
            h = rmsnorm(xb, norm_mix_g[l], BF16)
            oA, oB, oC = 3 * A_WIDTH, 3 * A_WIDTH + 3 * B_WIDTH, 3 * A_WIDTH + 3 * B_WIDTH + 3 * C_WIDTH
            proj_a = matmul_ws(h, w_in_t, l, 0, oA, BF16, tn=512, w_transposed=True)
            proj_b = matmul_ws(h, w_in_t, l, oA, oB - oA, F32, tn=512, w_transposed=True)
            proj_c = matmul_ws(h, w_in_t, l, oB, oC - oB, BF16, tn=512, w_transposed=True)
            proj_i = matmul_ws(h, w_in_t, l, oC, w_in.shape[2] - oC, F32, tn=384, w_transposed=True)

            lam_init = 0.8 - 0.6 * math.exp(-0.3 * l)
            lam_vecs = jnp.stack([lam_q1[l], lam_k1[l], lam_q2[l], lam_k2[l]])
            o_a = diff_attention(proj_a, lam_vecs, diff_subln_g[l], lam_init)

            p = dict(mu_rkv=rw_mu_rkv[l], mu_wag=rw_mu_wag[l], w0=rw_w0[l], w1=rw_w1[l], w2=rw_w2[l],
                     a0=rw_a0[l], a1=rw_a1[l], a2=rw_a2[l], g1=rw_g1[l], g2=rw_g2[l], k_k=rw_k_k[l],
                     k_a=rw_k_a[l], r_k=rw_r_k[l], ln_w=rw_ln_w[l], ln_b=rw_ln_b[l])
            if l > 0:
                p.update(v_mu=rw_v_mu[l - 1], v0=rw_v0[l - 1], v1=rw_v1[l - 1], v2=rw_v2[l - 1])
            o_b, v_out = rwkv7_mixer(h, proj_b, p, v_first if l > 0 else None)
            if l == 0:
                v_first = v_out

            nq = IDX_HEADS * IDX_HDIM
            qi = proj_i[:, :nq].astype(BF16)
            ki = proj_i[:, nq:nq + IDX_HDIM].astype(BF16)
            wiT = proj_i[:, nq + IDX_HDIM:nq + IDX_HDIM + IDX_HEADS].T
            bias = dsa_index(ki, qi, wiT, topk)
            vT = proj_c[:, 2 * C_WIDTH:].T
            o_c = dsa_attention(proj_c, vT, bias).T

            mixed = jnp.concatenate([o_a, o_b, o_c], axis=-1)
            xb = matmul_ws(mixed, w_out, l, 0, D, F32, tn=512, epilogue="residual", residual=xb)
            h2 = rmsnorm(xb, norm_ffn_g[l], BF16)
            up = matmul_ws(h2, w_up, l, 0, w_up.shape[2], BF16, tn=1024, tm=512, epilogue="relu2")
            xb = matmul(up, w_down_bf, l, F32, epilogue="residual", residual=xb)
        outs.append(rmsnorm(xb, norm_final_g, F32))
    return jnp.stack(outs)
```

```python
import functools
import math

import jax
import jax.numpy as jnp
from jax import lax
from jax.experimental import pallas as pl
from jax.experimental.pallas import tpu as pltpu

F32 = jnp.float32
BF16 = jnp.bfloat16

A_HEADS, A_HDIM = 8, 64
A_WIDTH = A_HEADS * 2 * A_HDIM
B_HDIM, B_WIDTH = 64, 2048
B_HEADS = B_WIDTH // B_HDIM
B_GN_EPS = 64e-5
C_HEADS, C_HDIM = 8, 128
C_WIDTH = C_HEADS * C_HDIM
IDX_HEADS, IDX_HDIM = 16, 64
TOPK_MAX = 256
EPS = 1e-6

LANES = 128
VMEM_LIMIT = 56 * 1024 * 1024
NEG = -1e30
WKV_CHUNK = 64
SOFTMAX_ROWS = 64
HI = lax.Precision.HIGHEST


def _cparams(sem):
    return pltpu.CompilerParams(dimension_semantics=sem, vmem_limit_bytes=VMEM_LIMIT)


def _rmsnorm_body(x_ref, g_ref, o_ref):
    x = x_ref[...]
    ms = jnp.mean(x * x, axis=-1, keepdims=True)
    o_ref[...] = (x * lax.rsqrt(ms + EPS) * g_ref[...]).astype(o_ref.dtype)


def rmsnorm(x, g, out_dtype, tm=256):
    S, D = x.shape
    tm = min(tm, S)
    return pl.pallas_call(
        _rmsnorm_body,
        grid=(S // tm,),
        in_specs=[pl.BlockSpec((tm, D), lambda i: (i, 0)),
                  pl.BlockSpec((1, D), lambda i: (0, 0))],
        out_specs=pl.BlockSpec((tm, D), lambda i: (i, 0)),
        out_shape=jax.ShapeDtypeStruct((S, D), out_dtype),
        compiler_params=_cparams(("parallel",)),
    )(x, g.reshape(1, D))


def _mm_body(a_ref, b_ref, *rest, nk, epilogue):
    if epilogue == "residual":
        res_ref, o_ref, acc_ref = rest
    else:
        o_ref, acc_ref = rest
    k = pl.program_id(2)

    @pl.when(k == 0)
    def _():
        acc_ref[...] = jnp.zeros_like(acc_ref)

    acc_ref[...] += jnp.dot(a_ref[...], b_ref[...], preferred_element_type=F32)

    @pl.when(k == nk - 1)
    def _():
        acc = acc_ref[...]
        if epilogue == "relu2":
            r = jnp.maximum(acc, 0.0)
            acc = r * r
        elif epilogue == "residual":
            acc = acc + res_ref[...]
        o_ref[...] = acc.astype(o_ref.dtype)


def _pick(n, pref):
    for t in pref:
        if n % t == 0:
            return t
    return n


def matmul(a, b3, layer, out_dtype, epilogue="none", residual=None):
    M, K = a.shape
    _, _, N = b3.shape
    tm = _pick(M, (1024, 512, 256))
    tn = _pick(N, (1024, 768, 512, 384, 256, 128))
    tk = _pick(K, (2048, 1024, 512))
    nk = K // tk
    in_specs = [pl.BlockSpec((tm, tk), lambda i, j, k: (i, k)),
                pl.BlockSpec((None, tk, tn), lambda i, j, k: (layer, k, j))]
    args = [a, b3]
    if epilogue == "residual":
        in_specs.append(pl.BlockSpec((tm, tn), lambda i, j, k: (i, j)))
        args.append(residual)
    return pl.pallas_call(
        functools.partial(_mm_body, nk=nk, epilogue=epilogue),
        grid=(M // tm, N // tn, nk),
        in_specs=in_specs,
        out_specs=pl.BlockSpec((tm, tn), lambda i, j, k: (i, j)),
        out_shape=jax.ShapeDtypeStruct((M, N), out_dtype),
        scratch_shapes=[pltpu.VMEM((tm, tn), F32)],
        compiler_params=_cparams(("parallel", "parallel", "arbitrary")),
    )(*args)


def _mm_ws_body(x_ref, w_ref, *rest, epilogue, w_transposed):
    if epilogue == "residual":
        res_ref, o_ref, wb_ref = rest
    else:
        o_ref, wb_ref = rest

    @pl.when(pl.program_id(1) == 0)
    def _():
        wb_ref[...] = w_ref[...].astype(BF16)

    contract = (((1,), (1,)), ((), ())) if w_transposed else (((1,), (0,)), ((), ()))
    acc = lax.dot_general(x_ref[...], wb_ref[...], contract, preferred_element_type=F32)
    if epilogue == "relu2":
        r = jnp.maximum(acc, 0.0)
        acc = r * r
    elif epilogue == "residual":
        acc = acc + res_ref[...]
    o_ref[...] = acc.astype(o_ref.dtype)


def matmul_ws(x, w3, layer, col0, ncols, out_dtype, tn, epilogue="none", residual=None, tm=1024, w_transposed=False):
    M, K = x.shape
    tm = min(tm, M)
    nj = -(-ncols // tn)
    jb = col0 // tn
    assert col0 % tn == 0 and M % tm == 0
    if w_transposed:
        w_spec = pl.BlockSpec((None, tn, K), lambda j, i: (layer, jb + j, 0))
        wb_shape = (tn, K)
    else:
        w_spec = pl.BlockSpec((None, K, tn), lambda j, i: (layer, 0, jb + j))
        wb_shape = (K, tn)
    in_specs = [pl.BlockSpec((tm, K), lambda j, i: (i, 0)), w_spec]
    args = [x, w3]
    if epilogue == "residual":
        in_specs.append(pl.BlockSpec((tm, tn), lambda j, i: (i, j)))
        args.append(residual)
    return pl.pallas_call(
        functools.partial(_mm_ws_body, epilogue=epilogue, w_transposed=w_transposed),
        grid=(nj, M // tm),
        in_specs=in_specs,
        out_specs=pl.BlockSpec((tm, tn), lambda j, i: (i, j)),
        out_shape=jax.ShapeDtypeStruct((M, nj * tn), out_dtype),
        scratch_shapes=[pltpu.VMEM(wb_shape, BF16)],
        compiler_params=_cparams(("parallel", "arbitrary")),
    )(*args)


def _diffattn_body(q_ref, k_ref, v_ref, lam_ref, g_ref, o_ref, m_ref, acc_ref,
                   sa_ref, sb_ref, pa_ref, pb_ref, aa_ref, ab_ref, *, tq, lam_init):
    tk = tq
    i = pl.program_id(1)
    hw = 2 * A_HDIM
    q = q_ref[...] * (A_HDIM ** -0.5)
    lane = lax.broadcasted_iota(jnp.int32, (tq, hw), 1)
    zero = jnp.zeros_like(q)
    qz = jnp.concatenate([jnp.where(lane < A_HDIM, q, zero), jnp.where(lane >= A_HDIM, q, zero)], axis=0)
    m_ref[...] = jnp.full_like(m_ref, NEG)
    acc_ref[...] = jnp.zeros_like(acc_ref)
    ones = jnp.ones((tk, hw), BF16)

    def scores(j):
        k = k_ref[pl.ds(pl.multiple_of(j * tk, tk), tk), :]
        return lax.dot_general(qz, k, (((1,), (1,)), ((), ())), preferred_element_type=F32)

    nt = pl.num_programs(1)
    key_minus_row = (lax.broadcasted_iota(jnp.int32, (SOFTMAX_ROWS, tk), 1)
                     - lax.broadcasted_iota(jnp.int32, (SOFTMAX_ROWS, tk), 0))
    tile_rows = lambda j: pl.ds(pl.multiple_of(jnp.minimum(j, nt - 1) * tk, tk), tk)

    def scores(j, s_out):
        s_out[...] = lax.dot_general(qz, k_ref[tile_rows(j), :], (((1,), (1,)), ((), ())),
                                     preferred_element_type=F32)

    def softmax(j, s_in, p_out, alpha_out):
        for r0 in range(0, 2 * tq, SOFTMAX_ROWS):
            rs = slice(r0, r0 + SOFTMAX_ROWS)
            causal = key_minus_row <= (i * tq + r0 % tq) - j * tk
            s = jnp.where(causal, s_in[rs, :], NEG)
            m_old = m_ref[rs, :]
            m_new = jnp.maximum(m_old, jnp.max(s, axis=-1, keepdims=True))
            m_ref[rs, :] = m_new
            p_out[rs, :] = jnp.exp(s - m_new).astype(BF16)
            alpha_out[rs, :] = jnp.exp(m_old - m_new)

    def accumulate(j, p_in, alpha_in):
        v_ext = jnp.concatenate([v_ref[tile_rows(j), :], ones], axis=-1)
        acc_ref[...] = alpha_in[...] * acc_ref[...] + jnp.dot(p_in[...], v_ext, preferred_element_type=F32)

    scores(0, sa_ref)
    pb_ref[...] = jnp.zeros_like(pb_ref)
    ab_ref[...] = jnp.ones_like(ab_ref)

    def body(t, carry):
        j = 2 * t
        accumulate(jnp.maximum(j - 1, 0), pb_ref, ab_ref)
        softmax(j, sa_ref, pa_ref, aa_ref)
        scores(j + 1, sb_ref)
        accumulate(j, pa_ref, aa_ref)
        softmax(j + 1, sb_ref, pb_ref, ab_ref)
        scores(j + 2, sa_ref)
        return carry

    npairs = (i + 2) // 2
    lax.fori_loop(0, npairs, body, 0)
    accumulate(2 * npairs - 1, pb_ref, ab_ref)

    lv = lam_ref[...]
    lam = (jnp.exp(jnp.sum(lv[0:1] * lv[1:2], axis=-1, keepdims=True))
           - jnp.exp(jnp.sum(lv[2:3] * lv[3:4], axis=-1, keepdims=True)) + lam_init)
    acc = acc_ref[...]
    on = acc[:, :hw] / acc[:, hw:]
    o = on[:tq] - lam * on[tq:]
    ms = jnp.mean(o * o, axis=-1, keepdims=True)
    o_ref[...] = (o * lax.rsqrt(ms + EPS) * g_ref[...] * (1.0 - lam_init)).astype(o_ref.dtype)


def diff_attention(qkv, lam_vecs, subln_g, lam_init, tq=512):
    S = qkv.shape[0]
    tq = min(tq, S)
    hw = 2 * A_HDIM
    return pl.pallas_call(
        functools.partial(_diffattn_body, tq=tq, lam_init=lam_init),
        grid=(A_HEADS, S // tq),
        in_specs=[pl.BlockSpec((tq, hw), lambda h, i: (i, h)),
                  pl.BlockSpec((S, hw), lambda h, i: (0, A_HEADS + h)),
                  pl.BlockSpec((S, hw), lambda h, i: (0, 2 * A_HEADS + h)),
                  pl.BlockSpec((4, A_HDIM), lambda h, i: (0, 0)),
                  pl.BlockSpec((1, hw), lambda h, i: (0, 0))],
        out_specs=pl.BlockSpec((tq, hw), lambda h, i: (i, h)),
        out_shape=jax.ShapeDtypeStruct((S, A_WIDTH), BF16),
        scratch_shapes=[pltpu.VMEM((2 * tq, 1), F32), pltpu.VMEM((2 * tq, 2 * hw), F32),
                        pltpu.VMEM((2 * tq, tq), F32), pltpu.VMEM((2 * tq, tq), F32),
                        pltpu.VMEM((2 * tq, tq), BF16), pltpu.VMEM((2 * tq, tq), BF16),
                        pltpu.VMEM((2 * tq, 1), F32), pltpu.VMEM((2 * tq, 1), F32)],
        compiler_params=_cparams(("parallel", "parallel")),
    )(qkv, qkv, qkv, lam_vecs, subln_g.reshape(1, hw))


KEY_NEG_INF = -2139095041


def _float_key(s):
    b = pltpu.bitcast(s, jnp.int32)
    return b ^ ((b >> 31) & jnp.int32(0x7FFFFFFF))


def _dsa_index_body(ki_ref, qi_ref, wiT_ref, bias_ref, key_ref, *, qb, kc, topk, nkc_total):
    pos_bits = (nkc_total * kc).bit_length()
    i = pl.program_id(0)
    nch = ((i + 1) * qb) // kc
    wi = wiT_ref[...] * (IDX_HEADS ** -0.5 * IDX_HDIM ** -0.5)
    qi = qi_ref[...]
    tpos = i * qb + lax.broadcasted_iota(jnp.int32, (kc, qb), 1)
    srow = lax.broadcasted_iota(jnp.int32, (kc, qb), 0)

    def score_chunk(c, carry):
        r0 = pl.multiple_of(c * kc, kc)
        kic = ki_ref[pl.ds(r0, kc), :]
        acc = jnp.zeros((kc, qb), F32)
        for h in range(IDX_HEADS):
            d = lax.dot_general(kic, qi[:, h * IDX_HDIM:(h + 1) * IDX_HDIM],
                                (((1,), (1,)), ((), ())), preferred_element_type=F32)
            acc = acc + jnp.maximum(d, 0.0) * wi[h:h + 1, :]
        acc = jnp.where(r0 + srow <= tpos, acc, -jnp.inf)
        key_ref[pl.ds(r0, kc), :] = _float_key(acc)
        return carry

    lax.fori_loop(0, nch, score_chunk, 0)

    def count(pred):
        def body(c, cnt):
            r0 = pl.multiple_of(c * kc, kc)
            hit = pred(key_ref[pl.ds(r0, kc), :], r0 + srow).astype(jnp.int32)
            return cnt + jnp.sum(hit.reshape(kc // 8, 8, qb), axis=0)
        cnt8 = lax.fori_loop(0, nch, body, jnp.zeros((8, qb), jnp.int32))
        return jnp.sum(cnt8, axis=0, keepdims=True)

    def bit_step(it, carry):
        tau, n_ge = carry
        cand = tau + (jnp.int32(1) << (31 - it))
        cnt = count(lambda key, pos: key >= cand)
        ok = cnt >= topk
        return jnp.where(ok, cand, tau), jnp.where(ok, cnt, n_ge)

    tau, n_ge = lax.fori_loop(0, 32, bit_step, (jnp.full((1, qb), jnp.iinfo(jnp.int32).min, jnp.int32),
                                                jnp.zeros((1, qb), jnp.int32)))
    n_ge = jnp.where(tau <= KEY_NEG_INF, 0, n_ge)
    tau = jnp.maximum(tau, KEY_NEG_INF + 1)

    tie_rounds = (jnp.max(n_ge) > topk).astype(jnp.int32)
    n_gt = lax.fori_loop(0, tie_rounds, lambda _, c: count(lambda key, pos: key > tau), jnp.zeros((1, qb), jnp.int32))
    quota = topk - n_gt

    def pos_step(it, pos_end):
        cand = pos_end + (jnp.int32(1) << (pos_bits - 1 - it))
        n_tie = count(lambda key, pos: (key == tau) & (pos < cand))
        return jnp.where(n_tie <= quota, cand, pos_end)

    pos_all = jnp.int32((1 << pos_bits) - 1)
    pos_end = lax.fori_loop(0, tie_rounds * pos_bits, pos_step,
                            jnp.full((1, qb), 1, jnp.int32) * (pos_all * (1 - tie_rounds)))

    def write_chunk(c, carry):
        r0 = pl.multiple_of(c * kc, kc)
        key = key_ref[pl.ds(r0, kc), :]
        tie_kept = (key == tau) & (r0 + srow < pos_end)
        bias = jnp.where(key > tau, 0.0, jnp.where(tie_kept, 0.0, NEG))
        bias_ref[pl.ds(r0, kc), :] = bias.astype(bias_ref.dtype)
        return carry

    lax.fori_loop(0, nch, write_chunk, 0)

    def fill_chunk(c, carry):
        r0 = pl.multiple_of(c * kc, kc)
        bias_ref[pl.ds(r0, kc), :] = jnp.full((kc, qb), NEG, bias_ref.dtype)
        return carry

    lax.fori_loop(nch, nkc_total, fill_chunk, 0)


def dsa_index(ki, qi, wiT, topk, qb=512, kc=256):
    S = ki.shape[0]
    qb = min(qb, S)
    kc = min(kc, qb)
    return pl.pallas_call(
        functools.partial(_dsa_index_body, qb=qb, kc=kc, topk=topk, nkc_total=S // kc),
        grid=(S // qb,),
        in_specs=[pl.BlockSpec((S, IDX_HDIM), lambda i: (0, 0)),
                  pl.BlockSpec((qb, IDX_HEADS * IDX_HDIM), lambda i: (i, 0)),
                  pl.BlockSpec((IDX_HEADS, qb), lambda i: (0, i))],
        out_specs=pl.BlockSpec((S, qb), lambda i: (0, i)),
        out_shape=jax.ShapeDtypeStruct((S, S), BF16),
        scratch_shapes=[pltpu.VMEM((S, qb), jnp.int32)],
        compiler_params=_cparams(("parallel",)),
    )(ki, qi, wiT)


def _dsa_attn_body(qi_ref, kj_ref, q_ref, k_ref, vT_ref, bias_ref, oT_ref, m_ref, l_ref, acc_ref, *, qb, kc):
    step = pl.program_id(0)
    i = qi_ref[step]
    j = kj_ref[step]
    scale = C_HDIM ** -0.5
    heads = [slice(h * C_HDIM, (h + 1) * C_HDIM) for h in range(C_HEADS)]

    @pl.when(j == 0)
    def _():
        m_ref[...] = jnp.full_like(m_ref, NEG)
        l_ref[...] = jnp.zeros_like(l_ref)
        acc_ref[...] = jnp.zeros_like(acc_ref)

    bias = bias_ref[...].astype(F32)
    sT = [lax.dot_general(k_ref[:, cs], q_ref[:, cs], (((1,), (1,)), ((), ())),
                          preferred_element_type=F32) * scale + bias for cs in heads]
    m_old = m_ref[...]
    m_new = jnp.maximum(m_old, jnp.concatenate([jnp.max(s, axis=0, keepdims=True) for s in sT], axis=0))
    alpha = jnp.exp(m_old - m_new)
    p = [jnp.exp(s - m_new[h:h + 1, :]) for h, s in enumerate(sT)]
    l_ref[...] = alpha * l_ref[...] + jnp.concatenate([jnp.sum(x, axis=0, keepdims=True) for x in p], axis=0)
    m_ref[...] = m_new
    for h, cs in enumerate(heads):
        acc_ref[cs, :] = alpha[h:h + 1, :] * acc_ref[cs, :] + jnp.dot(vT_ref[cs, :], p[h].astype(BF16),
                                                                       preferred_element_type=F32)

    @pl.when(j == ((i + 1) * qb - 1) // kc)
    def _():
        l = l_ref[...]
        for h, cs in enumerate(heads):
            oT_ref[cs, :] = (acc_ref[cs, :] / l[h:h + 1, :]).astype(oT_ref.dtype)


def dsa_attention(qkv, vT, bias, qb=256, kc=512):
    S = qkv.shape[0]
    qb = min(qb, S)
    kc = min(kc, S)
    pairs = [(i, j) for i in range(S // qb) for j in range(((i + 1) * qb - 1) // kc + 1)]
    qi = jnp.asarray([p_[0] for p_ in pairs], jnp.int32)
    kj = jnp.asarray([p_[1] for p_ in pairs], jnp.int32)
    grid_spec = pltpu.PrefetchScalarGridSpec(
        num_scalar_prefetch=2,
        grid=(len(pairs),),
        in_specs=[pl.BlockSpec((qb, C_WIDTH), lambda s, qi, kj: (qi[s], 0)),
                  pl.BlockSpec((kc, C_WIDTH), lambda s, qi, kj: (kj[s], 1)),
                  pl.BlockSpec((C_WIDTH, kc), lambda s, qi, kj: (0, kj[s])),
                  pl.BlockSpec((kc, qb), lambda s, qi, kj: (kj[s], qi[s]))],
        out_specs=pl.BlockSpec((C_WIDTH, qb), lambda s, qi, kj: (0, qi[s])),
        scratch_shapes=[pltpu.VMEM((C_HEADS, qb), F32), pltpu.VMEM((C_HEADS, qb), F32),
                        pltpu.VMEM((C_WIDTH, qb), F32)])
    return pl.pallas_call(
        functools.partial(_dsa_attn_body, qb=qb, kc=kc),
        grid_spec=grid_spec,
        out_shape=jax.ShapeDtypeStruct((C_WIDTH, S), BF16),
        compiler_params=_cparams(("arbitrary",)),
    )(qi, kj, qkv, qkv, vT, bias)


def _sigmoid(x):
    return 1.0 / (1.0 + jnp.exp(-x))


def _shift_rows(cur, prev_ref, first_tile):
    prev_row = jnp.where(first_tile, 0.0, prev_ref[7:8, :].astype(F32))
    rolled = pltpu.roll(cur, 1, axis=0)
    row = lax.broadcasted_iota(jnp.int32, cur.shape, 0)
    return jnp.where(row == 0, prev_row, rolled)


def _rwkv_lora_body(*refs, has_v):
    if has_v:
        (h_ref, hp_ref, mu_ref, w1_ref, a1_ref, g1_ref, v1_ref, ow_ref, oa_ref, og_ref, ov_ref) = refs
    else:
        (h_ref, hp_ref, mu_ref, w1_ref, a1_ref, g1_ref, ow_ref, oa_ref, og_ref) = refs
    i = pl.program_id(0)
    h = h_ref[...].astype(F32)
    dh = _shift_rows(h, hp_ref, i == 0) - h

    def lora(row, w_ref):
        xm = (h + dh * mu_ref[row:row + 1, :]).astype(BF16)
        return jnp.dot(xm, w_ref[...], preferred_element_type=F32)

    ow_ref[...] = jnp.tanh(lora(0, w1_ref))
    oa_ref[...] = lora(1, a1_ref)
    og_ref[...] = _sigmoid(lora(2, g1_ref))
    if has_v:
        ov_ref[...] = lora(3, v1_ref)


def rwkv_lora(h, mu, w1, a1, g1, v1, tm=256):
    S, D = h.shape
    tm = min(tm, S)
    has_v = v1 is not None
    ws = [w1, a1, g1] + ([v1] if has_v else [])
    full = lambda a: pl.BlockSpec(a.shape, lambda i: (0, 0))
    return pl.pallas_call(
        functools.partial(_rwkv_lora_body, has_v=has_v),
        grid=(S // tm,),
        in_specs=[pl.BlockSpec((tm, D), lambda i: (i, 0)),
                  pl.BlockSpec((8, D), lambda i: (jnp.maximum(i * (tm // 8) - 1, 0), 0)),
                  full(mu)] + [full(w) for w in ws],
        out_specs=[pl.BlockSpec((tm, w.shape[1]), lambda i: (i, 0)) for w in ws],
        out_shape=[jax.ShapeDtypeStruct((S, w.shape[1]), F32) for w in ws],
        compiler_params=_cparams(("parallel",)),
    )(h, h, mu, *ws)


(V_MU_R, V_MU_K, V_MU_V, V_W0, V_A0, V_KK, V_KA, V_RK, V_V0) = range(9)
N_VEC_ROWS = 16


def _wkv_pre_body(*refs, has_v, tt):
    if has_v:
        (r_ref, k_ref, v_ref, rp_ref, kp_ref, vp_ref, hw_ref, ha_ref, hg_ref, w2_ref, a2_ref, g2_ref,
         vec_ref, hv_ref, v2_ref, vf_ref,
         m_out, g_out, q_out, z_out, vout_ref, gate_ref, bonus_ref) = refs
    else:
        (r_ref, k_ref, v_ref, rp_ref, kp_ref, vp_ref, hw_ref, ha_ref, hg_ref, w2_ref, a2_ref, g2_ref,
         vec_ref,
         m_out, g_out, q_out, z_out, vout_ref, gate_ref, bonus_ref) = refs
    i = pl.program_id(0)
    first = i == 0
    C = WKV_CHUNK
    N = B_HDIM
    vec = lambda row: vec_ref[row:row + 1, :]

    r = r_ref[...]
    k = k_ref[...]
    v = v_ref[...]
    r = r + (_shift_rows(r, rp_ref, first) - r) * vec(V_MU_R)
    k = k + (_shift_rows(k, kp_ref, first) - k) * vec(V_MU_K)
    v = v + (_shift_rows(v, vp_ref, first) - v) * vec(V_MU_V)

    wl = vec(V_W0) + jnp.dot(hw_ref[...], w2_ref[...], preferred_element_type=F32, precision=HI)
    z = -wl
    softplus = jnp.maximum(z, 0.0) + jnp.log(1.0 + jnp.exp(-jnp.abs(z)))
    logw = -jnp.exp(-softplus - 0.5)
    bdot = lambda a_ref, b_ref: jnp.dot(a_ref[...].astype(BF16), b_ref[...].astype(BF16), preferred_element_type=F32)
    a_sig = _sigmoid(vec(V_A0) + bdot(ha_ref, a2_ref))
    gate = bdot(hg_ref, g2_ref)
    if has_v:
        mix = _sigmoid(vec(V_V0) + bdot(hv_ref, v2_ref))
        v = v + (vf_ref[...] - v) * mix
    vout_ref[...] = v
    gate_ref[...] = gate

    kk = k * vec(V_KK)
    k_new = k * (1.0 + (a_sig - 1.0) * vec(V_KA))
    rk = r * k_new * vec(V_RK)

    rowc = lax.broadcasted_iota(jnp.int32, (C, C), 0)
    colc = lax.broadcasted_iota(jnp.int32, (C, C), 1)
    tril_incl = (rowc >= colc).astype(F32)
    prow = lax.broadcasted_iota(jnp.int32, (2 * C, 2 * C), 0)
    pcol = lax.broadcasted_iota(jnp.int32, (2 * C, 2 * C), 1)
    same_head = (prow >= C) == (pcol >= C)
    dstep = jnp.where(prow >= C, prow - C, prow) - jnp.where(pcol >= C, pcol - C, pcol)
    strict_bd = jnp.where(same_head, dstep, -1) > 0
    incl_bd = jnp.where(same_head, dstep, -1) >= 0
    eye_p = (prow == pcol).astype(F32)
    head_ones = jnp.where(same_head, 1.0, 0.0).astype(BF16)

    def head_sum(x):
        hi = x.astype(BF16)
        lo = (x - hi.astype(F32)).astype(BF16)
        return (jnp.dot(hi, head_ones, preferred_element_type=F32)
                + jnp.dot(lo, head_ones, preferred_element_type=F32))

    kkn = kk / jnp.maximum(jnp.sqrt(head_sum(kk * kk)), 1e-12)
    bonus_ref[...] = head_sum(rk) * v
    a_scan = -kkn
    b_scan = kkn * a_sig

    head0 = lax.broadcasted_iota(jnp.int32, (C, LANES), 1) < N

    def stack2(x):
        return jnp.concatenate([jnp.where(head0, x, 0.0), jnp.where(head0, 0.0, x)], axis=0)

    ts_of = lambda c: slice(c * C, (c + 1) * C)
    st = []
    for c in range(tt // C):
        ts = ts_of(c)
        lw = logw[ts]
        cum = jnp.dot(tril_incl, lw, preferred_element_type=F32, precision=HI)
        cum_last = cum[C - 1:C, :]
        e_neg = jnp.exp(-cum)
        e_end = jnp.exp(cum_last - cum)
        st.append(dict(At=stack2(a_scan[ts] * jnp.exp(cum - lw)), Rt=stack2(r[ts] * jnp.exp(cum)),
                       Bt=b_scan[ts] * e_neg, Kt=k_new[ts] * e_neg,
                       Bg=stack2(b_scan[ts] * e_end), Kg=stack2(k_new[ts] * e_end),
                       V=stack2(v[ts]).astype(BF16), gam=jnp.exp(cum_last)))
    for u in st:
        left = jnp.concatenate([u["At"], u["Rt"]], axis=0).astype(BF16)
        right = jnp.concatenate([u["Bt"], u["Bt"], u["Kt"], u["Kt"]], axis=0).astype(BF16)
        AA = lax.dot_general(left, right, (((1,), (1,)), ((), ())), preferred_element_type=F32)
        u["A_ab"] = jnp.where(strict_bd, AA[:2 * C, :2 * C], 0.0)
        u["A_ak"] = jnp.where(strict_bd, AA[:2 * C, 2 * C:], 0.0)
        u["A_r"] = jnp.concatenate([jnp.where(incl_bd, AA[2 * C:, :2 * C], 0.0),
                                    jnp.where(incl_bd, AA[2 * C:, 2 * C:], 0.0)], axis=-1).astype(BF16)
    for u in st:
        akv = jnp.dot(u["A_ak"].astype(BF16), u["V"], preferred_element_type=F32)
        u["XA"] = jnp.concatenate([u["At"] + pltpu.roll(akv, N, axis=1), u["A_ab"]], axis=-1)
    for step in range(6):
        for u in st:
            XA = u["XA"]
            Ap = XA[:, LANES:].astype(BF16)
            if step < 5:
                prod = jnp.dot(Ap, XA.astype(BF16), preferred_element_type=F32)
                u["XA"] = jnp.concatenate([XA[:, :LANES] + prod[:, :LANES], prod[:, LANES:]], axis=-1)
            else:
                u["X"] = XA[:, :LANES] + jnp.dot(Ap, XA[:, :LANES].astype(BF16), preferred_element_type=F32)
    own = (lax.broadcasted_iota(jnp.int32, (2 * C, LANES), 0) >= C) == (lax.broadcasted_iota(jnp.int32, (2 * C, LANES), 1) >= N)
    for u in st:
        p1 = jnp.where(own, u["X"], 0.0).astype(BF16)
        p2 = pltpu.roll(jnp.where(own, 0.0, u["X"]), N, axis=1).astype(BF16)
        upper = jnp.concatenate([p1, p2], axis=-1)
        lower = jnp.concatenate([jnp.zeros((2 * C, LANES), BF16), u["V"]], axis=-1)
        u["W2"] = jnp.concatenate([upper, lower], axis=0)
    for u in st:
        bk = jnp.concatenate([u["Bg"], u["Kg"]], axis=0).astype(BF16)
        u["MG"] = lax.dot_general(bk, u["W2"], (((0,), (0,)), ((), ())), preferred_element_type=F32)
    for u in st:
        u["QZ"] = jnp.dot(u["A_r"], u["W2"], preferred_element_type=F32)
    for c, u in enumerate(st):
        rows = slice(c * LANES, (c + 1) * LANES)
        m_out[rows, :] = (eye_p * u["gam"] + u["MG"][:, :LANES]).astype(m_out.dtype)
        g_out[rows, :] = u["MG"][:, LANES:].astype(g_out.dtype)
        qs = u["Rt"] + u["QZ"][:, :LANES]
        zs = u["QZ"][:, LANES:]
        q_out[ts_of(c), :] = (qs[:C] + qs[C:]).astype(q_out.dtype)
        z_out[ts_of(c), :] = zs[:C] + zs[C:]


def wkv_pre(proj_b, hids, w2s, vecs, v_first, tt=1024):
    S = proj_b.shape[0]
    tt = min(tt, S)
    has_v = v_first is not None
    nb = B_WIDTH // LANES
    nstate_rows = (tt // WKV_CHUNK) * LANES
    tok = lambda off: pl.BlockSpec((tt, LANES), lambda i, p: (i, off + p))
    prev = lambda off: pl.BlockSpec((8, LANES), lambda i, p: (jnp.maximum(i * (tt // 8) - 1, 0), off + p))
    hid = lambda a: pl.BlockSpec((tt, a.shape[1]), lambda i, p: (i, 0))
    wcol = lambda a: pl.BlockSpec((a.shape[0], LANES), lambda i, p: (0, p))
    in_specs = [tok(0), tok(nb), tok(2 * nb), prev(0), prev(nb), prev(2 * nb),
                hid(hids[0]), hid(hids[1]), hid(hids[2]), wcol(w2s[0]), wcol(w2s[1]), wcol(w2s[2]),
                pl.BlockSpec((N_VEC_ROWS, LANES), lambda i, p: (0, p))]
    args = [proj_b] * 6 + list(hids[:3]) + list(w2s[:3]) + [vecs]
    if has_v:
        in_specs += [hid(hids[3]), wcol(w2s[3]), tok(0)]
        args += [hids[3], w2s[3], v_first]
    state_shape = jax.ShapeDtypeStruct(((S // WKV_CHUNK) * LANES, B_WIDTH), BF16)
    out_tok = pl.BlockSpec((tt, LANES), lambda i, p: (i, p))
    out_st = pl.BlockSpec((nstate_rows, LANES), lambda i, p: (i, p))
    return pl.pallas_call(
        functools.partial(_wkv_pre_body, has_v=has_v, tt=tt),
        grid=(S // tt, nb),
        in_specs=in_specs,
        out_specs=[out_st, out_st, out_tok, out_tok, out_tok, out_tok, out_tok],
        out_shape=[state_shape, state_shape, jax.ShapeDtypeStruct((S, B_WIDTH), BF16)]
                  + [jax.ShapeDtypeStruct((S, B_WIDTH), F32)] * 4,
        compiler_params=_cparams(("parallel", "parallel")),
    )(*args)


def _wkv_scan_body(m_ref, g_ref, q_ref, z_ref, gate_ref, bonus_ref, ln_ref, o_ref, h_ref, *, tt):
    i = pl.program_id(0)
    C = WKV_CHUNK
    N = B_HDIM
    npair = B_WIDTH // LANES

    @pl.when(i == 0)
    def _():
        h_ref[...] = jnp.zeros_like(h_ref)

    row = lax.broadcasted_iota(jnp.int32, (LANES, LANES), 0)
    col = lax.broadcasted_iota(jnp.int32, (LANES, LANES), 1)
    head_avg = jnp.where((row // N) == (col // N), 1.0 / N, 0.0).astype(BF16)

    def head_mean(x):
        hi = x.astype(BF16)
        lo = (x - hi.astype(F32)).astype(BF16)
        return (jnp.dot(hi, head_avg, preferred_element_type=F32)
                + jnp.dot(lo, head_avg, preferred_element_type=F32))

    for c in range(tt // C):
        ts = slice(c * C, (c + 1) * C)
        ss = slice(c * LANES, (c + 1) * LANES)
        ys = []
        for p in range(npair):
            ps = slice(p * LANES, (p + 1) * LANES)
            Hb = h_ref[p].astype(BF16)
            ys.append(jnp.dot(q_ref[ts, ps], Hb, preferred_element_type=F32) + z_ref[ts, ps])
            h_ref[p] = jnp.dot(m_ref[ss, ps], Hb, preferred_element_type=F32) + g_ref[ss, ps].astype(F32)
        for p in range(npair):
            ps = slice(p * LANES, (p + 1) * LANES)
            yc = ys[p] - head_mean(ys[p])
            var = head_mean(yc * yc)
            yn = yc * lax.rsqrt(var + B_GN_EPS) * ln_ref[0:1, ps] + ln_ref[1:2, ps]
            o_ref[ts, ps] = ((yn + bonus_ref[ts, ps]) * gate_ref[ts, ps]).astype(o_ref.dtype)


def wkv_scan(m, g, q, z, gate, bonus, ln, tt=256):
    S = q.shape[0]
    tt = min(tt, S)
    npair = B_WIDTH // LANES
    nstate_rows = (tt // WKV_CHUNK) * LANES
    tok = pl.BlockSpec((tt, B_WIDTH), lambda i: (i, 0))
    st = pl.BlockSpec((nstate_rows, B_WIDTH), lambda i: (i, 0))
    return pl.pallas_call(
        functools.partial(_wkv_scan_body, tt=tt),
        grid=(S // tt,),
        in_specs=[st, st, tok, tok, tok, tok, pl.BlockSpec((8, B_WIDTH), lambda i: (0, 0))],
        out_specs=tok,
        out_shape=jax.ShapeDtypeStruct((S, B_WIDTH), BF16),
        scratch_shapes=[pltpu.VMEM((npair, LANES, LANES), F32)],
        compiler_params=_cparams(("arbitrary",)),
    )(m, g, q, z, gate, bonus, ln)


def rwkv7_mixer(h, proj_b, p, v_first):
    has_v = v_first is not None
    mu = p["mu_wag"] if not has_v else jnp.concatenate([p["mu_wag"], p["v_mu"][None]], axis=0)
    hids = rwkv_lora(h, mu, p["w1"].astype(BF16), p["a1"].astype(BF16), p["g1"].astype(BF16),
                     p["v1"].astype(BF16) if has_v else None)
    rows = [p["mu_rkv"][0], p["mu_rkv"][1], p["mu_rkv"][2], p["w0"], p["a0"], p["k_k"], p["k_a"],
            p["r_k"].reshape(-1), p["v0"] if has_v else jnp.zeros((B_WIDTH,), F32)]
    vecs = jnp.concatenate([jnp.stack(rows), jnp.zeros((N_VEC_ROWS - len(rows), B_WIDTH), F32)], axis=0)
    w2s = [p["w2"], p["a2"], p["g2"]] + ([p["v2"]] if has_v else [])
    m, g, q, z, v_out, gate, bonus = wkv_pre(proj_b, hids, w2s, vecs, v_first)
    ln = jnp.concatenate([p["ln_w"][None], p["ln_b"][None], jnp.zeros((6, B_WIDTH), F32)], axis=0)
    return wkv_scan(m, g, q, z, gate, bonus, ln), v_out


def kernel(x, norm_mix_g, w_in, lam_q1, lam_k1, lam_q2, lam_k2, diff_subln_g, rw_mu_rkv, rw_mu_wag, rw_w0, rw_w1, rw_w2, rw_a0, rw_a1, rw_a2, rw_g1, rw_g2, rw_k_k, rw_k_a, rw_r_k, rw_ln_w, rw_ln_b, rw_v_mu, rw_v0, rw_v1, rw_v2, w_out, norm_ffn_g, w_up, w_down, norm_final_g):
    Bsz, S, D = x.shape
    depth = w_in.shape[0]
    topk = min(TOPK_MAX, S // 4)
    w_in_t = jnp.swapaxes(w_in, 1, 2)
    w_down_bf = w_down.astype(BF16)
    outs = []
    for b in range(Bsz):
        xb = x[b]
        v_first = None
        for l in range(depth):
            h = rmsnorm(xb, norm_mix_g[l], BF16)
            oA, oB, oC = 3 * A_WIDTH, 3 * A_WIDTH + 3 * B_WIDTH, 3 * A_WIDTH + 3 * B_WIDTH + 3 * C_WIDTH
            proj_a = matmul_ws(h, w_in_t, l, 0, oA, BF16, tn=512, w_transposed=True)
            proj_b = matmul_ws(h, w_in_t, l, oA, oB - oA, F32, tn=512, w_transposed=True)
            proj_c = matmul_ws(h, w_in_t, l, oB, oC - oB, BF16, tn=512, w_transposed=True)
            proj_i = matmul_ws(h, w_in_t, l, oC, w_in.shape[2] - oC, F32, tn=384, w_transposed=True)

            lam_init = 0.8 - 0.6 * math.exp(-0.3 * l)
            lam_vecs = jnp.stack([lam_q1[l], lam_k1[l], lam_q2[l], lam_k2[l]])
            o_a = diff_attention(proj_a, lam_vecs, diff_subln_g[l], lam_init)

            p = dict(mu_rkv=rw_mu_rkv[l], mu_wag=rw_mu_wag[l], w0=rw_w0[l], w1=rw_w1[l], w2=rw_w2[l],
                     a0=rw_a0[l], a1=rw_a1[l], a2=rw_a2[l], g1=rw_g1[l], g2=rw_g2[l], k_k=rw_k_k[l],
                     k_a=rw_k_a[l], r_k=rw_r_k[l], ln_w=rw_ln_w[l], ln_b=rw_ln_b[l])
            if l > 0:
                p.update(v_mu=rw_v_mu[l - 1], v0=rw_v0[l - 1], v1=rw_v1[l - 1], v2=rw_v2[l - 1])
            o_b, v_out = rwkv7_mixer(h, proj_b, p, v_first if l > 0 else None)
            if l == 0:
                v_first = v_out

            nq = IDX_HEADS * IDX_HDIM
            qi = proj_i[:, :nq].astype(BF16)
            ki = proj_i[:, nq:nq + IDX_HDIM].astype(BF16)
            wiT = proj_i[:, nq + IDX_HDIM:nq + IDX_HDIM + IDX_HEADS].T
            bias = dsa_index(ki, qi, wiT, topk)
            vT = proj_c[:, 2 * C_WIDTH:].T
            o_c = dsa_attention(proj_c, vT, bias).T

            mixed = jnp.concatenate([o_a, o_b, o_c], axis=-1)
            xb = matmul_ws(mixed, w_out, l, 0, D, F32, tn=512, epilogue="residual", residual=xb)
            h2 = rmsnorm(xb, norm_ffn_g[l], BF16)
            up = matmul_ws(h2, w_up, l, 0, w_up.shape[2], BF16, tn=1024, tm=512, epilogue="relu2")
            xb = matmul(up, w_down_bf, l, F32, epilogue="residual", residual=xb)
        outs.append(rmsnorm(xb, norm_final_g, F32))
    return jnp.stack(outs)
```

```python
import functools
import math

import jax
import jax.numpy as jnp
from jax import lax
from jax.experimental import pallas as pl
from jax.experimental.pallas import tpu as pltpu

F32 = jnp.float32
BF16 = jnp.bfloat16

A_HEADS, A_HDIM = 8, 64
A_WIDTH = A_HEADS * 2 * A_HDIM
B_HDIM, B_WIDTH = 64, 2048
B_HEADS = B_WIDTH // B_HDIM
B_GN_EPS = 64e-5
C_HEADS, C_HDIM = 8, 128
C_WIDTH = C_HEADS * C_HDIM
IDX_HEADS, IDX_HDIM = 16, 64
TOPK_MAX = 256
EPS = 1e-6

LANES = 128
VMEM_LIMIT = 56 * 1024 * 1024
NEG = -1e30
WKV_CHUNK = 64
SOFTMAX_ROWS = 64
HI = lax.Precision.HIGHEST


def _cparams(sem):
    return pltpu.CompilerParams(dimension_semantics=sem, vmem_limit_bytes=VMEM_LIMIT)


def _rmsnorm_body(x_ref, g_ref, o_ref):
    x = x_ref[...]
    ms = jnp.mean(x * x, axis=-1, keepdims=True)
    o_ref[...] = (x * lax.rsqrt(ms + EPS) * g_ref[...]).astype(o_ref.dtype)


def rmsnorm(x, g, out_dtype, tm=256):
    S, D = x.shape
    tm = min(tm, S)
    return pl.pallas_call(
        _rmsnorm_body,
        grid=(S // tm,),
        in_specs=[pl.BlockSpec((tm, D), lambda i: (i, 0)),
                  pl.BlockSpec((1, D), lambda i: (0, 0))],
        out_specs=pl.BlockSpec((tm, D), lambda i: (i, 0)),
        out_shape=jax.ShapeDtypeStruct((S, D), out_dtype),
        compiler_params=_cparams(("parallel",)),
    )(x, g.reshape(1, D))


def _mm_body(a_ref, b_ref, *rest, nk, epilogue):
    if epilogue == "residual":
        res_ref, o_ref, acc_ref = rest
    else:
        o_ref, acc_ref = rest
    k = pl.program_id(2)

    @pl.when(k == 0)
    def _():
        acc_ref[...] = jnp.zeros_like(acc_ref)

    acc_ref[...] += jnp.dot(a_ref[...], b_ref[...], preferred_element_type=F32)

    @pl.when(k == nk - 1)
    def _():
        acc = acc_ref[...]
        if epilogue == "relu2":
            r = jnp.maximum(acc, 0.0)
            acc = r * r
        elif epilogue == "residual":
            acc = acc + res_ref[...]
        o_ref[...] = acc.astype(o_ref.dtype)


def _pick(n, pref):
    for t in pref:
        if n % t == 0:
            return t
    return n


def matmul(a, b3, layer, out_dtype, epilogue="none", residual=None):
    M, K = a.shape
    _, _, N = b3.shape
    tm = _pick(M, (1024, 512, 256))
    tn = _pick(N, (1024, 768, 512, 384, 256, 128))
    tk = _pick(K, (2048, 1024, 512))
    nk = K // tk
    in_specs = [pl.BlockSpec((tm, tk), lambda i, j, k: (i, k)),
                pl.BlockSpec((None, tk, tn), lambda i, j, k: (layer, k, j))]
    args = [a, b3]
    if epilogue == "residual":
        in_specs.append(pl.BlockSpec((tm, tn), lambda i, j, k: (i, j)))
        args.append(residual)
    return pl.pallas_call(
        functools.partial(_mm_body, nk=nk, epilogue=epilogue),
        grid=(M // tm, N // tn, nk),
        in_specs=in_specs,
        out_specs=pl.BlockSpec((tm, tn), lambda i, j, k: (i, j)),
        out_shape=jax.ShapeDtypeStruct((M, N), out_dtype),
        scratch_shapes=[pltpu.VMEM((tm, tn), F32)],
        compiler_params=_cparams(("parallel", "parallel", "arbitrary")),
    )(*args)


def _mm_ws_body(x_ref, w_ref, *rest, epilogue, w_transposed):
    if epilogue == "residual":
        res_ref, o_ref, wb_ref = rest
    else:
        o_ref, wb_ref = rest

    @pl.when(pl.program_id(1) == 0)
    def _():
        wb_ref[...] = w_ref[...].astype(BF16)

    contract = (((1,), (1,)), ((), ())) if w_transposed else (((1,), (0,)), ((), ()))
    acc = lax.dot_general(x_ref[...], wb_ref[...], contract, preferred_element_type=F32)
    if epilogue == "relu2":
        r = jnp.maximum(acc, 0.0)
        acc = r * r
    elif epilogue == "residual":
        acc = acc + res_ref[...]
    o_ref[...] = acc.astype(o_ref.dtype)


def matmul_ws(x, w3, layer, col0, ncols, out_dtype, tn, epilogue="none", residual=None, tm=1024, w_transposed=False):
    M, K = x.shape
    tm = min(tm, M)
    nj = ncols // tn
    jb = col0 // tn
    assert col0 % tn == 0 and ncols % tn == 0 and M % tm == 0
    if w_transposed:
        w_spec = pl.BlockSpec((None, tn, K), lambda j, i: (layer, jb + j, 0))
        wb_shape = (tn, K)
    else:
        w_spec = pl.BlockSpec((None, K, tn), lambda j, i: (layer, 0, jb + j))
        wb_shape = (K, tn)
    in_specs = [pl.BlockSpec((tm, K), lambda j, i: (i, 0)), w_spec]
    args = [x, w3]
    if epilogue == "residual":
        in_specs.append(pl.BlockSpec((tm, tn), lambda j, i: (i, j)))
        args.append(residual)
    return pl.pallas_call(
        functools.partial(_mm_ws_body, epilogue=epilogue, w_transposed=w_transposed),
        grid=(nj, M // tm),
        in_specs=in_specs,
        out_specs=pl.BlockSpec((tm, tn), lambda j, i: (i, j)),
        out_shape=jax.ShapeDtypeStruct((M, nj * tn), out_dtype),
        scratch_shapes=[pltpu.VMEM(wb_shape, BF16)],
        compiler_params=_cparams(("parallel", "arbitrary")),
    )(*args)


def _diffattn_body(q_ref, k_ref, v_ref, lam_ref, g_ref, o_ref, m_ref, acc_ref,
                   sa_ref, sb_ref, pa_ref, pb_ref, aa_ref, ab_ref, *, tq, lam_init):
    tk = tq
    i = pl.program_id(1)
    hw = 2 * A_HDIM
    q = q_ref[...] * (A_HDIM ** -0.5)
    lane = lax.broadcasted_iota(jnp.int32, (tq, hw), 1)
    zero = jnp.zeros_like(q)
    qz = jnp.concatenate([jnp.where(lane < A_HDIM, q, zero), jnp.where(lane >= A_HDIM, q, zero)], axis=0)
    m_ref[...] = jnp.full_like(m_ref, NEG)
    acc_ref[...] = jnp.zeros_like(acc_ref)
    ones = jnp.ones((tk, hw), BF16)

    def scores(j):
        k = k_ref[pl.ds(pl.multiple_of(j * tk, tk), tk), :]
        return lax.dot_general(qz, k, (((1,), (1,)), ((), ())), preferred_element_type=F32)

    nt = pl.num_programs(1)
    key_minus_row = (lax.broadcasted_iota(jnp.int32, (SOFTMAX_ROWS, tk), 1)
                     - lax.broadcasted_iota(jnp.int32, (SOFTMAX_ROWS, tk), 0))
    tile_rows = lambda j: pl.ds(pl.multiple_of(jnp.minimum(j, nt - 1) * tk, tk), tk)

    def scores(j, s_out):
        s_out[...] = lax.dot_general(qz, k_ref[tile_rows(j), :], (((1,), (1,)), ((), ())),
                                     preferred_element_type=F32)

    def softmax(j, s_in, p_out, alpha_out, masked):
        for r0 in range(0, 2 * tq, SOFTMAX_ROWS):
            rs = slice(r0, r0 + SOFTMAX_ROWS)
            m_old = m_ref[rs, :]
            if masked:
                causal = key_minus_row <= (i * tq + r0 % tq) - j * tk
                row_max = jnp.max(jnp.where(causal, s_in[rs, :], NEG), axis=-1, keepdims=True)
            else:
                row_max = jnp.max(s_in[rs, :], axis=-1, keepdims=True)
            m_new = jnp.maximum(m_old, row_max)
            m_ref[rs, :] = m_new
            e = jnp.exp(s_in[rs, :] - jnp.concatenate([m_new] * (tk // hw), axis=-1))
            p_out[rs, :] = (jnp.where(causal, e, 0.0) if masked else e).astype(BF16)
            alpha_out[rs, :] = jnp.exp(m_old - m_new)

    def accumulate(j, p_in, alpha_in):
        v_ext = jnp.concatenate([v_ref[tile_rows(j), :], ones], axis=-1)
        alpha = alpha_in[...]
        acc_ref[...] = (jnp.concatenate([alpha, alpha], axis=-1) * acc_ref[...]
                        + jnp.dot(p_in[...], v_ext, preferred_element_type=F32))

    scores(0, sa_ref)
    pb_ref[...] = jnp.zeros_like(pb_ref)
    ab_ref[...] = jnp.ones_like(ab_ref)

    def pair(t, masked):
        j = 2 * t
        accumulate(jnp.maximum(j - 1, 0), pb_ref, ab_ref)
        softmax(j, sa_ref, pa_ref, aa_ref, masked)
        scores(j + 1, sb_ref)
        accumulate(j, pa_ref, aa_ref)
        softmax(j + 1, sb_ref, pb_ref, ab_ref, masked)
        scores(j + 2, sa_ref)

    npairs = (i + 2) // 2
    lax.fori_loop(0, npairs - 1, lambda t, c: (pair(t, False), c)[1], 0)
    pair(npairs - 1, True)
    accumulate(2 * npairs - 1, pb_ref, ab_ref)

    lv = lam_ref[...]
    lam = (jnp.exp(jnp.sum(lv[0:1] * lv[1:2], axis=-1, keepdims=True))
           - jnp.exp(jnp.sum(lv[2:3] * lv[3:4], axis=-1, keepdims=True)) + lam_init)
    acc = acc_ref[...]
    on = acc[:, :hw] / acc[:, hw:]
    o = on[:tq] - lam * on[tq:]
    ms = jnp.mean(o * o, axis=-1, keepdims=True)
    o_ref[...] = (o * lax.rsqrt(ms + EPS) * g_ref[...] * (1.0 - lam_init)).astype(o_ref.dtype)


def diff_attention(qkv, lam_vecs, subln_g, lam_init, tq=512):
    S = qkv.shape[0]
    tq = min(tq, S)
    hw = 2 * A_HDIM
    return pl.pallas_call(
        functools.partial(_diffattn_body, tq=tq, lam_init=lam_init),
        grid=(A_HEADS, S // tq),
        in_specs=[pl.BlockSpec((tq, hw), lambda h, i: (i, h)),
                  pl.BlockSpec((S, hw), lambda h, i: (0, A_HEADS + h)),
                  pl.BlockSpec((S, hw), lambda h, i: (0, 2 * A_HEADS + h)),
                  pl.BlockSpec((4, A_HDIM), lambda h, i: (0, 0)),
                  pl.BlockSpec((1, hw), lambda h, i: (0, 0))],
        out_specs=pl.BlockSpec((tq, hw), lambda h, i: (i, h)),
        out_shape=jax.ShapeDtypeStruct((S, A_WIDTH), BF16),
        scratch_shapes=[pltpu.VMEM((2 * tq, hw), F32), pltpu.VMEM((2 * tq, 2 * hw), F32),
                        pltpu.VMEM((2 * tq, tq), F32), pltpu.VMEM((2 * tq, tq), F32),
                        pltpu.VMEM((2 * tq, tq), BF16), pltpu.VMEM((2 * tq, tq), BF16),
                        pltpu.VMEM((2 * tq, hw), F32), pltpu.VMEM((2 * tq, hw), F32)],
        compiler_params=_cparams(("parallel", "parallel")),
    )(qkv, qkv, qkv, lam_vecs, subln_g.reshape(1, hw))


KEY_NEG_INF = -2139095041


def _float_key(s):
    b = pltpu.bitcast(s, jnp.int32)
    return b ^ ((b >> 31) & jnp.int32(0x7FFFFFFF))


def _dsa_index_body(ki_ref, qi_ref, wiT_ref, bias_ref, key_ref, *, qb, kc, topk, nkc_total):
    pos_bits = (nkc_total * kc).bit_length()
    i = pl.program_id(0)
    nch = ((i + 1) * qb) // kc
    wi = wiT_ref[...] * (IDX_HEADS ** -0.5 * IDX_HDIM ** -0.5)
    qi = qi_ref[...]
    tpos = i * qb + lax.broadcasted_iota(jnp.int32, (kc, qb), 1)
    srow = lax.broadcasted_iota(jnp.int32, (kc, qb), 0)

    def score_chunk(c, carry):
        r0 = pl.multiple_of(c * kc, kc)
        kic = ki_ref[pl.ds(r0, kc), :]
        acc = jnp.zeros((kc, qb), F32)
        for h in range(IDX_HEADS):
            d = lax.dot_general(kic, qi[:, h * IDX_HDIM:(h + 1) * IDX_HDIM],
                                (((1,), (1,)), ((), ())), preferred_element_type=F32)
            acc = acc + jnp.maximum(d, 0.0) * wi[h:h + 1, :]
        acc = jnp.where(r0 + srow <= tpos, acc, -jnp.inf)
        key_ref[pl.ds(r0, kc), :] = _float_key(acc)
        return carry

    lax.fori_loop(0, nch, score_chunk, 0)

    def count(pred):
        def body(c, cnt):
            r0 = pl.multiple_of(c * kc, kc)
            hit = pred(key_ref[pl.ds(r0, kc), :], r0 + srow).astype(jnp.int32)
            return cnt + jnp.sum(hit.reshape(kc // 8, 8, qb), axis=0)
        cnt8 = lax.fori_loop(0, nch, body, jnp.zeros((8, qb), jnp.int32))
        return jnp.sum(cnt8, axis=0, keepdims=True)

    def bit_step(it, carry):
        tau, n_ge = carry
        cand = tau + (jnp.int32(1) << (31 - it))
        cnt = count(lambda key, pos: key >= cand)
        ok = cnt >= topk
        return jnp.where(ok, cand, tau), jnp.where(ok, cnt, n_ge)

    tau, n_ge = lax.fori_loop(0, 32, bit_step, (jnp.full((1, qb), jnp.iinfo(jnp.int32).min, jnp.int32),
                                                jnp.zeros((1, qb), jnp.int32)))
    n_ge = jnp.where(tau <= KEY_NEG_INF, 0, n_ge)
    tau = jnp.maximum(tau, KEY_NEG_INF + 1)

    tie_rounds = (jnp.max(n_ge) > topk).astype(jnp.int32)
    n_gt = lax.fori_loop(0, tie_rounds, lambda _, c: count(lambda key, pos: key > tau), jnp.zeros((1, qb), jnp.int32))
    quota = topk - n_gt

    def pos_step(it, pos_end):
        cand = pos_end + (jnp.int32(1) << (pos_bits - 1 - it))
        n_tie = count(lambda key, pos: (key == tau) & (pos < cand))
        return jnp.where(n_tie <= quota, cand, pos_end)

    pos_all = jnp.int32((1 << pos_bits) - 1)
    pos_end = lax.fori_loop(0, tie_rounds * pos_bits, pos_step,
                            jnp.full((1, qb), 1, jnp.int32) * (pos_all * (1 - tie_rounds)))

    def write_chunk(c, carry):
        r0 = pl.multiple_of(c * kc, kc)
        key = key_ref[pl.ds(r0, kc), :]
        tie_kept = (key == tau) & (r0 + srow < pos_end)
        bias = jnp.where(key > tau, 0.0, jnp.where(tie_kept, 0.0, NEG))
        bias_ref[pl.ds(r0, kc), :] = bias.astype(bias_ref.dtype)
        return carry

    lax.fori_loop(0, nch, write_chunk, 0)

    def fill_chunk(c, carry):
        r0 = pl.multiple_of(c * kc, kc)
        bias_ref[pl.ds(r0, kc), :] = jnp.full((kc, qb), NEG, bias_ref.dtype)
        return carry

    lax.fori_loop(nch, nkc_total, fill_chunk, 0)


def dsa_index(ki, qi, wiT, topk, qb=512, kc=256):
    S = ki.shape[0]
    qb = min(qb, S)
    kc = min(kc, qb)
    return pl.pallas_call(
        functools.partial(_dsa_index_body, qb=qb, kc=kc, topk=topk, nkc_total=S // kc),
        grid=(S // qb,),
        in_specs=[pl.BlockSpec((S, IDX_HDIM), lambda i: (0, 0)),
                  pl.BlockSpec((qb, IDX_HEADS * IDX_HDIM), lambda i: (i, 0)),
                  pl.BlockSpec((IDX_HEADS, qb), lambda i: (0, i))],
        out_specs=pl.BlockSpec((S, qb), lambda i: (0, i)),
        out_shape=jax.ShapeDtypeStruct((S, S), BF16),
        scratch_shapes=[pltpu.VMEM((S, qb), jnp.int32)],
        compiler_params=_cparams(("parallel",)),
    )(ki, qi, wiT)


def _dsa_attn_body(qi_ref, kj_ref, q_ref, k_ref, vT_ref, bias_ref, oT_ref, m_ref, l_ref, acc_ref, *, qb, kc):
    step = pl.program_id(0)
    i = qi_ref[step]
    j = kj_ref[step]
    scale = C_HDIM ** -0.5
    heads = [slice(h * C_HDIM, (h + 1) * C_HDIM) for h in range(C_HEADS)]

    @pl.when(j == 0)
    def _():
        m_ref[...] = jnp.full_like(m_ref, NEG)
        l_ref[...] = jnp.zeros_like(l_ref)
        acc_ref[...] = jnp.zeros_like(acc_ref)

    bias = bias_ref[...].astype(F32)
    sT = [lax.dot_general(k_ref[:, cs], q_ref[:, cs], (((1,), (1,)), ((), ())),
                          preferred_element_type=F32) * scale + bias for cs in heads]
    m_old = m_ref[...]
    m_new = jnp.maximum(m_old, jnp.concatenate([jnp.max(s, axis=0, keepdims=True) for s in sT], axis=0))
    alpha = jnp.exp(m_old - m_new)
    p = [jnp.exp(s - m_new[h:h + 1, :]) for h, s in enumerate(sT)]
    l_ref[...] = alpha * l_ref[...] + jnp.concatenate([jnp.sum(x, axis=0, keepdims=True) for x in p], axis=0)
    m_ref[...] = m_new
    for h, cs in enumerate(heads):
        acc_ref[cs, :] = alpha[h:h + 1, :] * acc_ref[cs, :] + jnp.dot(vT_ref[cs, :], p[h].astype(BF16),
                                                                       preferred_element_type=F32)

    @pl.when(j == ((i + 1) * qb - 1) // kc)
    def _():
        l = l_ref[...]
        for h, cs in enumerate(heads):
            oT_ref[cs, :] = (acc_ref[cs, :] / l[h:h + 1, :]).astype(oT_ref.dtype)


def dsa_attention(qkv, vT, bias, qb=256, kc=512):
    S = qkv.shape[0]
    qb = min(qb, S)
    kc = min(kc, S)
    pairs = [(i, j) for i in range(S // qb) for j in range(((i + 1) * qb - 1) // kc + 1)]
    qi = jnp.asarray([p_[0] for p_ in pairs], jnp.int32)
    kj = jnp.asarray([p_[1] for p_ in pairs], jnp.int32)
    grid_spec = pltpu.PrefetchScalarGridSpec(
        num_scalar_prefetch=2,
        grid=(len(pairs),),
        in_specs=[pl.BlockSpec((qb, C_WIDTH), lambda s, qi, kj: (qi[s], 0)),
                  pl.BlockSpec((kc, C_WIDTH), lambda s, qi, kj: (kj[s], 1)),
                  pl.BlockSpec((C_WIDTH, kc), lambda s, qi, kj: (0, kj[s])),
                  pl.BlockSpec((kc, qb), lambda s, qi, kj: (kj[s], qi[s]))],
        out_specs=pl.BlockSpec((C_WIDTH, qb), lambda s, qi, kj: (0, qi[s])),
        scratch_shapes=[pltpu.VMEM((C_HEADS, qb), F32), pltpu.VMEM((C_HEADS, qb), F32),
                        pltpu.VMEM((C_WIDTH, qb), F32)])
    return pl.pallas_call(
        functools.partial(_dsa_attn_body, qb=qb, kc=kc),
        grid_spec=grid_spec,
        out_shape=jax.ShapeDtypeStruct((C_WIDTH, S), BF16),
        compiler_params=_cparams(("arbitrary",)),
    )(qi, kj, qkv, qkv, vT, bias)


def _sigmoid(x):
    return 1.0 / (1.0 + jnp.exp(-x))


def _shift_rows(cur, prev_ref, first_tile):
    prev_row = jnp.where(first_tile, 0.0, prev_ref[7:8, :].astype(F32))
    rolled = pltpu.roll(cur, 1, axis=0)
    row = lax.broadcasted_iota(jnp.int32, cur.shape, 0)
    return jnp.where(row == 0, prev_row, rolled)


def _rwkv_lora_body(*refs, has_v):
    if has_v:
        (h_ref, hp_ref, mu_ref, w1_ref, a1_ref, g1_ref, v1_ref, ow_ref, oa_ref, og_ref, ov_ref) = refs
    else:
        (h_ref, hp_ref, mu_ref, w1_ref, a1_ref, g1_ref, ow_ref, oa_ref, og_ref) = refs
    i = pl.program_id(0)
    h = h_ref[...].astype(F32)
    dh = _shift_rows(h, hp_ref, i == 0) - h

    def lora(row, w_ref):
        xm = (h + dh * mu_ref[row:row + 1, :]).astype(BF16)
        return jnp.dot(xm, w_ref[...], preferred_element_type=F32)

    ow_ref[...] = jnp.tanh(lora(0, w1_ref))
    oa_ref[...] = lora(1, a1_ref)
    og_ref[...] = _sigmoid(lora(2, g1_ref))
    if has_v:
        ov_ref[...] = lora(3, v1_ref)


def rwkv_lora(h, mu, w1, a1, g1, v1, tm=256):
    S, D = h.shape
    tm = min(tm, S)
    has_v = v1 is not None
    ws = [w1, a1, g1] + ([v1] if has_v else [])
    full = lambda a: pl.BlockSpec(a.shape, lambda i: (0, 0))
    return pl.pallas_call(
        functools.partial(_rwkv_lora_body, has_v=has_v),
        grid=(S // tm,),
        in_specs=[pl.BlockSpec((tm, D), lambda i: (i, 0)),
                  pl.BlockSpec((8, D), lambda i: (jnp.maximum(i * (tm // 8) - 1, 0), 0)),
                  full(mu)] + [full(w) for w in ws],
        out_specs=[pl.BlockSpec((tm, w.shape[1]), lambda i: (i, 0)) for w in ws],
        out_shape=[jax.ShapeDtypeStruct((S, w.shape[1]), F32) for w in ws],
        compiler_params=_cparams(("parallel",)),
    )(h, h, mu, *ws)


(V_MU_R, V_MU_K, V_MU_V, V_W0, V_A0, V_KK, V_KA, V_RK, V_V0) = range(9)
N_VEC_ROWS = 16


def _wkv_pre_body(*refs, has_v, tt):
    if has_v:
        (r_ref, k_ref, v_ref, rp_ref, kp_ref, vp_ref, hw_ref, ha_ref, hg_ref, w2_ref, a2_ref, g2_ref,
         vec_ref, hv_ref, v2_ref, vf_ref,
         m_out, g_out, q_out, z_out, vout_ref, gate_ref, bonus_ref) = refs
    else:
        (r_ref, k_ref, v_ref, rp_ref, kp_ref, vp_ref, hw_ref, ha_ref, hg_ref, w2_ref, a2_ref, g2_ref,
         vec_ref,
         m_out, g_out, q_out, z_out, vout_ref, gate_ref, bonus_ref) = refs
    i = pl.program_id(0)
    first = i == 0
    C = WKV_CHUNK
    N = B_HDIM
    vec = lambda row: vec_ref[row:row + 1, :]

    r = r_ref[...]
    k = k_ref[...]
    v = v_ref[...]
    r = r + (_shift_rows(r, rp_ref, first) - r) * vec(V_MU_R)
    k = k + (_shift_rows(k, kp_ref, first) - k) * vec(V_MU_K)
    v = v + (_shift_rows(v, vp_ref, first) - v) * vec(V_MU_V)

    wl = vec(V_W0) + jnp.dot(hw_ref[...], w2_ref[...], preferred_element_type=F32, precision=HI)
    z = -wl
    softplus = jnp.maximum(z, 0.0) + jnp.log(1.0 + jnp.exp(-jnp.abs(z)))
    logw = -jnp.exp(-softplus - 0.5)
    bdot = lambda a_ref, b_ref: jnp.dot(a_ref[...].astype(BF16), b_ref[...].astype(BF16), preferred_element_type=F32)
    a_sig = _sigmoid(vec(V_A0) + bdot(ha_ref, a2_ref))
    gate = bdot(hg_ref, g2_ref)
    if has_v:
        mix = _sigmoid(vec(V_V0) + bdot(hv_ref, v2_ref))
        v = v + (vf_ref[...] - v) * mix
    vout_ref[...] = v
    gate_ref[...] = gate

    kk = k * vec(V_KK)
    k_new = k * (1.0 + (a_sig - 1.0) * vec(V_KA))
    rk = r * k_new * vec(V_RK)

    rowc = lax.broadcasted_iota(jnp.int32, (C, C), 0)
    colc = lax.broadcasted_iota(jnp.int32, (C, C), 1)
    tril_incl = (rowc >= colc).astype(F32)
    prow = lax.broadcasted_iota(jnp.int32, (2 * C, 2 * C), 0)
    pcol = lax.broadcasted_iota(jnp.int32, (2 * C, 2 * C), 1)
    same_head = (prow >= C) == (pcol >= C)
    dstep = jnp.where(prow >= C, prow - C, prow) - jnp.where(pcol >= C, pcol - C, pcol)
    strict_bd = jnp.where(same_head, dstep, -1) > 0
    incl_bd = jnp.where(same_head, dstep, -1) >= 0
    eye_p = (prow == pcol).astype(F32)
    head_ones = jnp.where(same_head, 1.0, 0.0).astype(BF16)

    def head_sum(x):
        hi = x.astype(BF16)
        lo = (x - hi.astype(F32)).astype(BF16)
        return (jnp.dot(hi, head_ones, preferred_element_type=F32)
                + jnp.dot(lo, head_ones, preferred_element_type=F32))

    kkn = kk / jnp.maximum(jnp.sqrt(head_sum(kk * kk)), 1e-12)
    bonus_ref[...] = head_sum(rk) * v
    a_scan = -kkn
    b_scan = kkn * a_sig

    head0 = lax.broadcasted_iota(jnp.int32, (C, LANES), 1) < N

    def stack2(x):
        return jnp.concatenate([jnp.where(head0, x, 0.0), jnp.where(head0, 0.0, x)], axis=0)

    ts_of = lambda c: slice(c * C, (c + 1) * C)
    st = []
    for c in range(tt // C):
        ts = ts_of(c)
        lw = logw[ts]
        cum = jnp.dot(tril_incl, lw, preferred_element_type=F32, precision=HI)
        cum_last = cum[C - 1:C, :]
        e_neg = jnp.exp(-cum)
        e_end = jnp.exp(cum_last - cum)
        st.append(dict(At=stack2(a_scan[ts] * jnp.exp(cum - lw)), Rt=stack2(r[ts] * jnp.exp(cum)),
                       Bt=b_scan[ts] * e_neg, Kt=k_new[ts] * e_neg,
                       Bg=stack2(b_scan[ts] * e_end), Kg=stack2(k_new[ts] * e_end),
                       V=stack2(v[ts]).astype(BF16), gam=jnp.exp(cum_last)))
    for u in st:
        left = jnp.concatenate([u["At"], u["Rt"]], axis=0).astype(BF16)
        right = jnp.concatenate([u["Bt"], u["Bt"], u["Kt"], u["Kt"]], axis=0).astype(BF16)
        AA = lax.dot_general(left, right, (((1,), (1,)), ((), ())), preferred_element_type=F32)
        u["A_ab"] = jnp.where(strict_bd, AA[:2 * C, :2 * C], 0.0)
        u["A_ak"] = jnp.where(strict_bd, AA[:2 * C, 2 * C:], 0.0)
        u["A_r"] = jnp.concatenate([jnp.where(incl_bd, AA[2 * C:, :2 * C], 0.0),
                                    jnp.where(incl_bd, AA[2 * C:, 2 * C:], 0.0)], axis=-1).astype(BF16)
    for u in st:
        akv = jnp.dot(u["A_ak"].astype(BF16), u["V"], preferred_element_type=F32)
        u["XA"] = jnp.concatenate([u["At"] + pltpu.roll(akv, N, axis=1), u["A_ab"]], axis=-1)
    for step in range(6):
        for u in st:
            XA = u["XA"]
            Ap = XA[:, LANES:].astype(BF16)
            if step < 5:
                prod = jnp.dot(Ap, XA.astype(BF16), preferred_element_type=F32)
                u["XA"] = jnp.concatenate([XA[:, :LANES] + prod[:, :LANES], prod[:, LANES:]], axis=-1)
            else:
                u["X"] = XA[:, :LANES] + jnp.dot(Ap, XA[:, :LANES].astype(BF16), preferred_element_type=F32)
    own = (lax.broadcasted_iota(jnp.int32, (2 * C, LANES), 0) >= C) == (lax.broadcasted_iota(jnp.int32, (2 * C, LANES), 1) >= N)
    for u in st:
        p1 = jnp.where(own, u["X"], 0.0).astype(BF16)
        p2 = pltpu.roll(jnp.where(own, 0.0, u["X"]), N, axis=1).astype(BF16)
        upper = jnp.concatenate([p1, p2], axis=-1)
        lower = jnp.concatenate([jnp.zeros((2 * C, LANES), BF16), u["V"]], axis=-1)
        u["W2"] = jnp.concatenate([upper, lower], axis=0)
    for u in st:
        bk = jnp.concatenate([u["Bg"], u["Kg"]], axis=0).astype(BF16)
        u["MG"] = lax.dot_general(bk, u["W2"], (((0,), (0,)), ((), ())), preferred_element_type=F32)
    for u in st:
        u["QZ"] = jnp.dot(u["A_r"], u["W2"], preferred_element_type=F32)
    for c, u in enumerate(st):
        rows = slice(c * LANES, (c + 1) * LANES)
        m_out[rows, :] = (eye_p * u["gam"] + u["MG"][:, :LANES]).astype(m_out.dtype)
        g_out[rows, :] = u["MG"][:, LANES:].astype(g_out.dtype)
        qs = u["Rt"] + u["QZ"][:, :LANES]
        zs = u["QZ"][:, LANES:]
        q_out[ts_of(c), :] = (qs[:C] + qs[C:]).astype(q_out.dtype)
        z_out[ts_of(c), :] = zs[:C] + zs[C:]


def wkv_pre(proj_b, hids, w2s, vecs, v_first, tt=1024):
    S = proj_b.shape[0]
    tt = min(tt, S)
    has_v = v_first is not None
    nb = B_WIDTH // LANES
    nstate_rows = (tt // WKV_CHUNK) * LANES
    tok = lambda off: pl.BlockSpec((tt, LANES), lambda i, p: (i, off + p))
    prev = lambda off: pl.BlockSpec((8, LANES), lambda i, p: (jnp.maximum(i * (tt // 8) - 1, 0), off + p))
    hid = lambda a: pl.BlockSpec((tt, a.shape[1]), lambda i, p: (i, 0))
    wcol = lambda a: pl.BlockSpec((a.shape[0], LANES), lambda i, p: (0, p))
    in_specs = [tok(0), tok(nb), tok(2 * nb), prev(0), prev(nb), prev(2 * nb),
                hid(hids[0]), hid(hids[1]), hid(hids[2]), wcol(w2s[0]), wcol(w2s[1]), wcol(w2s[2]),
                pl.BlockSpec((N_VEC_ROWS, LANES), lambda i, p: (0, p))]
    args = [proj_b] * 6 + list(hids[:3]) + list(w2s[:3]) + [vecs]
    if has_v:
        in_specs += [hid(hids[3]), wcol(w2s[3]), tok(0)]
        args += [hids[3], w2s[3], v_first]
    state_shape = jax.ShapeDtypeStruct(((S // WKV_CHUNK) * LANES, B_WIDTH), BF16)
    out_tok = pl.BlockSpec((tt, LANES), lambda i, p: (i, p))
    out_st = pl.BlockSpec((nstate_rows, LANES), lambda i, p: (i, p))
    return pl.pallas_call(
        functools.partial(_wkv_pre_body, has_v=has_v, tt=tt),
        grid=(S // tt, nb),
        in_specs=in_specs,
        out_specs=[out_st, out_st, out_tok, out_tok, out_tok, out_tok, out_tok],
        out_shape=[state_shape, state_shape, jax.ShapeDtypeStruct((S, B_WIDTH), BF16)]
                  + [jax.ShapeDtypeStruct((S, B_WIDTH), F32)] * 4,
        compiler_params=_cparams(("parallel", "parallel")),
    )(*args)


def _wkv_scan_body(m_ref, g_ref, q_ref, z_ref, gate_ref, bonus_ref, ln_ref, o_ref, h_ref, *, tt):
    i = pl.program_id(0)
    C = WKV_CHUNK
    N = B_HDIM
    npair = B_WIDTH // LANES

    @pl.when(i == 0)
    def _():
        h_ref[...] = jnp.zeros_like(h_ref)

    row = lax.broadcasted_iota(jnp.int32, (LANES, LANES), 0)
    col = lax.broadcasted_iota(jnp.int32, (LANES, LANES), 1)
    head_avg = jnp.where((row // N) == (col // N), 1.0 / N, 0.0).astype(BF16)

    def head_mean(x):
        hi = x.astype(BF16)
        lo = (x - hi.astype(F32)).astype(BF16)
        return (jnp.dot(hi, head_avg, preferred_element_type=F32)
                + jnp.dot(lo, head_avg, preferred_element_type=F32))

    for c in range(tt // C):
        ts = slice(c * C, (c + 1) * C)
        ss = slice(c * LANES, (c + 1) * LANES)
        ys = []
        for p in range(npair):
            ps = slice(p * LANES, (p + 1) * LANES)
            Hb = h_ref[p].astype(BF16)
            ys.append(jnp.dot(q_ref[ts, ps], Hb, preferred_element_type=F32) + z_ref[ts, ps])
            h_ref[p] = jnp.dot(m_ref[ss, ps], Hb, preferred_element_type=F32) + g_ref[ss, ps].astype(F32)
        for p in range(npair):
            ps = slice(p * LANES, (p + 1) * LANES)
            yc = ys[p] - head_mean(ys[p])
            var = head_mean(yc * yc)
            yn = yc * lax.rsqrt(var + B_GN_EPS) * ln_ref[0:1, ps] + ln_ref[1:2, ps]
            o_ref[ts, ps] = ((yn + bonus_ref[ts, ps]) * gate_ref[ts, ps]).astype(o_ref.dtype)


def wkv_scan(m, g, q, z, gate, bonus, ln, tt=256):
    S = q.shape[0]
    tt = min(tt, S)
    npair = B_WIDTH // LANES
    nstate_rows = (tt // WKV_CHUNK) * LANES
    tok = pl.BlockSpec((tt, B_WIDTH), lambda i: (i, 0))
    st = pl.BlockSpec((nstate_rows, B_WIDTH), lambda i: (i, 0))
    return pl.pallas_call(
        functools.partial(_wkv_scan_body, tt=tt),
        grid=(S // tt,),
        in_specs=[st, st, tok, tok, tok, tok, pl.BlockSpec((8, B_WIDTH), lambda i: (0, 0))],
        out_specs=tok,
        out_shape=jax.ShapeDtypeStruct((S, B_WIDTH), BF16),
        scratch_shapes=[pltpu.VMEM((npair, LANES, LANES), F32)],
        compiler_params=_cparams(("arbitrary",)),
    )(m, g, q, z, gate, bonus, ln)


def rwkv7_mixer(h, proj_b, p, v_first):
    has_v = v_first is not None
    mu = p["mu_wag"] if not has_v else jnp.concatenate([p["mu_wag"], p["v_mu"][None]], axis=0)
    hids = rwkv_lora(h, mu, p["w1"].astype(BF16), p["a1"].astype(BF16), p["g1"].astype(BF16),
                     p["v1"].astype(BF16) if has_v else None)
    rows = [p["mu_rkv"][0], p["mu_rkv"][1], p["mu_rkv"][2], p["w0"], p["a0"], p["k_k"], p["k_a"],
            p["r_k"].reshape(-1), p["v0"] if has_v else jnp.zeros((B_WIDTH,), F32)]
    vecs = jnp.concatenate([jnp.stack(rows), jnp.zeros((N_VEC_ROWS - len(rows), B_WIDTH), F32)], axis=0)
    w2s = [p["w2"], p["a2"], p["g2"]] + ([p["v2"]] if has_v else [])
    m, g, q, z, v_out, gate, bonus = wkv_pre(proj_b, hids, w2s, vecs, v_first)
    ln = jnp.concatenate([p["ln_w"][None], p["ln_b"][None], jnp.zeros((6, B_WIDTH), F32)], axis=0)
    return wkv_scan(m, g, q, z, gate, bonus, ln), v_out


def kernel(x, norm_mix_g, w_in, lam_q1, lam_k1, lam_q2, lam_k2, diff_subln_g, rw_mu_rkv, rw_mu_wag, rw_w0, rw_w1, rw_w2, rw_a0, rw_a1, rw_a2, rw_g1, rw_g2, rw_k_k, rw_k_a, rw_r_k, rw_ln_w, rw_ln_b, rw_v_mu, rw_v0, rw_v1, rw_v2, w_out, norm_ffn_g, w_up, w_down, norm_final_g):
    Bsz, S, D = x.shape
    depth = w_in.shape[0]
    topk = min(TOPK_MAX, S // 4)
    w_in_t = jnp.swapaxes(w_in, 1, 2)
    w_down_bf = w_down.astype(BF16)
    nq = IDX_HEADS * IDX_HDIM
    n_kw = IDX_HDIM + IDX_HEADS
    w_idx_kw = w_in_t[:, w_in.shape[2] - n_kw:, :]
    outs = []
    for b in range(Bsz):
        xb = x[b]
        v_first = None
        for l in range(depth):
            h = rmsnorm(xb, norm_mix_g[l], BF16)
            oA, oB, oC = 3 * A_WIDTH, 3 * A_WIDTH + 3 * B_WIDTH, 3 * A_WIDTH + 3 * B_WIDTH + 3 * C_WIDTH
            proj_a = matmul_ws(h, w_in_t, l, 0, oA, BF16, tn=512, w_transposed=True)
            proj_b = matmul_ws(h, w_in_t, l, oA, oB - oA, F32, tn=512, w_transposed=True)
            proj_c = matmul_ws(h, w_in_t, l, oB, oC - oB, BF16, tn=512, w_transposed=True)
            qi = matmul_ws(h, w_in_t, l, oC, nq, BF16, tn=512, w_transposed=True)
            proj_kw = matmul_ws(h, w_idx_kw, l, 0, n_kw, F32, tn=n_kw, w_transposed=True)

            lam_init = 0.8 - 0.6 * math.exp(-0.3 * l)
            lam_vecs = jnp.stack([lam_q1[l], lam_k1[l], lam_q2[l], lam_k2[l]])
            o_a = diff_attention(proj_a, lam_vecs, diff_subln_g[l], lam_init)

            p = dict(mu_rkv=rw_mu_rkv[l], mu_wag=rw_mu_wag[l], w0=rw_w0[l], w1=rw_w1[l], w2=rw_w2[l],
                     a0=rw_a0[l], a1=rw_a1[l], a2=rw_a2[l], g1=rw_g1[l], g2=rw_g2[l], k_k=rw_k_k[l],
                     k_a=rw_k_a[l], r_k=rw_r_k[l], ln_w=rw_ln_w[l], ln_b=rw_ln_b[l])
            if l > 0:
                p.update(v_mu=rw_v_mu[l - 1], v0=rw_v0[l - 1], v1=rw_v1[l - 1], v2=rw_v2[l - 1])
            o_b, v_out = rwkv7_mixer(h, proj_b, p, v_first if l > 0 else None)
            if l == 0:
                v_first = v_out

            ki = proj_kw[:, :IDX_HDIM].astype(BF16)
            wiT = proj_kw[:, IDX_HDIM:].T
            bias = dsa_index(ki, qi, wiT, topk)
            vT = proj_c[:, 2 * C_WIDTH:].T
            o_c = dsa_attention(proj_c, vT, bias).T

            mixed = jnp.concatenate([o_a, o_b, o_c], axis=-1)
            xb = matmul_ws(mixed, w_out, l, 0, D, F32, tn=512, epilogue="residual", residual=xb)
            h2 = rmsnorm(xb, norm_ffn_g[l], BF16)
            up = matmul_ws(h2, w_up, l, 0, w_up.shape[2], BF16, tn=1024, tm=512, epilogue="relu2")
            xb = matmul(up, w_down_bf, l, F32, epilogue="residual", residual=xb)
        outs.append(rmsnorm(xb, norm_final_g, F32))
    return jnp.stack(outs)
```

```python
import functools
import math

import jax
import jax.numpy as jnp
from jax import lax
from jax.experimental import pallas as pl
from jax.experimental.pallas import tpu as pltpu

F32 = jnp.float32
BF16 = jnp.bfloat16

A_HEADS, A_HDIM = 8, 64
A_WIDTH = A_HEADS * 2 * A_HDIM
B_HDIM, B_WIDTH = 64, 2048
B_HEADS = B_WIDTH // B_HDIM
B_GN_EPS = 64e-5
C_HEADS, C_HDIM = 8, 128
C_WIDTH = C_HEADS * C_HDIM
IDX_HEADS, IDX_HDIM = 16, 64
TOPK_MAX = 256
EPS = 1e-6

LANES = 128
VMEM_LIMIT = 56 * 1024 * 1024
NEG = -1e30
WKV_CHUNK = 64
SOFTMAX_ROWS = 64


def _cparams(sem):
    return pltpu.CompilerParams(dimension_semantics=sem, vmem_limit_bytes=VMEM_LIMIT)


def _rmsnorm_body(x_ref, g_ref, o_ref):
    x = x_ref[...]
    ms = jnp.mean(x * x, axis=-1, keepdims=True)
    o_ref[...] = (x * lax.rsqrt(ms + EPS) * g_ref[...]).astype(o_ref.dtype)


def rmsnorm(x, g, out_dtype, tm=256):
    S, D = x.shape
    tm = min(tm, S)
    return pl.pallas_call(
        _rmsnorm_body,
        grid=(S // tm,),
        in_specs=[pl.BlockSpec((tm, D), lambda i: (i, 0)),
                  pl.BlockSpec((1, D), lambda i: (0, 0))],
        out_specs=pl.BlockSpec((tm, D), lambda i: (i, 0)),
        out_shape=jax.ShapeDtypeStruct((S, D), out_dtype),
        compiler_params=_cparams(("parallel",)),
    )(x, g.reshape(1, D))


def _mm_body(a_ref, b_ref, *rest, nk, epilogue):
    if epilogue == "residual":
        res_ref, o_ref, acc_ref = rest
    else:
        o_ref, acc_ref = rest
    k = pl.program_id(2)

    @pl.when(k == 0)
    def _():
        acc_ref[...] = jnp.zeros_like(acc_ref)

    acc_ref[...] += jnp.dot(a_ref[...], b_ref[...], preferred_element_type=F32)

    @pl.when(k == nk - 1)
    def _():
        acc = acc_ref[...]
        if epilogue == "relu2":
            r = jnp.maximum(acc, 0.0)
            acc = r * r
        elif epilogue == "residual":
            acc = acc + res_ref[...]
        o_ref[...] = acc.astype(o_ref.dtype)


def _pick(n, pref):
    for t in pref:
        if n % t == 0:
            return t
    return n


def matmul(a, b3, layer, out_dtype, epilogue="none", residual=None):
    M, K = a.shape
    _, _, N = b3.shape
    tm = _pick(M, (1024, 512, 256))
    tn = _pick(N, (1024, 768, 512, 384, 256, 128))
    tk = _pick(K, (2048, 1024, 512))
    nk = K // tk
    in_specs = [pl.BlockSpec((tm, tk), lambda i, j, k: (i, k)),
                pl.BlockSpec((None, tk, tn), lambda i, j, k: (layer, k, j))]
    args = [a, b3]
    if epilogue == "residual":
        in_specs.append(pl.BlockSpec((tm, tn), lambda i, j, k: (i, j)))
        args.append(residual)
    return pl.pallas_call(
        functools.partial(_mm_body, nk=nk, epilogue=epilogue),
        grid=(M // tm, N // tn, nk),
        in_specs=in_specs,
        out_specs=pl.BlockSpec((tm, tn), lambda i, j, k: (i, j)),
        out_shape=jax.ShapeDtypeStruct((M, N), out_dtype),
        scratch_shapes=[pltpu.VMEM((tm, tn), F32)],
        compiler_params=_cparams(("parallel", "parallel", "arbitrary")),
    )(*args)


def _mm_ws_body(x_ref, w_ref, *rest, epilogue, w_transposed):
    if epilogue == "residual":
        res_ref, o_ref, wb_ref = rest
    else:
        o_ref, wb_ref = rest

    @pl.when(pl.program_id(1) == 0)
    def _():
        wb_ref[...] = w_ref[...].astype(BF16)

    contract = (((1,), (1,)), ((), ())) if w_transposed else (((1,), (0,)), ((), ()))
    acc = lax.dot_general(x_ref[...], wb_ref[...], contract, preferred_element_type=F32)
    if epilogue == "relu2":
        r = jnp.maximum(acc, 0.0)
        acc = r * r
    elif epilogue == "residual":
        acc = acc + res_ref[...]
    o_ref[...] = acc.astype(o_ref.dtype)


def matmul_ws(x, w3, layer, col0, ncols, out_dtype, tn, epilogue="none", residual=None, tm=1024, w_transposed=False):
    M, K = x.shape
    tm = min(tm, M)
    nj = ncols // tn
    jb = col0 // tn
    assert col0 % tn == 0 and ncols % tn == 0 and M % tm == 0
    if w_transposed:
        w_spec = pl.BlockSpec((None, tn, K), lambda j, i: (layer, jb + j, 0))
        wb_shape = (tn, K)
    else:
        w_spec = pl.BlockSpec((None, K, tn), lambda j, i: (layer, 0, jb + j))
        wb_shape = (K, tn)
    in_specs = [pl.BlockSpec((tm, K), lambda j, i: (i, 0)), w_spec]
    args = [x, w3]
    if epilogue == "residual":
        in_specs.append(pl.BlockSpec((tm, tn), lambda j, i: (i, j)))
        args.append(residual)
    return pl.pallas_call(
        functools.partial(_mm_ws_body, epilogue=epilogue, w_transposed=w_transposed),
        grid=(nj, M // tm),
        in_specs=in_specs,
        out_specs=pl.BlockSpec((tm, tn), lambda j, i: (i, j)),
        out_shape=jax.ShapeDtypeStruct((M, nj * tn), out_dtype),
        scratch_shapes=[pltpu.VMEM(wb_shape, BF16)],
        compiler_params=_cparams(("parallel", "arbitrary")),
    )(*args)


def _diffattn_body(q_ref, k_ref, v_ref, lam_ref, g_ref, o_ref, m_ref, acc_ref,
                   sa_ref, sb_ref, pa_ref, pb_ref, aa_ref, ab_ref, *, tq, lam_init):
    tk = tq
    i = pl.program_id(1)
    hw = 2 * A_HDIM
    q = q_ref[...] * (A_HDIM ** -0.5)
    lane = lax.broadcasted_iota(jnp.int32, (tq, hw), 1)
    zero = jnp.zeros_like(q)
    qz = jnp.concatenate([jnp.where(lane < A_HDIM, q, zero), jnp.where(lane >= A_HDIM, q, zero)], axis=0)
    m_ref[...] = jnp.full_like(m_ref, NEG)
    acc_ref[...] = jnp.zeros_like(acc_ref)
    ones = jnp.ones((tk, hw), BF16)

    def scores(j):
        k = k_ref[pl.ds(pl.multiple_of(j * tk, tk), tk), :]
        return lax.dot_general(qz, k, (((1,), (1,)), ((), ())), preferred_element_type=F32)

    key_minus_row = (lax.broadcasted_iota(jnp.int32, (SOFTMAX_ROWS, tk), 1)
                     - lax.broadcasted_iota(jnp.int32, (SOFTMAX_ROWS, tk), 0))
    tile_rows = lambda j: pl.ds(pl.multiple_of(j * tk, tk), tk)

    def scores(j, s_out):
        s_out[...] = lax.dot_general(qz, k_ref[tile_rows(j), :], (((1,), (1,)), ((), ())),
                                     preferred_element_type=F32)

    def softmax(j, s_in, p_out, alpha_out, masked):
        for r0 in range(0, 2 * tq, SOFTMAX_ROWS):
            rs = slice(r0, r0 + SOFTMAX_ROWS)
            m_old = m_ref[rs, :]
            if masked:
                causal = key_minus_row <= (i * tq + r0 % tq) - j * tk
                row_max = jnp.max(jnp.where(causal, s_in[rs, :], NEG), axis=-1, keepdims=True)
            else:
                row_max = jnp.max(s_in[rs, :], axis=-1, keepdims=True)
            m_new = jnp.maximum(m_old, row_max)
            m_ref[rs, :] = m_new
            e = jnp.exp(s_in[rs, :] - jnp.concatenate([m_new] * (tk // hw), axis=-1))
            p_out[rs, :] = (jnp.where(causal, e, 0.0) if masked else e).astype(BF16)
            alpha_out[rs, :] = jnp.exp(m_old - m_new)

    def accumulate(j, p_in, alpha_in):
        v_ext = jnp.concatenate([v_ref[tile_rows(j), :], ones], axis=-1)
        alpha = alpha_in[...]
        acc_ref[...] = (jnp.concatenate([alpha, alpha], axis=-1) * acc_ref[...]
                        + jnp.dot(p_in[...], v_ext, preferred_element_type=F32))

    scores(0, sa_ref)
    pb_ref[...] = jnp.zeros_like(pb_ref)
    ab_ref[...] = jnp.ones_like(ab_ref)

    def pair(t, masked):
        j = 2 * t
        accumulate(jnp.maximum(j - 1, 0), pb_ref, ab_ref)
        softmax(j, sa_ref, pa_ref, aa_ref, masked)
        scores(j + 1, sb_ref)
        accumulate(j, pa_ref, aa_ref)
        softmax(j + 1, sb_ref, pb_ref, ab_ref, masked)
        if not masked:
            scores(j + 2, sa_ref)

    nfree = i // 2
    lax.fori_loop(0, nfree, lambda t, c: (pair(t, False), c)[1], 0)

    @pl.when(i % 2 == 0)
    def _():
        accumulate(jnp.maximum(i - 1, 0), pb_ref, ab_ref)
        softmax(i, sa_ref, pa_ref, aa_ref, True)
        accumulate(i, pa_ref, aa_ref)

    @pl.when(i % 2 == 1)
    def _():
        pair(nfree, True)
        accumulate(i, pb_ref, ab_ref)

    lv = lam_ref[...]
    lam = (jnp.exp(jnp.sum(lv[0:1] * lv[1:2], axis=-1, keepdims=True))
           - jnp.exp(jnp.sum(lv[2:3] * lv[3:4], axis=-1, keepdims=True)) + lam_init)
    acc = acc_ref[...]
    on = acc[:, :hw] / acc[:, hw:]
    o = on[:tq] - lam * on[tq:]
    ms = jnp.mean(o * o, axis=-1, keepdims=True)
    o_ref[...] = (o * lax.rsqrt(ms + EPS) * g_ref[...] * (1.0 - lam_init)).astype(o_ref.dtype)


def diff_attention(qkv, lam_vecs, subln_g, lam_init, tq=512):
    S = qkv.shape[0]
    tq = min(tq, S)
    hw = 2 * A_HDIM
    return pl.pallas_call(
        functools.partial(_diffattn_body, tq=tq, lam_init=lam_init),
        grid=(A_HEADS, S // tq),
        in_specs=[pl.BlockSpec((tq, hw), lambda h, i: (i, h)),
                  pl.BlockSpec((S, hw), lambda h, i: (0, A_HEADS + h)),
                  pl.BlockSpec((S, hw), lambda h, i: (0, 2 * A_HEADS + h)),
                  pl.BlockSpec((4, A_HDIM), lambda h, i: (0, 0)),
                  pl.BlockSpec((1, hw), lambda h, i: (0, 0))],
        out_specs=pl.BlockSpec((tq, hw), lambda h, i: (i, h)),
        out_shape=jax.ShapeDtypeStruct((S, A_WIDTH), BF16),
        scratch_shapes=[pltpu.VMEM((2 * tq, hw), F32), pltpu.VMEM((2 * tq, 2 * hw), F32),
                        pltpu.VMEM((2 * tq, tq), F32), pltpu.VMEM((2 * tq, tq), F32),
                        pltpu.VMEM((2 * tq, tq), BF16), pltpu.VMEM((2 * tq, tq), BF16),
                        pltpu.VMEM((2 * tq, hw), F32), pltpu.VMEM((2 * tq, hw), F32)],
        compiler_params=_cparams(("parallel", "parallel")),
    )(qkv, qkv, qkv, lam_vecs, subln_g.reshape(1, hw))


KEY_NEG_INF = -2139095041


def _float_key(s):
    b = pltpu.bitcast(s, jnp.int32)
    return b ^ ((b >> 31) & jnp.int32(0x7FFFFFFF))


def _dsa_index_body(ki_ref, qi_ref, wiT_ref, bias_ref, key_ref, *, qb, kc, topk, nkc_total):
    pos_bits = (nkc_total * kc).bit_length()
    i = pl.program_id(0)
    nch = ((i + 1) * qb) // kc
    wi = wiT_ref[...] * (IDX_HEADS ** -0.5 * IDX_HDIM ** -0.5)
    qi = qi_ref[...]
    tpos = i * qb + lax.broadcasted_iota(jnp.int32, (kc, qb), 1)
    srow = lax.broadcasted_iota(jnp.int32, (kc, qb), 0)

    def score_chunk(c, carry):
        r0 = pl.multiple_of(c * kc, kc)
        kic = ki_ref[pl.ds(r0, kc), :]
        acc = jnp.zeros((kc, qb), F32)
        for h in range(IDX_HEADS):
            d = lax.dot_general(kic, qi[:, h * IDX_HDIM:(h + 1) * IDX_HDIM],
                                (((1,), (1,)), ((), ())), preferred_element_type=F32)
            acc = acc + jnp.maximum(d, 0.0) * wi[h:h + 1, :]
        acc = jnp.where(r0 + srow <= tpos, acc, -jnp.inf)
        key_ref[pl.ds(r0, kc), :] = _float_key(acc)
        return carry

    lax.fori_loop(0, nch, score_chunk, 0)

    def count(pred):
        def body(c, cnt):
            r0 = pl.multiple_of(c * kc, kc)
            hit = pred(key_ref[pl.ds(r0, kc), :], r0 + srow).astype(jnp.int32)
            return cnt + jnp.sum(hit.reshape(kc // 8, 8, qb), axis=0)
        cnt8 = lax.fori_loop(0, nch, body, jnp.zeros((8, qb), jnp.int32))
        return jnp.sum(cnt8, axis=0, keepdims=True)

    def bit_step(it, carry):
        tau, n_ge = carry
        cand = tau + (jnp.int32(1) << (31 - it))
        cnt = count(lambda key, pos: key >= cand)
        ok = cnt >= topk
        return jnp.where(ok, cand, tau), jnp.where(ok, cnt, n_ge)

    tau, n_ge = lax.fori_loop(0, 32, bit_step, (jnp.full((1, qb), jnp.iinfo(jnp.int32).min, jnp.int32),
                                                jnp.zeros((1, qb), jnp.int32)))
    n_ge = jnp.where(tau <= KEY_NEG_INF, 0, n_ge)
    tau = jnp.maximum(tau, KEY_NEG_INF + 1)

    tie_rounds = (jnp.max(n_ge) > topk).astype(jnp.int32)
    n_gt = lax.fori_loop(0, tie_rounds, lambda _, c: count(lambda key, pos: key > tau), jnp.zeros((1, qb), jnp.int32))
    quota = topk - n_gt

    def pos_step(it, pos_end):
        cand = pos_end + (jnp.int32(1) << (pos_bits - 1 - it))
        n_tie = count(lambda key, pos: (key == tau) & (pos < cand))
        return jnp.where(n_tie <= quota, cand, pos_end)

    pos_all = jnp.int32((1 << pos_bits) - 1)
    pos_end = lax.fori_loop(0, tie_rounds * pos_bits, pos_step,
                            jnp.full((1, qb), 1, jnp.int32) * (pos_all * (1 - tie_rounds)))

    def write_chunk(c, carry):
        r0 = pl.multiple_of(c * kc, kc)
        key = key_ref[pl.ds(r0, kc), :]
        tie_kept = (key == tau) & (r0 + srow < pos_end)
        bias = jnp.where(key > tau, 0.0, jnp.where(tie_kept, 0.0, NEG))
        bias_ref[pl.ds(r0, kc), :] = bias.astype(bias_ref.dtype)
        return carry

    lax.fori_loop(0, nch, write_chunk, 0)

    def fill_chunk(c, carry):
        r0 = pl.multiple_of(c * kc, kc)
        bias_ref[pl.ds(r0, kc), :] = jnp.full((kc, qb), NEG, bias_ref.dtype)
        return carry

    lax.fori_loop(nch, nkc_total, fill_chunk, 0)


def dsa_index(ki, qi, wiT, topk, qb=512, kc=256):
    S = ki.shape[0]
    qb = min(qb, S)
    kc = min(kc, qb)
    return pl.pallas_call(
        functools.partial(_dsa_index_body, qb=qb, kc=kc, topk=topk, nkc_total=S // kc),
        grid=(S // qb,),
        in_specs=[pl.BlockSpec((S, IDX_HDIM), lambda i: (0, 0)),
                  pl.BlockSpec((qb, IDX_HEADS * IDX_HDIM), lambda i: (i, 0)),
                  pl.BlockSpec((IDX_HEADS, qb), lambda i: (0, i))],
        out_specs=pl.BlockSpec((S, qb), lambda i: (0, i)),
        out_shape=jax.ShapeDtypeStruct((S, S), BF16),
        scratch_shapes=[pltpu.VMEM((S, qb), jnp.int32)],
        compiler_params=_cparams(("parallel",)),
    )(ki, qi, wiT)


def _dsa_attn_body(qi_ref, kj_ref, q_ref, k_ref, vT_ref, bias_ref, oT_ref, m_ref, l_ref, acc_ref, *, qb, kc):
    step = pl.program_id(0)
    i = qi_ref[step]
    j = kj_ref[step]
    exp_scale = C_HDIM ** -0.5 * math.log2(math.e)
    heads = [slice(h * C_HDIM, (h + 1) * C_HDIM) for h in range(C_HEADS)]

    @pl.when(j == 0)
    def _():
        m_ref[...] = jnp.full_like(m_ref, NEG)
        l_ref[...] = jnp.zeros_like(l_ref)
        acc_ref[...] = jnp.zeros_like(acc_ref)

    bias = bias_ref[...].astype(F32)
    sT = [lax.dot_general(k_ref[:, cs], q_ref[:, cs], (((1,), (1,)), ((), ())),
                          preferred_element_type=F32) + bias for cs in heads]
    m_old = m_ref[...]
    m_new = jnp.maximum(m_old, jnp.concatenate([jnp.max(s, axis=0, keepdims=True) for s in sT], axis=0))
    alpha = jnp.exp2((m_old - m_new) * exp_scale)
    p = [jnp.exp2((s - m_new[h:h + 1, :]) * exp_scale) for h, s in enumerate(sT)]
    l_ref[...] = alpha * l_ref[...] + jnp.concatenate([jnp.sum(x, axis=0, keepdims=True) for x in p], axis=0)
    m_ref[...] = m_new
    for h, cs in enumerate(heads):
        acc_ref[cs, :] = alpha[h:h + 1, :] * acc_ref[cs, :] + jnp.dot(vT_ref[cs, :], p[h].astype(BF16),
                                                                       preferred_element_type=F32)

    @pl.when(j == ((i + 1) * qb - 1) // kc)
    def _():
        l = l_ref[...]
        for h, cs in enumerate(heads):
            oT_ref[cs, :] = (acc_ref[cs, :] / l[h:h + 1, :]).astype(oT_ref.dtype)


def dsa_attention(qkv, vT, bias, qb=256, kc=512):
    S = qkv.shape[0]
    qb = min(qb, S)
    kc = min(kc, S)
    pairs = [(i, j) for i in range(S // qb) for j in range(((i + 1) * qb - 1) // kc + 1)]
    qi = jnp.asarray([p_[0] for p_ in pairs], jnp.int32)
    kj = jnp.asarray([p_[1] for p_ in pairs], jnp.int32)
    grid_spec = pltpu.PrefetchScalarGridSpec(
        num_scalar_prefetch=2,
        grid=(len(pairs),),
        in_specs=[pl.BlockSpec((qb, C_WIDTH), lambda s, qi, kj: (qi[s], 0)),
                  pl.BlockSpec((kc, C_WIDTH), lambda s, qi, kj: (kj[s], 1)),
                  pl.BlockSpec((C_WIDTH, kc), lambda s, qi, kj: (0, kj[s])),
                  pl.BlockSpec((kc, qb), lambda s, qi, kj: (kj[s], qi[s]))],
        out_specs=pl.BlockSpec((C_WIDTH, qb), lambda s, qi, kj: (0, qi[s])),
        scratch_shapes=[pltpu.VMEM((C_HEADS, qb), F32), pltpu.VMEM((C_HEADS, qb), F32),
                        pltpu.VMEM((C_WIDTH, qb), F32)])
    return pl.pallas_call(
        functools.partial(_dsa_attn_body, qb=qb, kc=kc),
        grid_spec=grid_spec,
        out_shape=jax.ShapeDtypeStruct((C_WIDTH, S), BF16),
        compiler_params=_cparams(("arbitrary",)),
    )(qi, kj, qkv, qkv, vT, bias)


def _sigmoid(x):
    return 1.0 / (1.0 + jnp.exp(-x))


def _split_bf16(x, terms):
    parts = []
    for _ in range(terms):
        p = x.astype(BF16)
        parts.append(p)
        x = x - p.astype(F32)
    return parts


def _dot_bf16x3(a, b):
    a_hi, a_lo = _split_bf16(a, 2)
    b_hi, b_lo = _split_bf16(b, 2)
    dot = lambda x, y: jnp.dot(x, y, preferred_element_type=F32)
    return dot(a_hi, b_hi) + (dot(a_hi, b_lo) + dot(a_lo, b_hi))


def _dot_exact_lhs(a, b):
    a = a.astype(BF16)
    b_hi, b_mid, b_lo = _split_bf16(b, 3)
    dot = lambda y: jnp.dot(a, y, preferred_element_type=F32)
    return dot(b_hi) + (dot(b_mid) + dot(b_lo))


def _shift_rows(cur, prev_ref, first_tile):
    prev_row = jnp.where(first_tile, 0.0, prev_ref[7:8, :].astype(F32))
    rolled = pltpu.roll(cur, 1, axis=0)
    row = lax.broadcasted_iota(jnp.int32, cur.shape, 0)
    return jnp.where(row == 0, prev_row, rolled)


def _rwkv_lora_body(*refs, has_v):
    if has_v:
        (h_ref, hp_ref, mu_ref, w1_ref, a1_ref, g1_ref, v1_ref, ow_ref, oa_ref, og_ref, ov_ref) = refs
    else:
        (h_ref, hp_ref, mu_ref, w1_ref, a1_ref, g1_ref, ow_ref, oa_ref, og_ref) = refs
    i = pl.program_id(0)
    h = h_ref[...].astype(F32)
    dh = _shift_rows(h, hp_ref, i == 0) - h

    def lora(row, w_ref):
        xm = (h + dh * mu_ref[row:row + 1, :]).astype(BF16)
        return jnp.dot(xm, w_ref[...], preferred_element_type=F32)

    ow_ref[...] = jnp.tanh(lora(0, w1_ref))
    oa_ref[...] = lora(1, a1_ref)
    og_ref[...] = _sigmoid(lora(2, g1_ref))
    if has_v:
        ov_ref[...] = lora(3, v1_ref)


def rwkv_lora(h, mu, w1, a1, g1, v1, tm=256):
    S, D = h.shape
    tm = min(tm, S)
    has_v = v1 is not None
    ws = [w1, a1, g1] + ([v1] if has_v else [])
    full = lambda a: pl.BlockSpec(a.shape, lambda i: (0, 0))
    return pl.pallas_call(
        functools.partial(_rwkv_lora_body, has_v=has_v),
        grid=(S // tm,),
        in_specs=[pl.BlockSpec((tm, D), lambda i: (i, 0)),
                  pl.BlockSpec((8, D), lambda i: (jnp.maximum(i * (tm // 8) - 1, 0), 0)),
                  full(mu)] + [full(w) for w in ws],
        out_specs=[pl.BlockSpec((tm, w.shape[1]), lambda i: (i, 0)) for w in ws],
        out_shape=[jax.ShapeDtypeStruct((S, w.shape[1]), F32) for w in ws],
        compiler_params=_cparams(("parallel",)),
    )(h, h, mu, *ws)


(V_MU_R, V_MU_K, V_MU_V, V_W0, V_A0, V_KK, V_KA, V_RK, V_V0) = range(9)
N_VEC_ROWS = 16


def _wkv_pre_body(*refs, has_v, tt):
    if has_v:
        (r_ref, k_ref, v_ref, rp_ref, kp_ref, vp_ref, hw_ref, ha_ref, hg_ref, w2_ref, a2_ref, g2_ref,
         vec_ref, hv_ref, v2_ref, vf_ref,
         m_out, g_out, q_out, z_out, vout_ref, gate_ref, bonus_ref) = refs
    else:
        (r_ref, k_ref, v_ref, rp_ref, kp_ref, vp_ref, hw_ref, ha_ref, hg_ref, w2_ref, a2_ref, g2_ref,
         vec_ref,
         m_out, g_out, q_out, z_out, vout_ref, gate_ref, bonus_ref) = refs
    i = pl.program_id(0)
    first = i == 0
    C = WKV_CHUNK
    N = B_HDIM
    vec = lambda row: vec_ref[row:row + 1, :]

    r = r_ref[...]
    k = k_ref[...]
    v = v_ref[...]
    r = r + (_shift_rows(r, rp_ref, first) - r) * vec(V_MU_R)
    k = k + (_shift_rows(k, kp_ref, first) - k) * vec(V_MU_K)
    v = v + (_shift_rows(v, vp_ref, first) - v) * vec(V_MU_V)

    wl = vec(V_W0) + _dot_bf16x3(hw_ref[...], w2_ref[...])
    z = -wl
    softplus = jnp.maximum(z, 0.0) + jnp.log(1.0 + jnp.exp(-jnp.abs(z)))
    logw = -jnp.exp(-softplus - 0.5)
    bdot = lambda a_ref, b_ref: jnp.dot(a_ref[...].astype(BF16), b_ref[...].astype(BF16), preferred_element_type=F32)
    a_sig = _sigmoid(vec(V_A0) + bdot(ha_ref, a2_ref))
    gate = bdot(hg_ref, g2_ref)
    if has_v:
        mix = _sigmoid(vec(V_V0) + bdot(hv_ref, v2_ref))
        v = v + (vf_ref[...] - v) * mix
    vout_ref[...] = v
    gate_ref[...] = gate

    kk = k * vec(V_KK)
    k_new = k * (1.0 + (a_sig - 1.0) * vec(V_KA))
    rk = r * k_new * vec(V_RK)

    rowc = lax.broadcasted_iota(jnp.int32, (C, C), 0)
    colc = lax.broadcasted_iota(jnp.int32, (C, C), 1)
    tril_incl = (rowc >= colc).astype(F32)
    prow = lax.broadcasted_iota(jnp.int32, (2 * C, 2 * C), 0)
    pcol = lax.broadcasted_iota(jnp.int32, (2 * C, 2 * C), 1)
    same_head = (prow >= C) == (pcol >= C)
    dstep = jnp.where(prow >= C, prow - C, prow) - jnp.where(pcol >= C, pcol - C, pcol)
    strict_bd = jnp.where(same_head, dstep, -1) > 0
    incl_bd = jnp.where(same_head, dstep, -1) >= 0
    eye_p = (prow == pcol).astype(F32)
    head_ones = jnp.where(same_head, 1.0, 0.0).astype(BF16)

    def head_sum(x):
        hi = x.astype(BF16)
        lo = (x - hi.astype(F32)).astype(BF16)
        return (jnp.dot(hi, head_ones, preferred_element_type=F32)
                + jnp.dot(lo, head_ones, preferred_element_type=F32))

    kkn = kk / jnp.maximum(jnp.sqrt(head_sum(kk * kk)), 1e-12)
    bonus_ref[...] = head_sum(rk) * v
    a_scan = -kkn
    b_scan = kkn * a_sig

    head0 = lax.broadcasted_iota(jnp.int32, (C, LANES), 1) < N

    def stack2(x):
        return jnp.concatenate([jnp.where(head0, x, 0.0), jnp.where(head0, 0.0, x)], axis=0)

    ts_of = lambda c: slice(c * C, (c + 1) * C)
    st = []
    for c in range(tt // C):
        ts = ts_of(c)
        lw = logw[ts]
        cum = _dot_exact_lhs(tril_incl, lw)
        cum_last = cum[C - 1:C, :]
        e_neg = jnp.exp(-cum)
        e_end = jnp.exp(cum_last - cum)
        st.append(dict(At=stack2(a_scan[ts] * jnp.exp(cum - lw)), Rt=stack2(r[ts] * jnp.exp(cum)),
                       Bt=b_scan[ts] * e_neg, Kt=k_new[ts] * e_neg,
                       Bg=stack2(b_scan[ts] * e_end), Kg=stack2(k_new[ts] * e_end),
                       V=stack2(v[ts]).astype(BF16), gam=jnp.exp(cum_last)))
    for u in st:
        left = jnp.concatenate([u["At"], u["Rt"]], axis=0).astype(BF16)
        right = jnp.concatenate([u["Bt"], u["Bt"], u["Kt"], u["Kt"]], axis=0).astype(BF16)
        AA = lax.dot_general(left, right, (((1,), (1,)), ((), ())), preferred_element_type=F32)
        u["A_ab"] = jnp.where(strict_bd, AA[:2 * C, :2 * C], 0.0)
        u["A_ak"] = jnp.where(strict_bd, AA[:2 * C, 2 * C:], 0.0)
        u["A_r"] = jnp.concatenate([jnp.where(incl_bd, AA[2 * C:, :2 * C], 0.0),
                                    jnp.where(incl_bd, AA[2 * C:, 2 * C:], 0.0)], axis=-1).astype(BF16)
    for u in st:
        akv = jnp.dot(u["A_ak"].astype(BF16), u["V"], preferred_element_type=F32)
        u["XA"] = jnp.concatenate([u["At"] + pltpu.roll(akv, N, axis=1), u["A_ab"]], axis=-1)
    for step in range(6):
        for u in st:
            XA = u["XA"]
            Ap = XA[:, LANES:].astype(BF16)
            if step < 5:
                prod = jnp.dot(Ap, XA.astype(BF16), preferred_element_type=F32)
                u["XA"] = jnp.concatenate([XA[:, :LANES] + prod[:, :LANES], prod[:, LANES:]], axis=-1)
            else:
                u["X"] = XA[:, :LANES] + jnp.dot(Ap, XA[:, :LANES].astype(BF16), preferred_element_type=F32)
    own = (lax.broadcasted_iota(jnp.int32, (2 * C, LANES), 0) >= C) == (lax.broadcasted_iota(jnp.int32, (2 * C, LANES), 1) >= N)
    for u in st:
        p1 = jnp.where(own, u["X"], 0.0).astype(BF16)
        p2 = pltpu.roll(jnp.where(own, 0.0, u["X"]), N, axis=1).astype(BF16)
        upper = jnp.concatenate([p1, p2], axis=-1)
        lower = jnp.concatenate([jnp.zeros((2 * C, LANES), BF16), u["V"]], axis=-1)
        u["W2"] = jnp.concatenate([upper, lower], axis=0)
    for u in st:
        bk = jnp.concatenate([u["Bg"], u["Kg"]], axis=0).astype(BF16)
        u["MG"] = lax.dot_general(bk, u["W2"], (((0,), (0,)), ((), ())), preferred_element_type=F32)
    for u in st:
        u["QZ"] = jnp.dot(u["A_r"], u["W2"], preferred_element_type=F32)
    for c, u in enumerate(st):
        rows = slice(c * LANES, (c + 1) * LANES)
        m_out[rows, :] = (eye_p * u["gam"] + u["MG"][:, :LANES]).astype(m_out.dtype)
        g_out[rows, :] = u["MG"][:, LANES:].astype(g_out.dtype)
        qs = u["Rt"] + u["QZ"][:, :LANES]
        zs = u["QZ"][:, LANES:]
        q_out[ts_of(c), :] = (qs[:C] + qs[C:]).astype(q_out.dtype)
        z_out[ts_of(c), :] = zs[:C] + zs[C:]


def wkv_pre(proj_b, hids, w2s, vecs, v_first, tt=1024):
    S = proj_b.shape[0]
    tt = min(tt, S)
    has_v = v_first is not None
    nb = B_WIDTH // LANES
    nstate_rows = (tt // WKV_CHUNK) * LANES
    tok = lambda off: pl.BlockSpec((tt, LANES), lambda i, p: (i, off + p))
    prev = lambda off: pl.BlockSpec((8, LANES), lambda i, p: (jnp.maximum(i * (tt // 8) - 1, 0), off + p))
    hid = lambda a: pl.BlockSpec((tt, a.shape[1]), lambda i, p: (i, 0))
    wcol = lambda a: pl.BlockSpec((a.shape[0], LANES), lambda i, p: (0, p))
    in_specs = [tok(0), tok(nb), tok(2 * nb), prev(0), prev(nb), prev(2 * nb),
                hid(hids[0]), hid(hids[1]), hid(hids[2]), wcol(w2s[0]), wcol(w2s[1]), wcol(w2s[2]),
                pl.BlockSpec((N_VEC_ROWS, LANES), lambda i, p: (0, p))]
    args = [proj_b] * 6 + list(hids[:3]) + list(w2s[:3]) + [vecs]
    if has_v:
        in_specs += [hid(hids[3]), wcol(w2s[3]), tok(0)]
        args += [hids[3], w2s[3], v_first]
    state_shape = jax.ShapeDtypeStruct(((S // WKV_CHUNK) * LANES, B_WIDTH), BF16)
    out_tok = pl.BlockSpec((tt, LANES), lambda i, p: (i, p))
    out_st = pl.BlockSpec((nstate_rows, LANES), lambda i, p: (i, p))
    return pl.pallas_call(
        functools.partial(_wkv_pre_body, has_v=has_v, tt=tt),
        grid=(S // tt, nb),
        in_specs=in_specs,
        out_specs=[out_st, out_st, out_tok, out_tok, out_tok, out_tok, out_tok],
        out_shape=[state_shape, state_shape, jax.ShapeDtypeStruct((S, B_WIDTH), BF16)]
                  + [jax.ShapeDtypeStruct((S, B_WIDTH), F32)] * 4,
        compiler_params=_cparams(("parallel", "parallel")),
    )(*args)


def _wkv_scan_body(m_ref, g_ref, q_ref, z_ref, gate_ref, bonus_ref, ln_ref, o_ref, h_ref, *, tt):
    i = pl.program_id(0)
    C = WKV_CHUNK
    N = B_HDIM
    npair = B_WIDTH // LANES

    @pl.when(i == 0)
    def _():
        h_ref[...] = jnp.zeros_like(h_ref)

    row = lax.broadcasted_iota(jnp.int32, (LANES, LANES), 0)
    col = lax.broadcasted_iota(jnp.int32, (LANES, LANES), 1)
    head_avg = jnp.where((row // N) == (col // N), 1.0 / N, 0.0).astype(BF16)

    def head_mean(x):
        hi = x.astype(BF16)
        lo = (x - hi.astype(F32)).astype(BF16)
        return (jnp.dot(hi, head_avg, preferred_element_type=F32)
                + jnp.dot(lo, head_avg, preferred_element_type=F32))

    for c in range(tt // C):
        ts = slice(c * C, (c + 1) * C)
        ss = slice(c * LANES, (c + 1) * LANES)
        ys = []
        for p in range(npair):
            ps = slice(p * LANES, (p + 1) * LANES)
            Hb = h_ref[p].astype(BF16)
            ys.append(jnp.dot(q_ref[ts, ps], Hb, preferred_element_type=F32) + z_ref[ts, ps])
            h_ref[p] = jnp.dot(m_ref[ss, ps], Hb, preferred_element_type=F32) + g_ref[ss, ps].astype(F32)
        for p in range(npair):
            ps = slice(p * LANES, (p + 1) * LANES)
            yc = ys[p] - head_mean(ys[p])
            var = head_mean(yc * yc)
            yn = yc * lax.rsqrt(var + B_GN_EPS) * ln_ref[0:1, ps] + ln_ref[1:2, ps]
            o_ref[ts, ps] = ((yn + bonus_ref[ts, ps]) * gate_ref[ts, ps]).astype(o_ref.dtype)


def wkv_scan(m, g, q, z, gate, bonus, ln, tt=256):
    S = q.shape[0]
    tt = min(tt, S)
    npair = B_WIDTH // LANES
    nstate_rows = (tt // WKV_CHUNK) * LANES
    tok = pl.BlockSpec((tt, B_WIDTH), lambda i: (i, 0))
    st = pl.BlockSpec((nstate_rows, B_WIDTH), lambda i: (i, 0))
    return pl.pallas_call(
        functools.partial(_wkv_scan_body, tt=tt),
        grid=(S // tt,),
        in_specs=[st, st, tok, tok, tok, tok, pl.BlockSpec((8, B_WIDTH), lambda i: (0, 0))],
        out_specs=tok,
        out_shape=jax.ShapeDtypeStruct((S, B_WIDTH), BF16),
        scratch_shapes=[pltpu.VMEM((npair, LANES, LANES), F32)],
        compiler_params=_cparams(("arbitrary",)),
    )(m, g, q, z, gate, bonus, ln)


def rwkv7_mixer(h, proj_b, p, v_first):
    has_v = v_first is not None
    mu = p["mu_wag"] if not has_v else jnp.concatenate([p["mu_wag"], p["v_mu"][None]], axis=0)
    hids = rwkv_lora(h, mu, p["w1"].astype(BF16), p["a1"].astype(BF16), p["g1"].astype(BF16),
                     p["v1"].astype(BF16) if has_v else None)
    rows = [p["mu_rkv"][0], p["mu_rkv"][1], p["mu_rkv"][2], p["w0"], p["a0"], p["k_k"], p["k_a"],
            p["r_k"].reshape(-1), p["v0"] if has_v else jnp.zeros((B_WIDTH,), F32)]
    vecs = jnp.concatenate([jnp.stack(rows), jnp.zeros((N_VEC_ROWS - len(rows), B_WIDTH), F32)], axis=0)
    w2s = [p["w2"], p["a2"], p["g2"]] + ([p["v2"]] if has_v else [])
    m, g, q, z, v_out, gate, bonus = wkv_pre(proj_b, hids, w2s, vecs, v_first)
    ln = jnp.concatenate([p["ln_w"][None], p["ln_b"][None], jnp.zeros((6, B_WIDTH), F32)], axis=0)
    return wkv_scan(m, g, q, z, gate, bonus, ln), v_out


def kernel(x, norm_mix_g, w_in, lam_q1, lam_k1, lam_q2, lam_k2, diff_subln_g, rw_mu_rkv, rw_mu_wag, rw_w0, rw_w1, rw_w2, rw_a0, rw_a1, rw_a2, rw_g1, rw_g2, rw_k_k, rw_k_a, rw_r_k, rw_ln_w, rw_ln_b, rw_v_mu, rw_v0, rw_v1, rw_v2, w_out, norm_ffn_g, w_up, w_down, norm_final_g):
    Bsz, S, D = x.shape
    depth = w_in.shape[0]
    topk = min(TOPK_MAX, S // 4)
    w_in_t = jnp.swapaxes(w_in, 1, 2)
    w_down_bf = w_down.astype(BF16)
    nq = IDX_HEADS * IDX_HDIM
    n_kw = IDX_HDIM + IDX_HEADS
    w_idx_kw = w_in_t[:, w_in.shape[2] - n_kw:, :]
    outs = []
    for b in range(Bsz):
        xb = x[b]
        v_first = None
        for l in range(depth):
            h = rmsnorm(xb, norm_mix_g[l], BF16)
            oA, oB, oC = 3 * A_WIDTH, 3 * A_WIDTH + 3 * B_WIDTH, 3 * A_WIDTH + 3 * B_WIDTH + 3 * C_WIDTH
            proj_a = matmul_ws(h, w_in_t, l, 0, oA, BF16, tn=512, w_transposed=True)
            proj_b = matmul_ws(h, w_in_t, l, oA, oB - oA, F32, tn=512, w_transposed=True)
            proj_c = matmul_ws(h, w_in_t, l, oB, oC - oB, BF16, tn=512, w_transposed=True)
            qi = matmul_ws(h, w_in_t, l, oC, nq, BF16, tn=512, w_transposed=True)
            proj_kw = matmul_ws(h, w_idx_kw, l, 0, n_kw, F32, tn=n_kw, w_transposed=True)

            lam_init = 0.8 - 0.6 * math.exp(-0.3 * l)
            lam_vecs = jnp.stack([lam_q1[l], lam_k1[l], lam_q2[l], lam_k2[l]])
            o_a = diff_attention(proj_a, lam_vecs, diff_subln_g[l], lam_init)

            p = dict(mu_rkv=rw_mu_rkv[l], mu_wag=rw_mu_wag[l], w0=rw_w0[l], w1=rw_w1[l], w2=rw_w2[l],
                     a0=rw_a0[l], a1=rw_a1[l], a2=rw_a2[l], g1=rw_g1[l], g2=rw_g2[l], k_k=rw_k_k[l],
                     k_a=rw_k_a[l], r_k=rw_r_k[l], ln_w=rw_ln_w[l], ln_b=rw_ln_b[l])
            if l > 0:
                p.update(v_mu=rw_v_mu[l - 1], v0=rw_v0[l - 1], v1=rw_v1[l - 1], v2=rw_v2[l - 1])
            o_b, v_out = rwkv7_mixer(h, proj_b, p, v_first if l > 0 else None)
            if l == 0:
                v_first = v_out

            ki = proj_kw[:, :IDX_HDIM].astype(BF16)
            wiT = proj_kw[:, IDX_HDIM:].T
            bias = dsa_index(ki, qi, wiT, topk)
            vT = proj_c[:, 2 * C_WIDTH:].T
            o_c = dsa_attention(proj_c, vT, bias).T

            mixed = jnp.concatenate([o_a, o_b, o_c], axis=-1)
            xb = matmul_ws(mixed, w_out, l, 0, D, F32, tn=512, epilogue="residual", residual=xb)
            h2 = rmsnorm(xb, norm_ffn_g[l], BF16)
            up = matmul_ws(h2, w_up, l, 0, w_up.shape[2], BF16, tn=1024, tm=512, epilogue="relu2")
            xb = matmul(up, w_down_bf, l, F32, epilogue="residual", residual=xb)
        outs.append(rmsnorm(xb, norm_final_g, F32))
    return jnp.stack(outs)
```

```python
import functools
import math

import jax
import jax.numpy as jnp
from jax import lax
from jax.experimental import pallas as pl
from jax.experimental.pallas import tpu as pltpu

F32 = jnp.float32
BF16 = jnp.bfloat16

A_HEADS, A_HDIM = 8, 64
A_WIDTH = A_HEADS * 2 * A_HDIM
B_HDIM, B_WIDTH = 64, 2048
B_HEADS = B_WIDTH // B_HDIM
B_GN_EPS = 64e-5
C_HEADS, C_HDIM = 8, 128
C_WIDTH = C_HEADS * C_HDIM
IDX_HEADS, IDX_HDIM = 16, 64
TOPK_MAX = 256
EPS = 1e-6

LANES = 128
VMEM_LIMIT = 56 * 1024 * 1024
NEG = -1e30
WKV_CHUNK = 64
SOFTMAX_ROWS = 64


def _cparams(sem):
    return pltpu.CompilerParams(dimension_semantics=sem, vmem_limit_bytes=VMEM_LIMIT)


def _rmsnorm_body(x_ref, g_ref, o_ref):
    x = x_ref[...]
    ms = jnp.mean(x * x, axis=-1, keepdims=True)
    o_ref[...] = (x * lax.rsqrt(ms + EPS) * g_ref[...]).astype(o_ref.dtype)


def rmsnorm(x, g, out_dtype, tm=256):
    S, D = x.shape
    tm = min(tm, S)
    return pl.pallas_call(
        _rmsnorm_body,
        grid=(S // tm,),
        in_specs=[pl.BlockSpec((tm, D), lambda i: (i, 0)),
                  pl.BlockSpec((1, D), lambda i: (0, 0))],
        out_specs=pl.BlockSpec((tm, D), lambda i: (i, 0)),
        out_shape=jax.ShapeDtypeStruct((S, D), out_dtype),
        compiler_params=_cparams(("parallel",)),
    )(x, g.reshape(1, D))


def _mm_body(a_ref, b_ref, *rest, nk, epilogue):
    if epilogue == "residual":
        res_ref, o_ref, acc_ref = rest
    else:
        o_ref, acc_ref = rest
    k = pl.program_id(2)

    @pl.when(k == 0)
    def _():
        acc_ref[...] = jnp.zeros_like(acc_ref)

    acc_ref[...] += jnp.dot(a_ref[...], b_ref[...], preferred_element_type=F32)

    @pl.when(k == nk - 1)
    def _():
        acc = acc_ref[...]
        if epilogue == "relu2":
            r = jnp.maximum(acc, 0.0)
            acc = r * r
        elif epilogue == "residual":
            acc = acc + res_ref[...]
        o_ref[...] = acc.astype(o_ref.dtype)


def _pick(n, pref):
    for t in pref:
        if n % t == 0:
            return t
    return n


def matmul(a, b3, layer, out_dtype, epilogue="none", residual=None):
    M, K = a.shape
    _, _, N = b3.shape
    tm = _pick(M, (1024, 512, 256))
    tn = _pick(N, (1024, 768, 512, 384, 256, 128))
    tk = _pick(K, (2048, 1024, 512))
    nk = K // tk
    in_specs = [pl.BlockSpec((tm, tk), lambda i, j, k: (i, k)),
                pl.BlockSpec((None, tk, tn), lambda i, j, k: (layer, k, j))]
    args = [a, b3]
    if epilogue == "residual":
        in_specs.append(pl.BlockSpec((tm, tn), lambda i, j, k: (i, j)))
        args.append(residual)
    return pl.pallas_call(
        functools.partial(_mm_body, nk=nk, epilogue=epilogue),
        grid=(M // tm, N // tn, nk),
        in_specs=in_specs,
        out_specs=pl.BlockSpec((tm, tn), lambda i, j, k: (i, j)),
        out_shape=jax.ShapeDtypeStruct((M, N), out_dtype),
        scratch_shapes=[pltpu.VMEM((tm, tn), F32)],
        compiler_params=_cparams(("parallel", "parallel", "arbitrary")),
    )(*args)


def _mm_ws_body(x_ref, w_ref, *rest, epilogue, w_transposed):
    if epilogue == "residual":
        res_ref, o_ref, wb_ref = rest
    else:
        o_ref, wb_ref = rest

    @pl.when(pl.program_id(1) == 0)
    def _():
        wb_ref[...] = w_ref[...].astype(BF16)

    contract = (((1,), (1,)), ((), ())) if w_transposed else (((1,), (0,)), ((), ()))
    acc = lax.dot_general(x_ref[...], wb_ref[...], contract, preferred_element_type=F32)
    if epilogue == "relu2":
        r = jnp.maximum(acc, 0.0)
        acc = r * r
    elif epilogue == "residual":
        acc = acc + res_ref[...]
    o_ref[...] = acc.astype(o_ref.dtype)


def matmul_ws(x, w3, layer, col0, ncols, out_dtype, tn, epilogue="none", residual=None, tm=1024, w_transposed=False):
    M, K = x.shape
    tm = min(tm, M)
    nj = ncols // tn
    jb = col0 // tn
    assert col0 % tn == 0 and ncols % tn == 0 and M % tm == 0
    if w_transposed:
        w_spec = pl.BlockSpec((None, tn, K), lambda j, i: (layer, jb + j, 0))
        wb_shape = (tn, K)
    else:
        w_spec = pl.BlockSpec((None, K, tn), lambda j, i: (layer, 0, jb + j))
        wb_shape = (K, tn)
    in_specs = [pl.BlockSpec((tm, K), lambda j, i: (i, 0)), w_spec]
    args = [x, w3]
    if epilogue == "residual":
        in_specs.append(pl.BlockSpec((tm, tn), lambda j, i: (i, j)))
        args.append(residual)
    return pl.pallas_call(
        functools.partial(_mm_ws_body, epilogue=epilogue, w_transposed=w_transposed),
        grid=(nj, M // tm),
        in_specs=in_specs,
        out_specs=pl.BlockSpec((tm, tn), lambda j, i: (i, j)),
        out_shape=jax.ShapeDtypeStruct((M, nj * tn), out_dtype),
        scratch_shapes=[pltpu.VMEM(wb_shape, BF16)],
        compiler_params=_cparams(("parallel", "arbitrary")),
    )(*args)


def _diffattn_body(q_ref, k_ref, v_ref, lam_ref, g_ref, o_ref, m_ref, acc_ref,
                   sa_ref, sb_ref, pa_ref, pb_ref, aa_ref, ab_ref, *, tq, lam_init):
    tk = tq
    i = pl.program_id(1)
    hw = 2 * A_HDIM
    q = q_ref[...] * (A_HDIM ** -0.5)
    lane = lax.broadcasted_iota(jnp.int32, (tq, hw), 1)
    zero = jnp.zeros_like(q)
    qz = jnp.concatenate([jnp.where(lane < A_HDIM, q, zero), jnp.where(lane >= A_HDIM, q, zero)], axis=0)
    m_ref[...] = jnp.full_like(m_ref, NEG)
    acc_ref[...] = jnp.zeros_like(acc_ref)
    ones = jnp.ones((tk, hw), BF16)

    def scores(j):
        k = k_ref[pl.ds(pl.multiple_of(j * tk, tk), tk), :]
        return lax.dot_general(qz, k, (((1,), (1,)), ((), ())), preferred_element_type=F32)

    key_minus_row = (lax.broadcasted_iota(jnp.int32, (SOFTMAX_ROWS, tk), 1)
                     - lax.broadcasted_iota(jnp.int32, (SOFTMAX_ROWS, tk), 0))
    tile_rows = lambda j: pl.ds(pl.multiple_of(j * tk, tk), tk)

    def scores(j, s_out):
        s_out[...] = lax.dot_general(qz, k_ref[tile_rows(j), :], (((1,), (1,)), ((), ())),
                                     preferred_element_type=F32)

    def softmax(j, s_in, p_out, alpha_out, masked):
        for r0 in range(0, 2 * tq, SOFTMAX_ROWS):
            rs = slice(r0, r0 + SOFTMAX_ROWS)
            m_old = m_ref[rs, :]
            if masked:
                causal = key_minus_row <= (i * tq + r0 % tq) - j * tk
                row_max = jnp.max(jnp.where(causal, s_in[rs, :], NEG), axis=-1, keepdims=True)
            else:
                row_max = jnp.max(s_in[rs, :], axis=-1, keepdims=True)
            m_new = jnp.maximum(m_old, row_max)
            m_ref[rs, :] = m_new
            e = jnp.exp(s_in[rs, :] - jnp.concatenate([m_new] * (tk // hw), axis=-1))
            p_out[rs, :] = (jnp.where(causal, e, 0.0) if masked else e).astype(BF16)
            alpha_out[rs, :] = jnp.exp(m_old - m_new)

    def accumulate(j, p_in, alpha_in):
        v_ext = jnp.concatenate([v_ref[tile_rows(j), :], ones], axis=-1)
        alpha = alpha_in[...]
        acc_ref[...] = (jnp.concatenate([alpha, alpha], axis=-1) * acc_ref[...]
                        + jnp.dot(p_in[...], v_ext, preferred_element_type=F32))

    scores(0, sa_ref)
    pb_ref[...] = jnp.zeros_like(pb_ref)
    ab_ref[...] = jnp.ones_like(ab_ref)

    def pair(t, masked):
        j = 2 * t
        accumulate(jnp.maximum(j - 1, 0), pb_ref, ab_ref)
        softmax(j, sa_ref, pa_ref, aa_ref, masked)
        scores(j + 1, sb_ref)
        accumulate(j, pa_ref, aa_ref)
        softmax(j + 1, sb_ref, pb_ref, ab_ref, masked)
        if not masked:
            scores(j + 2, sa_ref)

    nfree = i // 2
    lax.fori_loop(0, nfree, lambda t, c: (pair(t, False), c)[1], 0)

    @pl.when(i % 2 == 0)
    def _():
        accumulate(jnp.maximum(i - 1, 0), pb_ref, ab_ref)
        softmax(i, sa_ref, pa_ref, aa_ref, True)
        accumulate(i, pa_ref, aa_ref)

    @pl.when(i % 2 == 1)
    def _():
        pair(nfree, True)
        accumulate(i, pb_ref, ab_ref)

    lv = lam_ref[...]
    lam = (jnp.exp(jnp.sum(lv[0:1] * lv[1:2], axis=-1, keepdims=True))
           - jnp.exp(jnp.sum(lv[2:3] * lv[3:4], axis=-1, keepdims=True)) + lam_init)
    acc = acc_ref[...]
    on = acc[:, :hw] / acc[:, hw:]
    o = on[:tq] - lam * on[tq:]
    ms = jnp.mean(o * o, axis=-1, keepdims=True)
    o_ref[...] = (o * lax.rsqrt(ms + EPS) * g_ref[...] * (1.0 - lam_init)).astype(o_ref.dtype)


def diff_attention(qkv, lam_vecs, subln_g, lam_init, tq=512):
    S = qkv.shape[0]
    tq = min(tq, S)
    hw = 2 * A_HDIM
    return pl.pallas_call(
        functools.partial(_diffattn_body, tq=tq, lam_init=lam_init),
        grid=(A_HEADS, S // tq),
        in_specs=[pl.BlockSpec((tq, hw), lambda h, i: (i, h)),
                  pl.BlockSpec((S, hw), lambda h, i: (0, A_HEADS + h)),
                  pl.BlockSpec((S, hw), lambda h, i: (0, 2 * A_HEADS + h)),
                  pl.BlockSpec((4, A_HDIM), lambda h, i: (0, 0)),
                  pl.BlockSpec((1, hw), lambda h, i: (0, 0))],
        out_specs=pl.BlockSpec((tq, hw), lambda h, i: (i, h)),
        out_shape=jax.ShapeDtypeStruct((S, A_WIDTH), BF16),
        scratch_shapes=[pltpu.VMEM((2 * tq, hw), F32), pltpu.VMEM((2 * tq, 2 * hw), F32),
                        pltpu.VMEM((2 * tq, tq), F32), pltpu.VMEM((2 * tq, tq), F32),
                        pltpu.VMEM((2 * tq, tq), BF16), pltpu.VMEM((2 * tq, tq), BF16),
                        pltpu.VMEM((2 * tq, hw), F32), pltpu.VMEM((2 * tq, hw), F32)],
        compiler_params=_cparams(("parallel", "parallel")),
    )(qkv, qkv, qkv, lam_vecs, subln_g.reshape(1, hw))


KEY_NEG_INF = -2139095041


def _float_key(s):
    b = pltpu.bitcast(s, jnp.int32)
    return b ^ ((b >> 31) & jnp.int32(0x7FFFFFFF))


def _dsa_index_body(ki_ref, qi_ref, wiT_ref, bias_ref, key_ref, *, qb, kc, topk, nkc_total):
    pos_bits = (nkc_total * kc).bit_length()
    i = pl.program_id(0)
    nch = ((i + 1) * qb) // kc
    wi = wiT_ref[...] * (IDX_HEADS ** -0.5 * IDX_HDIM ** -0.5)
    qi = qi_ref[...]
    tpos = i * qb + lax.broadcasted_iota(jnp.int32, (kc, qb), 1)
    srow = lax.broadcasted_iota(jnp.int32, (kc, qb), 0)

    def score_chunk(c, carry):
        r0 = pl.multiple_of(c * kc, kc)
        kic = ki_ref[pl.ds(r0, kc), :]
        acc = jnp.zeros((kc, qb), F32)
        for h in range(IDX_HEADS):
            d = lax.dot_general(kic, qi[:, h * IDX_HDIM:(h + 1) * IDX_HDIM],
                                (((1,), (1,)), ((), ())), preferred_element_type=F32)
            acc = acc + jnp.maximum(d, 0.0) * wi[h:h + 1, :]
        acc = jnp.where(r0 + srow <= tpos, acc, -jnp.inf)
        key_ref[pl.ds(r0, kc), :] = _float_key(acc)
        return carry

    lax.fori_loop(0, nch, score_chunk, 0)

    def count(pred):
        def body(c, cnt):
            r0 = pl.multiple_of(c * kc, kc)
            hit = pred(key_ref[pl.ds(r0, kc), :], r0 + srow).astype(jnp.int32)
            return cnt + jnp.sum(hit.reshape(kc // 8, 8, qb), axis=0)
        cnt8 = lax.fori_loop(0, nch, body, jnp.zeros((8, qb), jnp.int32))
        return jnp.sum(cnt8, axis=0, keepdims=True)

    def bit_step(carry):
        it, tau, n_ge, _ = carry
        cand = tau + (jnp.int32(1) << (31 - it))
        cnt = count(lambda key, pos: key >= cand)
        ok = cnt >= topk
        tau, n_ge = jnp.where(ok, cand, tau), jnp.where(ok, cnt, n_ge)
        settled = jnp.min(jnp.where(n_ge == topk, 1, 0))
        return it + 1, tau, n_ge, settled

    _, tau, n_ge, _ = lax.while_loop(
        lambda c: (c[0] < 32) & (c[3] == 0), bit_step,
        (jnp.int32(0), jnp.full((1, qb), jnp.iinfo(jnp.int32).min, jnp.int32), jnp.zeros((1, qb), jnp.int32),
         jnp.int32(0)))
    n_ge = jnp.where(tau <= KEY_NEG_INF, 0, n_ge)
    tau = jnp.maximum(tau, KEY_NEG_INF + 1)

    tie_rounds = (jnp.max(n_ge) > topk).astype(jnp.int32)
    n_gt = lax.fori_loop(0, tie_rounds, lambda _, c: count(lambda key, pos: key > tau), jnp.zeros((1, qb), jnp.int32))
    quota = topk - n_gt

    def pos_step(it, pos_end):
        cand = pos_end + (jnp.int32(1) << (pos_bits - 1 - it))
        n_tie = count(lambda key, pos: (key == tau) & (pos < cand))
        return jnp.where(n_tie <= quota, cand, pos_end)

    pos_all = jnp.int32((1 << pos_bits) - 1)
    pos_end = lax.fori_loop(0, tie_rounds * pos_bits, pos_step,
                            jnp.full((1, qb), 1, jnp.int32) * (pos_all * (1 - tie_rounds)))

    def write_chunk(c, carry):
        r0 = pl.multiple_of(c * kc, kc)
        key = key_ref[pl.ds(r0, kc), :]
        tie_kept = (key == tau) & (r0 + srow < pos_end)
        bias = jnp.where(key > tau, 0.0, jnp.where(tie_kept, 0.0, NEG))
        bias_ref[pl.ds(r0, kc), :] = bias.astype(bias_ref.dtype)
        return carry

    lax.fori_loop(0, nch, write_chunk, 0)

    def fill_chunk(c, carry):
        r0 = pl.multiple_of(c * kc, kc)
        bias_ref[pl.ds(r0, kc), :] = jnp.full((kc, qb), NEG, bias_ref.dtype)
        return carry

    lax.fori_loop(nch, nkc_total, fill_chunk, 0)


def dsa_index(ki, qi, wiT, topk, qb=512, kc=256):
    S = ki.shape[0]
    qb = min(qb, S)
    kc = min(kc, qb)
    return pl.pallas_call(
        functools.partial(_dsa_index_body, qb=qb, kc=kc, topk=topk, nkc_total=S // kc),
        grid=(S // qb,),
        in_specs=[pl.BlockSpec((S, IDX_HDIM), lambda i: (0, 0)),
                  pl.BlockSpec((qb, IDX_HEADS * IDX_HDIM), lambda i: (i, 0)),
                  pl.BlockSpec((IDX_HEADS, qb), lambda i: (0, i))],
        out_specs=pl.BlockSpec((S, qb), lambda i: (0, i)),
        out_shape=jax.ShapeDtypeStruct((S, S), BF16),
        scratch_shapes=[pltpu.VMEM((S, qb), jnp.int32)],
        compiler_params=_cparams(("parallel",)),
    )(ki, qi, wiT)


def _dsa_attn_body(qi_ref, kj_ref, q_ref, k_ref, vT_ref, bias_ref, oT_ref, m_ref, l_ref, acc_ref, *, qb, kc):
    step = pl.program_id(0)
    i = qi_ref[step]
    j = kj_ref[step]
    scale = C_HDIM ** -0.5
    heads = [slice(h * C_HDIM, (h + 1) * C_HDIM) for h in range(C_HEADS)]

    @pl.when(j == 0)
    def _():
        m_ref[...] = jnp.full_like(m_ref, NEG)
        l_ref[...] = jnp.zeros_like(l_ref)
        acc_ref[...] = jnp.zeros_like(acc_ref)

    bias = bias_ref[...].astype(F32)
    sT = [lax.dot_general(k_ref[:, cs], q_ref[:, cs], (((1,), (1,)), ((), ())),
                          preferred_element_type=F32) * scale + bias for cs in heads]
    m_old = m_ref[...]
    m_new = jnp.maximum(m_old, jnp.concatenate([jnp.max(s, axis=0, keepdims=True) for s in sT], axis=0))
    alpha = jnp.exp(m_old - m_new)
    p = [jnp.exp(s - m_new[h:h + 1, :]) for h, s in enumerate(sT)]
    l_ref[...] = alpha * l_ref[...] + jnp.concatenate([jnp.sum(x, axis=0, keepdims=True) for x in p], axis=0)
    m_ref[...] = m_new
    for h, cs in enumerate(heads):
        acc_ref[cs, :] = alpha[h:h + 1, :] * acc_ref[cs, :] + jnp.dot(vT_ref[cs, :], p[h].astype(BF16),
                                                                       preferred_element_type=F32)

    @pl.when(j == ((i + 1) * qb - 1) // kc)
    def _():
        l = l_ref[...]
        for h, cs in enumerate(heads):
            oT_ref[cs, :] = (acc_ref[cs, :] / l[h:h + 1, :]).astype(oT_ref.dtype)


def dsa_attention(qkv, vT, bias, qb=256, kc=512):
    S = qkv.shape[0]
    qb = min(qb, S)
    kc = min(kc, S)
    pairs = [(i, j) for i in range(S // qb) for j in range(((i + 1) * qb - 1) // kc + 1)]
    qi = jnp.asarray([p_[0] for p_ in pairs], jnp.int32)
    kj = jnp.asarray([p_[1] for p_ in pairs], jnp.int32)
    grid_spec = pltpu.PrefetchScalarGridSpec(
        num_scalar_prefetch=2,
        grid=(len(pairs),),
        in_specs=[pl.BlockSpec((qb, C_WIDTH), lambda s, qi, kj: (qi[s], 0)),
                  pl.BlockSpec((kc, C_WIDTH), lambda s, qi, kj: (kj[s], 1)),
                  pl.BlockSpec((C_WIDTH, kc), lambda s, qi, kj: (0, kj[s])),
                  pl.BlockSpec((kc, qb), lambda s, qi, kj: (kj[s], qi[s]))],
        out_specs=pl.BlockSpec((C_WIDTH, qb), lambda s, qi, kj: (0, qi[s])),
        scratch_shapes=[pltpu.VMEM((C_HEADS, qb), F32), pltpu.VMEM((C_HEADS, qb), F32),
                        pltpu.VMEM((C_WIDTH, qb), F32)])
    return pl.pallas_call(
        functools.partial(_dsa_attn_body, qb=qb, kc=kc),
        grid_spec=grid_spec,
        out_shape=jax.ShapeDtypeStruct((C_WIDTH, S), BF16),
        compiler_params=_cparams(("arbitrary",)),
    )(qi, kj, qkv, qkv, vT, bias)


def _sigmoid(x):
    return 1.0 / (1.0 + jnp.exp(-x))


def _split_bf16(x, terms):
    parts = []
    for _ in range(terms):
        p = x.astype(BF16)
        parts.append(p)
        x = x - p.astype(F32)
    return parts


def _dot_bf16x3(a, b):
    a_hi, a_lo = _split_bf16(a, 2)
    b_hi, b_lo = _split_bf16(b, 2)
    dot = lambda x, y: jnp.dot(x, y, preferred_element_type=F32)
    return dot(a_hi, b_hi) + (dot(a_hi, b_lo) + dot(a_lo, b_hi))


def _dot_exact_lhs(a, b):
    a = a.astype(BF16)
    b_hi, b_mid, b_lo = _split_bf16(b, 3)
    dot = lambda y: jnp.dot(a, y, preferred_element_type=F32)
    return dot(b_hi) + (dot(b_mid) + dot(b_lo))


def _shift_rows(cur, prev_ref, first_tile):
    prev_row = jnp.where(first_tile, 0.0, prev_ref[7:8, :].astype(F32))
    rolled = pltpu.roll(cur, 1, axis=0)
    row = lax.broadcasted_iota(jnp.int32, cur.shape, 0)
    return jnp.where(row == 0, prev_row, rolled)


def _rwkv_lora_body(*refs, has_v):
    if has_v:
        (h_ref, hp_ref, mu_ref, w1_ref, a1_ref, g1_ref, v1_ref, ow_ref, oa_ref, og_ref, ov_ref) = refs
    else:
        (h_ref, hp_ref, mu_ref, w1_ref, a1_ref, g1_ref, ow_ref, oa_ref, og_ref) = refs
    i = pl.program_id(0)
    h = h_ref[...].astype(F32)
    dh = _shift_rows(h, hp_ref, i == 0) - h

    def lora(row, w_ref):
        xm = (h + dh * mu_ref[row:row + 1, :]).astype(BF16)
        return jnp.dot(xm, w_ref[...], preferred_element_type=F32)

    ow_ref[...] = jnp.tanh(lora(0, w1_ref))
    oa_ref[...] = lora(1, a1_ref)
    og_ref[...] = _sigmoid(lora(2, g1_ref))
    if has_v:
        ov_ref[...] = lora(3, v1_ref)


def rwkv_lora(h, mu, w1, a1, g1, v1, tm=256):
    S, D = h.shape
    tm = min(tm, S)
    has_v = v1 is not None
    ws = [w1, a1, g1] + ([v1] if has_v else [])
    full = lambda a: pl.BlockSpec(a.shape, lambda i: (0, 0))
    return pl.pallas_call(
        functools.partial(_rwkv_lora_body, has_v=has_v),
        grid=(S // tm,),
        in_specs=[pl.BlockSpec((tm, D), lambda i: (i, 0)),
                  pl.BlockSpec((8, D), lambda i: (jnp.maximum(i * (tm // 8) - 1, 0), 0)),
                  full(mu)] + [full(w) for w in ws],
        out_specs=[pl.BlockSpec((tm, w.shape[1]), lambda i: (i, 0)) for w in ws],
        out_shape=[jax.ShapeDtypeStruct((S, w.shape[1]), F32) for w in ws],
        compiler_params=_cparams(("parallel",)),
    )(h, h, mu, *ws)


(V_MU_R, V_MU_K, V_MU_V, V_W0, V_A0, V_KK, V_KA, V_RK, V_V0) = range(9)
N_VEC_ROWS = 16


def _wkv_pre_body(*refs, has_v, tt):
    if has_v:
        (r_ref, k_ref, v_ref, rp_ref, kp_ref, vp_ref, hw_ref, ha_ref, hg_ref, w2_ref, a2_ref, g2_ref,
         vec_ref, hv_ref, v2_ref, vf_ref,
         m_out, g_out, q_out, z_out, vout_ref, gate_ref, bonus_ref) = refs
    else:
        (r_ref, k_ref, v_ref, rp_ref, kp_ref, vp_ref, hw_ref, ha_ref, hg_ref, w2_ref, a2_ref, g2_ref,
         vec_ref,
         m_out, g_out, q_out, z_out, vout_ref, gate_ref, bonus_ref) = refs
    i = pl.program_id(0)
    first = i == 0
    C = WKV_CHUNK
    N = B_HDIM
    vec = lambda row: vec_ref[row:row + 1, :]

    r = r_ref[...]
    k = k_ref[...]
    v = v_ref[...]
    r = r + (_shift_rows(r, rp_ref, first) - r) * vec(V_MU_R)
    k = k + (_shift_rows(k, kp_ref, first) - k) * vec(V_MU_K)
    v = v + (_shift_rows(v, vp_ref, first) - v) * vec(V_MU_V)

    wl = vec(V_W0) + _dot_bf16x3(hw_ref[...], w2_ref[...])
    z = -wl
    softplus = jnp.maximum(z, 0.0) + jnp.log(1.0 + jnp.exp(-jnp.abs(z)))
    logw = -jnp.exp(-softplus - 0.5)
    bdot = lambda a_ref, b_ref: jnp.dot(a_ref[...].astype(BF16), b_ref[...].astype(BF16), preferred_element_type=F32)
    a_sig = _sigmoid(vec(V_A0) + bdot(ha_ref, a2_ref))
    gate = bdot(hg_ref, g2_ref)
    if has_v:
        mix = _sigmoid(vec(V_V0) + bdot(hv_ref, v2_ref))
        v = v + (vf_ref[...] - v) * mix
    vout_ref[...] = v
    gate_ref[...] = gate

    kk = k * vec(V_KK)
    k_new = k * (1.0 + (a_sig - 1.0) * vec(V_KA))
    rk = r * k_new * vec(V_RK)

    rowc = lax.broadcasted_iota(jnp.int32, (C, C), 0)
    colc = lax.broadcasted_iota(jnp.int32, (C, C), 1)
    tril_incl = (rowc >= colc).astype(F32)
    prow = lax.broadcasted_iota(jnp.int32, (2 * C, 2 * C), 0)
    pcol = lax.broadcasted_iota(jnp.int32, (2 * C, 2 * C), 1)
    same_head = (prow >= C) == (pcol >= C)
    dstep = jnp.where(prow >= C, prow - C, prow) - jnp.where(pcol >= C, pcol - C, pcol)
    strict_bd = jnp.where(same_head, dstep, -1) > 0
    incl_bd = jnp.where(same_head, dstep, -1) >= 0
    eye_p = (prow == pcol).astype(F32)
    head_ones = jnp.where(same_head, 1.0, 0.0).astype(BF16)

    def head_sum(x):
        hi = x.astype(BF16)
        lo = (x - hi.astype(F32)).astype(BF16)
        return (jnp.dot(hi, head_ones, preferred_element_type=F32)
                + jnp.dot(lo, head_ones, preferred_element_type=F32))

    kkn = kk / jnp.maximum(jnp.sqrt(head_sum(kk * kk)), 1e-12)
    bonus_ref[...] = head_sum(rk) * v
    a_scan = -kkn
    b_scan = kkn * a_sig

    head0 = lax.broadcasted_iota(jnp.int32, (C, LANES), 1) < N

    def stack2(x):
        return jnp.concatenate([jnp.where(head0, x, 0.0), jnp.where(head0, 0.0, x)], axis=0)

    ts_of = lambda c: slice(c * C, (c + 1) * C)
    st = []
    for c in range(tt // C):
        ts = ts_of(c)
        lw = logw[ts]
        cum = _dot_exact_lhs(tril_incl, lw)
        cum_last = cum[C - 1:C, :]
        e_neg = jnp.exp(-cum)
        e_end = jnp.exp(cum_last - cum)
        st.append(dict(At=stack2(a_scan[ts] * jnp.exp(cum - lw)), Rt=stack2(r[ts] * jnp.exp(cum)),
                       Bt=b_scan[ts] * e_neg, Kt=k_new[ts] * e_neg,
                       Bg=stack2(b_scan[ts] * e_end), Kg=stack2(k_new[ts] * e_end),
                       V=stack2(v[ts]).astype(BF16), gam=jnp.exp(cum_last)))
    for u in st:
        left = jnp.concatenate([u["At"], u["Rt"]], axis=0).astype(BF16)
        right = jnp.concatenate([u["Bt"], u["Bt"], u["Kt"], u["Kt"]], axis=0).astype(BF16)
        AA = lax.dot_general(left, right, (((1,), (1,)), ((), ())), preferred_element_type=F32)
        u["A_ab"] = jnp.where(strict_bd, AA[:2 * C, :2 * C], 0.0)
        u["A_ak"] = jnp.where(strict_bd, AA[:2 * C, 2 * C:], 0.0)
        u["A_r"] = jnp.concatenate([jnp.where(incl_bd, AA[2 * C:, :2 * C], 0.0),
                                    jnp.where(incl_bd, AA[2 * C:, 2 * C:], 0.0)], axis=-1).astype(BF16)
    for u in st:
        akv = jnp.dot(u["A_ak"].astype(BF16), u["V"], preferred_element_type=F32)
        u["XA"] = jnp.concatenate([u["At"] + pltpu.roll(akv, N, axis=1), u["A_ab"]], axis=-1)
    for step in range(6):
        for u in st:
            XA = u["XA"]
            Ap = XA[:, LANES:].astype(BF16)
            if step < 5:
                prod = jnp.dot(Ap, XA.astype(BF16), preferred_element_type=F32)
                u["XA"] = jnp.concatenate([XA[:, :LANES] + prod[:, :LANES], prod[:, LANES:]], axis=-1)
            else:
                u["X"] = XA[:, :LANES] + jnp.dot(Ap, XA[:, :LANES].astype(BF16), preferred_element_type=F32)
    own = (lax.broadcasted_iota(jnp.int32, (2 * C, LANES), 0) >= C) == (lax.broadcasted_iota(jnp.int32, (2 * C, LANES), 1) >= N)
    for u in st:
        p1 = jnp.where(own, u["X"], 0.0).astype(BF16)
        p2 = pltpu.roll(jnp.where(own, 0.0, u["X"]), N, axis=1).astype(BF16)
        upper = jnp.concatenate([p1, p2], axis=-1)
        lower = jnp.concatenate([jnp.zeros((2 * C, LANES), BF16), u["V"]], axis=-1)
        u["W2"] = jnp.concatenate([upper, lower], axis=0)
    for u in st:
        bk = jnp.concatenate([u["Bg"], u["Kg"]], axis=0).astype(BF16)
        u["MG"] = lax.dot_general(bk, u["W2"], (((0,), (0,)), ((), ())), preferred_element_type=F32)
    for u in st:
        u["QZ"] = jnp.dot(u["A_r"], u["W2"], preferred_element_type=F32)
    for c, u in enumerate(st):
        rows = slice(c * LANES, (c + 1) * LANES)
        m_out[rows, :] = (eye_p * u["gam"] + u["MG"][:, :LANES]).astype(m_out.dtype)
        g_out[rows, :] = u["MG"][:, LANES:].astype(g_out.dtype)
        qs = u["Rt"] + u["QZ"][:, :LANES]
        zs = u["QZ"][:, LANES:]
        q_out[ts_of(c), :] = (qs[:C] + qs[C:]).astype(q_out.dtype)
        z_out[ts_of(c), :] = zs[:C] + zs[C:]


def wkv_pre(proj_b, hids, w2s, vecs, v_first, tt=1024):
    S = proj_b.shape[0]
    tt = min(tt, S)
    has_v = v_first is not None
    nb = B_WIDTH // LANES
    nstate_rows = (tt // WKV_CHUNK) * LANES
    tok = lambda off: pl.BlockSpec((tt, LANES), lambda i, p: (i, off + p))
    prev = lambda off: pl.BlockSpec((8, LANES), lambda i, p: (jnp.maximum(i * (tt // 8) - 1, 0), off + p))
    hid = lambda a: pl.BlockSpec((tt, a.shape[1]), lambda i, p: (i, 0))
    wcol = lambda a: pl.BlockSpec((a.shape[0], LANES), lambda i, p: (0, p))
    in_specs = [tok(0), tok(nb), tok(2 * nb), prev(0), prev(nb), prev(2 * nb),
                hid(hids[0]), hid(hids[1]), hid(hids[2]), wcol(w2s[0]), wcol(w2s[1]), wcol(w2s[2]),
                pl.BlockSpec((N_VEC_ROWS, LANES), lambda i, p: (0, p))]
    args = [proj_b] * 6 + list(hids[:3]) + list(w2s[:3]) + [vecs]
    if has_v:
        in_specs += [hid(hids[3]), wcol(w2s[3]), tok(0)]
        args += [hids[3], w2s[3], v_first]
    state_shape = jax.ShapeDtypeStruct(((S // WKV_CHUNK) * LANES, B_WIDTH), BF16)
    out_tok = pl.BlockSpec((tt, LANES), lambda i, p: (i, p))
    out_st = pl.BlockSpec((nstate_rows, LANES), lambda i, p: (i, p))
    return pl.pallas_call(
        functools.partial(_wkv_pre_body, has_v=has_v, tt=tt),
        grid=(S // tt, nb),
        in_specs=in_specs,
        out_specs=[out_st, out_st, out_tok, out_tok, out_tok, out_tok, out_tok],
        out_shape=[state_shape, state_shape, jax.ShapeDtypeStruct((S, B_WIDTH), BF16)]
                  + [jax.ShapeDtypeStruct((S, B_WIDTH), F32)] * 4,
        compiler_params=_cparams(("parallel", "parallel")),
    )(*args)


def _wkv_scan_body(m_ref, g_ref, q_ref, z_ref, gate_ref, bonus_ref, ln_ref, o_ref, h_ref, *, tt):
    i = pl.program_id(0)
    C = WKV_CHUNK
    N = B_HDIM
    npair = B_WIDTH // LANES

    @pl.when(i == 0)
    def _():
        h_ref[...] = jnp.zeros_like(h_ref)

    row = lax.broadcasted_iota(jnp.int32, (LANES, LANES), 0)
    col = lax.broadcasted_iota(jnp.int32, (LANES, LANES), 1)
    head_avg = jnp.where((row // N) == (col // N), 1.0 / N, 0.0).astype(BF16)

    def head_mean(x):
        hi = x.astype(BF16)
        lo = (x - hi.astype(F32)).astype(BF16)
        return (jnp.dot(hi, head_avg, preferred_element_type=F32)
                + jnp.dot(lo, head_avg, preferred_element_type=F32))

    for c in range(tt // C):
        ts = slice(c * C, (c + 1) * C)
        ss = slice(c * LANES, (c + 1) * LANES)
        ys = []
        for p in range(npair):
            ps = slice(p * LANES, (p + 1) * LANES)
            Hb = h_ref[p].astype(BF16)
            ys.append(jnp.dot(q_ref[ts, ps], Hb, preferred_element_type=F32) + z_ref[ts, ps])
            h_ref[p] = jnp.dot(m_ref[ss, ps], Hb, preferred_element_type=F32) + g_ref[ss, ps].astype(F32)
        for p in range(npair):
            ps = slice(p * LANES, (p + 1) * LANES)
            yc = ys[p] - head_mean(ys[p])
            var = head_mean(yc * yc)
            yn = yc * lax.rsqrt(var + B_GN_EPS) * ln_ref[0:1, ps] + ln_ref[1:2, ps]
            o_ref[ts, ps] = ((yn + bonus_ref[ts, ps]) * gate_ref[ts, ps]).astype(o_ref.dtype)


def wkv_scan(m, g, q, z, gate, bonus, ln, tt=256):
    S = q.shape[0]
    tt = min(tt, S)
    npair = B_WIDTH // LANES
    nstate_rows = (tt // WKV_CHUNK) * LANES
    tok = pl.BlockSpec((tt, B_WIDTH), lambda i: (i, 0))
    st = pl.BlockSpec((nstate_rows, B_WIDTH), lambda i: (i, 0))
    return pl.pallas_call(
        functools.partial(_wkv_scan_body, tt=tt),
        grid=(S // tt,),
        in_specs=[st, st, tok, tok, tok, tok, pl.BlockSpec((8, B_WIDTH), lambda i: (0, 0))],
        out_specs=tok,
        out_shape=jax.ShapeDtypeStruct((S, B_WIDTH), BF16),
        scratch_shapes=[pltpu.VMEM((npair, LANES, LANES), F32)],
        compiler_params=_cparams(("arbitrary",)),
    )(m, g, q, z, gate, bonus, ln)


def rwkv7_mixer(h, proj_b, p, v_first):
    has_v = v_first is not None
    mu = p["mu_wag"] if not has_v else jnp.concatenate([p["mu_wag"], p["v_mu"][None]], axis=0)
    hids = rwkv_lora(h, mu, p["w1"].astype(BF16), p["a1"].astype(BF16), p["g1"].astype(BF16),
                     p["v1"].astype(BF16) if has_v else None)
    rows = [p["mu_rkv"][0], p["mu_rkv"][1], p["mu_rkv"][2], p["w0"], p["a0"], p["k_k"], p["k_a"],
            p["r_k"].reshape(-1), p["v0"] if has_v else jnp.zeros((B_WIDTH,), F32)]
    vecs = jnp.concatenate([jnp.stack(rows), jnp.zeros((N_VEC_ROWS - len(rows), B_WIDTH), F32)], axis=0)
    w2s = [p["w2"], p["a2"], p["g2"]] + ([p["v2"]] if has_v else [])
    m, g, q, z, v_out, gate, bonus = wkv_pre(proj_b, hids, w2s, vecs, v_first)
    ln = jnp.concatenate([p["ln_w"][None], p["ln_b"][None], jnp.zeros((6, B_WIDTH), F32)], axis=0)
    return wkv_scan(m, g, q, z, gate, bonus, ln), v_out


def kernel(x, norm_mix_g, w_in, lam_q1, lam_k1, lam_q2, lam_k2, diff_subln_g, rw_mu_rkv, rw_mu_wag, rw_w0, rw_w1, rw_w2, rw_a0, rw_a1, rw_a2, rw_g1, rw_g2, rw_k_k, rw_k_a, rw_r_k, rw_ln_w, rw_ln_b, rw_v_mu, rw_v0, rw_v1, rw_v2, w_out, norm_ffn_g, w_up, w_down, norm_final_g):
    Bsz, S, D = x.shape
    depth = w_in.shape[0]
    topk = min(TOPK_MAX, S // 4)
    w_in_t = jnp.swapaxes(w_in, 1, 2)
    w_down_bf = w_down.astype(BF16)
    nq = IDX_HEADS * IDX_HDIM
    n_kw = IDX_HDIM + IDX_HEADS
    w_idx_kw = w_in_t[:, w_in.shape[2] - n_kw:, :]
    outs = []
    for b in range(Bsz):
        xb = x[b]
        v_first = None
        for l in range(depth):
            h = rmsnorm(xb, norm_mix_g[l], BF16)
            oA, oB, oC = 3 * A_WIDTH, 3 * A_WIDTH + 3 * B_WIDTH, 3 * A_WIDTH + 3 * B_WIDTH + 3 * C_WIDTH
            proj_a = matmul_ws(h, w_in_t, l, 0, oA, BF16, tn=512, w_transposed=True)
            proj_b = matmul_ws(h, w_in_t, l, oA, oB - oA, F32, tn=512, w_transposed=True)
            proj_c = matmul_ws(h, w_in_t, l, oB, oC - oB, BF16, tn=512, w_transposed=True)
            qi = matmul_ws(h, w_in_t, l, oC, nq, BF16, tn=512, w_transposed=True)
            proj_kw = matmul_ws(h, w_idx_kw, l, 0, n_kw, F32, tn=n_kw, w_transposed=True)

            lam_init = 0.8 - 0.6 * math.exp(-0.3 * l)
            lam_vecs = jnp.stack([lam_q1[l], lam_k1[l], lam_q2[l], lam_k2[l]])
            o_a = diff_attention(proj_a, lam_vecs, diff_subln_g[l], lam_init)

            p = dict(mu_rkv=rw_mu_rkv[l], mu_wag=rw_mu_wag[l], w0=rw_w0[l], w1=rw_w1[l], w2=rw_w2[l],
                     a0=rw_a0[l], a1=rw_a1[l], a2=rw_a2[l], g1=rw_g1[l], g2=rw_g2[l], k_k=rw_k_k[l],
                     k_a=rw_k_a[l], r_k=rw_r_k[l], ln_w=rw_ln_w[l], ln_b=rw_ln_b[l])
            if l > 0:
                p.update(v_mu=rw_v_mu[l - 1], v0=rw_v0[l - 1], v1=rw_v1[l - 1], v2=rw_v2[l - 1])
            o_b, v_out = rwkv7_mixer(h, proj_b, p, v_first if l > 0 else None)
            if l == 0:
                v_first = v_out

            ki = proj_kw[:, :IDX_HDIM].astype(BF16)
            wiT = proj_kw[:, IDX_HDIM:].T
            bias = dsa_index(ki, qi, wiT, topk)
            vT = proj_c[:, 2 * C_WIDTH:].T
            o_c = dsa_attention(proj_c, vT, bias).T

            mixed = jnp.concatenate([o_a, o_b, o_c], axis=-1)
            xb = matmul_ws(mixed, w_out, l, 0, D, F32, tn=512, epilogue="residual", residual=xb)
            h2 = rmsnorm(xb, norm_ffn_g[l], BF16)
            up = matmul_ws(h2, w_up, l, 0, w_up.shape[2], BF16, tn=1024, tm=512, epilogue="relu2")
            xb = matmul(up, w_down_bf, l, F32, epilogue="residual", residual=xb)
        outs.append(rmsnorm(xb, norm_final_g, F32))
    return jnp.stack(outs)
```

```python
import functools
import math

import jax
import jax.numpy as jnp
from jax import lax
from jax.experimental import pallas as pl
from jax.experimental.pallas import tpu as pltpu

F32 = jnp.float32
BF16 = jnp.bfloat16

A_HEADS, A_HDIM = 8, 64
A_WIDTH = A_HEADS * 2 * A_HDIM
B_HDIM, B_WIDTH = 64, 2048
B_HEADS = B_WIDTH // B_HDIM
B_GN_EPS = 64e-5
C_HEADS, C_HDIM = 8, 128
C_WIDTH = C_HEADS * C_HDIM
IDX_HEADS, IDX_HDIM = 16, 64
TOPK_MAX = 256
EPS = 1e-6

LANES = 128
VMEM_LIMIT = 56 * 1024 * 1024
NEG = -1e30
WKV_CHUNK = 64
SOFTMAX_ROWS = 64

NORM_ROWS = 256
WS_ROWS, WS_COLS = 1024, 512
ATTN_TILE = 512
INDEX_QUERIES, INDEX_KEYS = 512, 256
DSA_QUERIES, DSA_KEYS = 512, 1024
LORA_ROWS = 256
WKV_PRE_TOKENS = 1024
WKV_SCAN_TOKENS = 256


def _cparams(sem):
    return pltpu.CompilerParams(dimension_semantics=sem, vmem_limit_bytes=VMEM_LIMIT)


def _rmsnorm_body(x_ref, g_ref, o_ref):
    x = x_ref[...]
    ms = jnp.mean(x * x, axis=-1, keepdims=True)
    o_ref[...] = (x * lax.rsqrt(ms + EPS) * g_ref[...]).astype(o_ref.dtype)


def rmsnorm(x, g, out_dtype, tm=NORM_ROWS):
    S, D = x.shape
    tm = min(tm, S)
    return pl.pallas_call(
        _rmsnorm_body,
        grid=(S // tm,),
        in_specs=[pl.BlockSpec((tm, D), lambda i: (i, 0)),
                  pl.BlockSpec((1, D), lambda i: (0, 0))],
        out_specs=pl.BlockSpec((tm, D), lambda i: (i, 0)),
        out_shape=jax.ShapeDtypeStruct((S, D), out_dtype),
        compiler_params=_cparams(("parallel",)),
    )(x, g.reshape(1, D))


def _mm_body(a_ref, b_ref, *rest, nk, epilogue):
    if epilogue == "residual":
        res_ref, o_ref, acc_ref = rest
    else:
        o_ref, acc_ref = rest
    k = pl.program_id(2)

    @pl.when(k == 0)
    def _():
        acc_ref[...] = jnp.zeros_like(acc_ref)

    acc_ref[...] += jnp.dot(a_ref[...], b_ref[...], preferred_element_type=F32)

    @pl.when(k == nk - 1)
    def _():
        acc = acc_ref[...]
        if epilogue == "relu2":
            r = jnp.maximum(acc, 0.0)
            acc = r * r
        elif epilogue == "residual":
            acc = acc + res_ref[...]
        o_ref[...] = acc.astype(o_ref.dtype)


def _pick(n, pref):
    for t in pref:
        if n % t == 0:
            return t
    return n


def matmul(a, b3, layer, out_dtype, epilogue="none", residual=None):
    M, K = a.shape
    _, _, N = b3.shape
    tm = _pick(M, (1024, 512, 256))
    tn = _pick(N, (1024, 768, 512, 384, 256, 128))
    tk = _pick(K, (2048, 1024, 512))
    nk = K // tk
    in_specs = [pl.BlockSpec((tm, tk), lambda i, j, k: (i, k)),
                pl.BlockSpec((None, tk, tn), lambda i, j, k: (layer, k, j))]
    args = [a, b3]
    if epilogue == "residual":
        in_specs.append(pl.BlockSpec((tm, tn), lambda i, j, k: (i, j)))
        args.append(residual)
    return pl.pallas_call(
        functools.partial(_mm_body, nk=nk, epilogue=epilogue),
        grid=(M // tm, N // tn, nk),
        in_specs=in_specs,
        out_specs=pl.BlockSpec((tm, tn), lambda i, j, k: (i, j)),
        out_shape=jax.ShapeDtypeStruct((M, N), out_dtype),
        scratch_shapes=[pltpu.VMEM((tm, tn), F32)],
        compiler_params=_cparams(("parallel", "parallel", "arbitrary")),
    )(*args)


def _mm_ws_body(x_ref, w_ref, *rest, epilogue, w_transposed):
    if epilogue == "residual":
        res_ref, o_ref, wb_ref = rest
    else:
        o_ref, wb_ref = rest

    @pl.when(pl.program_id(1) == 0)
    def _():
        wb_ref[...] = w_ref[...].astype(BF16)

    contract = (((1,), (1,)), ((), ())) if w_transposed else (((1,), (0,)), ((), ()))
    acc = lax.dot_general(x_ref[...], wb_ref[...], contract, preferred_element_type=F32)
    if epilogue == "relu2":
        r = jnp.maximum(acc, 0.0)
        acc = r * r
    elif epilogue == "residual":
        acc = acc + res_ref[...]
    o_ref[...] = acc.astype(o_ref.dtype)


def matmul_ws(x, w3, layer, col0, ncols, out_dtype, tn=WS_COLS, epilogue="none", residual=None, tm=WS_ROWS,
              w_transposed=False):
    M, K = x.shape
    tm = min(tm, M)
    nj = ncols // tn
    jb = col0 // tn
    assert col0 % tn == 0 and ncols % tn == 0 and M % tm == 0
    if w_transposed:
        w_spec = pl.BlockSpec((None, tn, K), lambda j, i: (layer, jb + j, 0))
        wb_shape = (tn, K)
    else:
        w_spec = pl.BlockSpec((None, K, tn), lambda j, i: (layer, 0, jb + j))
        wb_shape = (K, tn)
    in_specs = [pl.BlockSpec((tm, K), lambda j, i: (i, 0)), w_spec]
    args = [x, w3]
    if epilogue == "residual":
        in_specs.append(pl.BlockSpec((tm, tn), lambda j, i: (i, j)))
        args.append(residual)
    return pl.pallas_call(
        functools.partial(_mm_ws_body, epilogue=epilogue, w_transposed=w_transposed),
        grid=(nj, M // tm),
        in_specs=in_specs,
        out_specs=pl.BlockSpec((tm, tn), lambda j, i: (i, j)),
        out_shape=jax.ShapeDtypeStruct((M, nj * tn), out_dtype),
        scratch_shapes=[pltpu.VMEM(wb_shape, BF16)],
        compiler_params=_cparams(("parallel", "arbitrary")),
    )(*args)


def _diffattn_body(q_ref, k_ref, v_ref, lam_ref, g_ref, o_ref, m_ref, acc_ref,
                   sa_ref, sb_ref, pa_ref, pb_ref, aa_ref, ab_ref, *, tq, lam_init):
    tk = tq
    i = pl.program_id(1)
    hw = 2 * A_HDIM
    q = q_ref[...] * (A_HDIM ** -0.5)
    lane = lax.broadcasted_iota(jnp.int32, (tq, hw), 1)
    zero = jnp.zeros_like(q)
    qz = jnp.concatenate([jnp.where(lane < A_HDIM, q, zero), jnp.where(lane >= A_HDIM, q, zero)], axis=0)
    m_ref[...] = jnp.full_like(m_ref, NEG)
    acc_ref[...] = jnp.zeros_like(acc_ref)
    ones = jnp.ones((tk, hw), BF16)

    def scores(j):
        k = k_ref[pl.ds(pl.multiple_of(j * tk, tk), tk), :]
        return lax.dot_general(qz, k, (((1,), (1,)), ((), ())), preferred_element_type=F32)

    key_minus_row = (lax.broadcasted_iota(jnp.int32, (SOFTMAX_ROWS, tk), 1)
                     - lax.broadcasted_iota(jnp.int32, (SOFTMAX_ROWS, tk), 0))
    tile_rows = lambda j: pl.ds(pl.multiple_of(j * tk, tk), tk)

    def scores(j, s_out):
        s_out[...] = lax.dot_general(qz, k_ref[tile_rows(j), :], (((1,), (1,)), ((), ())),
                                     preferred_element_type=F32)

    def softmax(j, s_in, p_out, alpha_out, masked):
        for r0 in range(0, 2 * tq, SOFTMAX_ROWS):
            rs = slice(r0, r0 + SOFTMAX_ROWS)
            m_old = m_ref[rs, :]
            if masked:
                causal = key_minus_row <= (i * tq + r0 % tq) - j * tk
                row_max = jnp.max(jnp.where(causal, s_in[rs, :], NEG), axis=-1, keepdims=True)
            else:
                row_max = jnp.max(s_in[rs, :], axis=-1, keepdims=True)
            m_new = jnp.maximum(m_old, row_max)
            m_ref[rs, :] = m_new
            e = jnp.exp(s_in[rs, :] - jnp.concatenate([m_new] * (tk // hw), axis=-1))
            p_out[rs, :] = (jnp.where(causal, e, 0.0) if masked else e).astype(BF16)
            alpha_out[rs, :] = jnp.exp(m_old - m_new)

    def accumulate(j, p_in, alpha_in):
        v_ext = jnp.concatenate([v_ref[tile_rows(j), :], ones], axis=-1)
        alpha = alpha_in[...]
        acc_ref[...] = (jnp.concatenate([alpha, alpha], axis=-1) * acc_ref[...]
                        + jnp.dot(p_in[...], v_ext, preferred_element_type=F32))

    scores(0, sa_ref)
    pb_ref[...] = jnp.zeros_like(pb_ref)
    ab_ref[...] = jnp.ones_like(ab_ref)

    def pair(t, masked):
        j = 2 * t
        accumulate(jnp.maximum(j - 1, 0), pb_ref, ab_ref)
        softmax(j, sa_ref, pa_ref, aa_ref, masked)
        scores(j + 1, sb_ref)
        accumulate(j, pa_ref, aa_ref)
        softmax(j + 1, sb_ref, pb_ref, ab_ref, masked)
        if not masked:
            scores(j + 2, sa_ref)

    nfree = i // 2
    lax.fori_loop(0, nfree, lambda t, c: (pair(t, False), c)[1], 0)

    @pl.when(i % 2 == 0)
    def _():
        accumulate(jnp.maximum(i - 1, 0), pb_ref, ab_ref)
        softmax(i, sa_ref, pa_ref, aa_ref, True)
        accumulate(i, pa_ref, aa_ref)

    @pl.when(i % 2 == 1)
    def _():
        pair(nfree, True)
        accumulate(i, pb_ref, ab_ref)

    lv = lam_ref[...]
    lam = (jnp.exp(jnp.sum(lv[0:1] * lv[1:2], axis=-1, keepdims=True))
           - jnp.exp(jnp.sum(lv[2:3] * lv[3:4], axis=-1, keepdims=True)) + lam_init)
    acc = acc_ref[...]
    on = acc[:, :hw] / acc[:, hw:]
    o = on[:tq] - lam * on[tq:]
    ms = jnp.mean(o * o, axis=-1, keepdims=True)
    o_ref[...] = (o * lax.rsqrt(ms + EPS) * g_ref[...] * (1.0 - lam_init)).astype(o_ref.dtype)


def diff_attention(qkv, lam_vecs, subln_g, lam_init, tq=ATTN_TILE):
    S = qkv.shape[0]
    tq = min(tq, S)
    hw = 2 * A_HDIM
    return pl.pallas_call(
        functools.partial(_diffattn_body, tq=tq, lam_init=lam_init),
        grid=(A_HEADS, S // tq),
        in_specs=[pl.BlockSpec((tq, hw), lambda h, i: (i, h)),
                  pl.BlockSpec((S, hw), lambda h, i: (0, A_HEADS + h)),
                  pl.BlockSpec((S, hw), lambda h, i: (0, 2 * A_HEADS + h)),
                  pl.BlockSpec((4, A_HDIM), lambda h, i: (0, 0)),
                  pl.BlockSpec((1, hw), lambda h, i: (0, 0))],
        out_specs=pl.BlockSpec((tq, hw), lambda h, i: (i, h)),
        out_shape=jax.ShapeDtypeStruct((S, A_WIDTH), BF16),
        scratch_shapes=[pltpu.VMEM((2 * tq, hw), F32), pltpu.VMEM((2 * tq, 2 * hw), F32),
                        pltpu.VMEM((2 * tq, tq), F32), pltpu.VMEM((2 * tq, tq), F32),
                        pltpu.VMEM((2 * tq, tq), BF16), pltpu.VMEM((2 * tq, tq), BF16),
                        pltpu.VMEM((2 * tq, hw), F32), pltpu.VMEM((2 * tq, hw), F32)],
        compiler_params=_cparams(("parallel", "parallel")),
    )(qkv, qkv, qkv, lam_vecs, subln_g.reshape(1, hw))


KEY_NEG_INF = -2139095041


def _float_key(s):
    b = pltpu.bitcast(s, jnp.int32)
    return b ^ ((b >> 31) & jnp.int32(0x7FFFFFFF))


def _dsa_index_body(ki_ref, qi_ref, wiT_ref, bias_ref, key_ref, *, qb, kc, topk, nkc_total):
    pos_bits = (nkc_total * kc).bit_length()
    i = pl.program_id(0)
    nch = ((i + 1) * qb) // kc
    wi = wiT_ref[...] * (IDX_HEADS ** -0.5 * IDX_HDIM ** -0.5)
    qi = qi_ref[...]
    tpos = i * qb + lax.broadcasted_iota(jnp.int32, (kc, qb), 1)
    srow = lax.broadcasted_iota(jnp.int32, (kc, qb), 0)

    def score_chunk(c, carry):
        r0 = pl.multiple_of(c * kc, kc)
        kic = ki_ref[pl.ds(r0, kc), :]
        acc = jnp.zeros((kc, qb), F32)
        for h in range(IDX_HEADS):
            d = lax.dot_general(kic, qi[:, h * IDX_HDIM:(h + 1) * IDX_HDIM],
                                (((1,), (1,)), ((), ())), preferred_element_type=F32)
            acc = acc + jnp.maximum(d, 0.0) * wi[h:h + 1, :]
        acc = jnp.where(r0 + srow <= tpos, acc, -jnp.inf)
        key_ref[pl.ds(r0, kc), :] = _float_key(acc)
        return carry

    lax.fori_loop(0, nch, score_chunk, 0)

    def count(pred):
        def body(c, cnt):
            r0 = pl.multiple_of(c * kc, kc)
            hit = pred(key_ref[pl.ds(r0, kc), :], r0 + srow).astype(jnp.int32)
            return cnt + jnp.sum(hit.reshape(kc // 8, 8, qb), axis=0)
        cnt8 = lax.fori_loop(0, nch, body, jnp.zeros((8, qb), jnp.int32))
        return jnp.sum(cnt8, axis=0, keepdims=True)

    def bit_step(it, carry):
        tau, n_ge = carry
        cand = tau + (jnp.int32(1) << (31 - it))
        cnt = count(lambda key, pos: key >= cand)
        ok = cnt >= topk
        return jnp.where(ok, cand, tau), jnp.where(ok, cnt, n_ge)

    tau, n_ge = lax.fori_loop(0, 32, bit_step, (jnp.full((1, qb), jnp.iinfo(jnp.int32).min, jnp.int32),
                                                jnp.zeros((1, qb), jnp.int32)))
    n_ge = jnp.where(tau <= KEY_NEG_INF, 0, n_ge)
    tau = jnp.maximum(tau, KEY_NEG_INF + 1)

    tie_rounds = (jnp.max(n_ge) > topk).astype(jnp.int32)
    n_gt = lax.fori_loop(0, tie_rounds, lambda _, c: count(lambda key, pos: key > tau), jnp.zeros((1, qb), jnp.int32))
    quota = topk - n_gt

    def pos_step(it, pos_end):
        cand = pos_end + (jnp.int32(1) << (pos_bits - 1 - it))
        n_tie = count(lambda key, pos: (key == tau) & (pos < cand))
        return jnp.where(n_tie <= quota, cand, pos_end)

    pos_all = jnp.int32((1 << pos_bits) - 1)
    pos_end = lax.fori_loop(0, tie_rounds * pos_bits, pos_step,
                            jnp.full((1, qb), 1, jnp.int32) * (pos_all * (1 - tie_rounds)))

    def write_chunk(c, carry):
        r0 = pl.multiple_of(c * kc, kc)
        key = key_ref[pl.ds(r0, kc), :]
        tie_kept = (key == tau) & (r0 + srow < pos_end)
        bias = jnp.where(key > tau, 0.0, jnp.where(tie_kept, 0.0, NEG))
        bias_ref[pl.ds(r0, kc), :] = bias.astype(bias_ref.dtype)
        return carry

    lax.fori_loop(0, nch, write_chunk, 0)

    def fill_chunk(c, carry):
        r0 = pl.multiple_of(c * kc, kc)
        bias_ref[pl.ds(r0, kc), :] = jnp.full((kc, qb), NEG, bias_ref.dtype)
        return carry

    lax.fori_loop(nch, nkc_total, fill_chunk, 0)


def dsa_index(ki, qi, wiT, topk, qb=INDEX_QUERIES, kc=INDEX_KEYS):
    S = ki.shape[0]
    qb = min(qb, S)
    kc = min(kc, qb)
    return pl.pallas_call(
        functools.partial(_dsa_index_body, qb=qb, kc=kc, topk=topk, nkc_total=S // kc),
        grid=(S // qb,),
        in_specs=[pl.BlockSpec((S, IDX_HDIM), lambda i: (0, 0)),
                  pl.BlockSpec((qb, IDX_HEADS * IDX_HDIM), lambda i: (i, 0)),
                  pl.BlockSpec((IDX_HEADS, qb), lambda i: (0, i))],
        out_specs=pl.BlockSpec((S, qb), lambda i: (0, i)),
        out_shape=jax.ShapeDtypeStruct((S, S), BF16),
        scratch_shapes=[pltpu.VMEM((S, qb), jnp.int32)],
        compiler_params=_cparams(("parallel",)),
    )(ki, qi, wiT)


def _dsa_attn_body(qi_ref, kj_ref, q_ref, k_ref, vT_ref, bias_ref, oT_ref, m_ref, l_ref, acc_ref, *, qb, kc):
    step = pl.program_id(0)
    i = qi_ref[step]
    j = kj_ref[step]
    scale = C_HDIM ** -0.5
    heads = [slice(h * C_HDIM, (h + 1) * C_HDIM) for h in range(C_HEADS)]

    @pl.when(j == 0)
    def _():
        m_ref[...] = jnp.full_like(m_ref, NEG)
        l_ref[...] = jnp.zeros_like(l_ref)
        acc_ref[...] = jnp.zeros_like(acc_ref)

    bias = bias_ref[...].astype(F32)
    sT = [lax.dot_general(k_ref[:, cs], q_ref[:, cs], (((1,), (1,)), ((), ())),
                          preferred_element_type=F32) * scale + bias for cs in heads]
    m_old = m_ref[...]
    m_new = jnp.maximum(m_old, jnp.concatenate([jnp.max(s, axis=0, keepdims=True) for s in sT], axis=0))
    alpha = jnp.exp(m_old - m_new)
    p = [jnp.exp(s - m_new[h:h + 1, :]) for h, s in enumerate(sT)]
    l_ref[...] = alpha * l_ref[...] + jnp.concatenate([jnp.sum(x, axis=0, keepdims=True) for x in p], axis=0)
    m_ref[...] = m_new
    for h, cs in enumerate(heads):
        acc_ref[cs, :] = alpha[h:h + 1, :] * acc_ref[cs, :] + jnp.dot(vT_ref[cs, :], p[h].astype(BF16),
                                                                       preferred_element_type=F32)

    @pl.when(j == ((i + 1) * qb - 1) // kc)
    def _():
        l = l_ref[...]
        for h, cs in enumerate(heads):
            oT_ref[cs, :] = (acc_ref[cs, :] / l[h:h + 1, :]).astype(oT_ref.dtype)


def dsa_attention(qkv, vT, bias, qb=DSA_QUERIES, kc=DSA_KEYS):
    S = qkv.shape[0]
    qb = min(qb, S)
    kc = min(kc, S)
    pairs = [(i, j) for i in range(S // qb) for j in range(((i + 1) * qb - 1) // kc + 1)]
    qi = jnp.asarray([p_[0] for p_ in pairs], jnp.int32)
    kj = jnp.asarray([p_[1] for p_ in pairs], jnp.int32)
    grid_spec = pltpu.PrefetchScalarGridSpec(
        num_scalar_prefetch=2,
        grid=(len(pairs),),
        in_specs=[pl.BlockSpec((qb, C_WIDTH), lambda s, qi, kj: (qi[s], 0)),
                  pl.BlockSpec((kc, C_WIDTH), lambda s, qi, kj: (kj[s], 1)),
                  pl.BlockSpec((C_WIDTH, kc), lambda s, qi, kj: (0, kj[s])),
                  pl.BlockSpec((kc, qb), lambda s, qi, kj: (kj[s], qi[s]))],
        out_specs=pl.BlockSpec((C_WIDTH, qb), lambda s, qi, kj: (0, qi[s])),
        scratch_shapes=[pltpu.VMEM((C_HEADS, qb), F32), pltpu.VMEM((C_HEADS, qb), F32),
                        pltpu.VMEM((C_WIDTH, qb), F32)])
    return pl.pallas_call(
        functools.partial(_dsa_attn_body, qb=qb, kc=kc),
        grid_spec=grid_spec,
        out_shape=jax.ShapeDtypeStruct((C_WIDTH, S), BF16),
        compiler_params=_cparams(("arbitrary",)),
    )(qi, kj, qkv, qkv, vT, bias)


def _sigmoid(x):
    return 1.0 / (1.0 + jnp.exp(-x))


def _split_bf16(x, terms):
    parts = []
    for _ in range(terms):
        p = x.astype(BF16)
        parts.append(p)
        x = x - p.astype(F32)
    return parts


def _dot_bf16x3(a, b):
    a_hi, a_lo = _split_bf16(a, 2)
    b_hi, b_lo = _split_bf16(b, 2)
    dot = lambda x, y: jnp.dot(x, y, preferred_element_type=F32)
    return dot(a_hi, b_hi) + (dot(a_hi, b_lo) + dot(a_lo, b_hi))


def _dot_exact_lhs(a, b):
    a = a.astype(BF16)
    b_hi, b_mid, b_lo = _split_bf16(b, 3)
    dot = lambda y: jnp.dot(a, y, preferred_element_type=F32)
    return dot(b_hi) + (dot(b_mid) + dot(b_lo))


def _shift_rows(cur, prev_ref, first_tile):
    prev_row = jnp.where(first_tile, 0.0, prev_ref[7:8, :].astype(F32))
    rolled = pltpu.roll(cur, 1, axis=0)
    row = lax.broadcasted_iota(jnp.int32, cur.shape, 0)
    return jnp.where(row == 0, prev_row, rolled)


def _rwkv_lora_body(*refs, has_v):
    if has_v:
        (h_ref, hp_ref, mu_ref, w1_ref, a1_ref, g1_ref, v1_ref, ow_ref, oa_ref, og_ref, ov_ref) = refs
    else:
        (h_ref, hp_ref, mu_ref, w1_ref, a1_ref, g1_ref, ow_ref, oa_ref, og_ref) = refs
    i = pl.program_id(0)
    h = h_ref[...].astype(F32)
    dh = _shift_rows(h, hp_ref, i == 0) - h

    def lora(row, w_ref):
        xm = (h + dh * mu_ref[row:row + 1, :]).astype(BF16)
        return jnp.dot(xm, w_ref[...], preferred_element_type=F32)

    ow_ref[...] = jnp.tanh(lora(0, w1_ref))
    oa_ref[...] = lora(1, a1_ref)
    og_ref[...] = _sigmoid(lora(2, g1_ref))
    if has_v:
        ov_ref[...] = lora(3, v1_ref)


def rwkv_lora(h, mu, w1, a1, g1, v1, tm=LORA_ROWS):
    S, D = h.shape
    tm = min(tm, S)
    has_v = v1 is not None
    ws = [w1, a1, g1] + ([v1] if has_v else [])
    full = lambda a: pl.BlockSpec(a.shape, lambda i: (0, 0))
    return pl.pallas_call(
        functools.partial(_rwkv_lora_body, has_v=has_v),
        grid=(S // tm,),
        in_specs=[pl.BlockSpec((tm, D), lambda i: (i, 0)),
                  pl.BlockSpec((8, D), lambda i: (jnp.maximum(i * (tm // 8) - 1, 0), 0)),
                  full(mu)] + [full(w) for w in ws],
        out_specs=[pl.BlockSpec((tm, w.shape[1]), lambda i: (i, 0)) for w in ws],
        out_shape=[jax.ShapeDtypeStruct((S, w.shape[1]), F32) for w in ws],
        compiler_params=_cparams(("parallel",)),
    )(h, h, mu, *ws)


(V_MU_R, V_MU_K, V_MU_V, V_W0, V_A0, V_KK, V_KA, V_RK, V_V0) = range(9)
N_VEC_ROWS = 16


def _wkv_pre_body(*refs, has_v, tt):
    if has_v:
        (r_ref, k_ref, v_ref, rp_ref, kp_ref, vp_ref, hw_ref, ha_ref, hg_ref, w2_ref, a2_ref, g2_ref,
         vec_ref, hv_ref, v2_ref, vf_ref,
         m_out, g_out, q_out, z_out, vout_ref, gate_ref, bonus_ref) = refs
    else:
        (r_ref, k_ref, v_ref, rp_ref, kp_ref, vp_ref, hw_ref, ha_ref, hg_ref, w2_ref, a2_ref, g2_ref,
         vec_ref,
         m_out, g_out, q_out, z_out, vout_ref, gate_ref, bonus_ref) = refs
    i = pl.program_id(0)
    first = i == 0
    C = WKV_CHUNK
    N = B_HDIM
    vec = lambda row: vec_ref[row:row + 1, :]

    r = r_ref[...]
    k = k_ref[...]
    v = v_ref[...]
    r = r + (_shift_rows(r, rp_ref, first) - r) * vec(V_MU_R)
    k = k + (_shift_rows(k, kp_ref, first) - k) * vec(V_MU_K)
    v = v + (_shift_rows(v, vp_ref, first) - v) * vec(V_MU_V)

    wl = vec(V_W0) + _dot_bf16x3(hw_ref[...], w2_ref[...])
    z = -wl
    softplus = jnp.maximum(z, 0.0) + jnp.log(1.0 + jnp.exp(-jnp.abs(z)))
    logw = -jnp.exp(-softplus - 0.5)
    bdot = lambda a_ref, b_ref: jnp.dot(a_ref[...].astype(BF16), b_ref[...].astype(BF16), preferred_element_type=F32)
    a_sig = _sigmoid(vec(V_A0) + bdot(ha_ref, a2_ref))
    gate = bdot(hg_ref, g2_ref)
    if has_v:
        mix = _sigmoid(vec(V_V0) + bdot(hv_ref, v2_ref))
        v = v + (vf_ref[...] - v) * mix
    vout_ref[...] = v
    gate_ref[...] = gate

    kk = k * vec(V_KK)
    k_new = k * (1.0 + (a_sig - 1.0) * vec(V_KA))
    rk = r * k_new * vec(V_RK)

    rowc = lax.broadcasted_iota(jnp.int32, (C, C), 0)
    colc = lax.broadcasted_iota(jnp.int32, (C, C), 1)
    tril_incl = (rowc >= colc).astype(F32)
    prow = lax.broadcasted_iota(jnp.int32, (2 * C, 2 * C), 0)
    pcol = lax.broadcasted_iota(jnp.int32, (2 * C, 2 * C), 1)
    same_head = (prow >= C) == (pcol >= C)
    dstep = jnp.where(prow >= C, prow - C, prow) - jnp.where(pcol >= C, pcol - C, pcol)
    strict_bd = jnp.where(same_head, dstep, -1) > 0
    incl_bd = jnp.where(same_head, dstep, -1) >= 0
    eye_p = (prow == pcol).astype(F32)
    head_ones = jnp.where(same_head, 1.0, 0.0).astype(BF16)

    def head_sum(x):
        hi = x.astype(BF16)
        lo = (x - hi.astype(F32)).astype(BF16)
        return (jnp.dot(hi, head_ones, preferred_element_type=F32)
                + jnp.dot(lo, head_ones, preferred_element_type=F32))

    kkn = kk / jnp.maximum(jnp.sqrt(head_sum(kk * kk)), 1e-12)
    bonus_ref[...] = head_sum(rk) * v
    a_scan = -kkn
    b_scan = kkn * a_sig

    head0 = lax.broadcasted_iota(jnp.int32, (C, LANES), 1) < N

    def stack2(x):
        return jnp.concatenate([jnp.where(head0, x, 0.0), jnp.where(head0, 0.0, x)], axis=0)

    ts_of = lambda c: slice(c * C, (c + 1) * C)
    st = []
    for c in range(tt // C):
        ts = ts_of(c)
        lw = logw[ts]
        cum = _dot_exact_lhs(tril_incl, lw)
        cum_last = cum[C - 1:C, :]
        e_neg = jnp.exp(-cum)
        e_end = jnp.exp(cum_last - cum)
        st.append(dict(At=stack2(a_scan[ts] * jnp.exp(cum - lw)), Rt=stack2(r[ts] * jnp.exp(cum)),
                       Bt=b_scan[ts] * e_neg, Kt=k_new[ts] * e_neg,
                       Bg=stack2(b_scan[ts] * e_end), Kg=stack2(k_new[ts] * e_end),
                       V=stack2(v[ts]).astype(BF16), gam=jnp.exp(cum_last)))
    for u in st:
        left = jnp.concatenate([u["At"], u["Rt"]], axis=0).astype(BF16)
        right = jnp.concatenate([u["Bt"], u["Bt"], u["Kt"], u["Kt"]], axis=0).astype(BF16)
        AA = lax.dot_general(left, right, (((1,), (1,)), ((), ())), preferred_element_type=F32)
        u["A_ab"] = jnp.where(strict_bd, AA[:2 * C, :2 * C], 0.0)
        u["A_ak"] = jnp.where(strict_bd, AA[:2 * C, 2 * C:], 0.0)
        u["A_r"] = jnp.concatenate([jnp.where(incl_bd, AA[2 * C:, :2 * C], 0.0),
                                    jnp.where(incl_bd, AA[2 * C:, 2 * C:], 0.0)], axis=-1).astype(BF16)
    for u in st:
        akv = jnp.dot(u["A_ak"].astype(BF16), u["V"], preferred_element_type=F32)
        u["XA"] = jnp.concatenate([u["At"] + pltpu.roll(akv, N, axis=1), u["A_ab"]], axis=-1)
    for step in range(6):
        for u in st:
            XA = u["XA"]
            Ap = XA[:, LANES:].astype(BF16)
            if step < 5:
                prod = jnp.dot(Ap, XA.astype(BF16), preferred_element_type=F32)
                u["XA"] = jnp.concatenate([XA[:, :LANES] + prod[:, :LANES], prod[:, LANES:]], axis=-1)
            else:
                u["X"] = XA[:, :LANES] + jnp.dot(Ap, XA[:, :LANES].astype(BF16), preferred_element_type=F32)
    own = (lax.broadcasted_iota(jnp.int32, (2 * C, LANES), 0) >= C) == (lax.broadcasted_iota(jnp.int32, (2 * C, LANES), 1) >= N)
    for u in st:
        p1 = jnp.where(own, u["X"], 0.0).astype(BF16)
        p2 = pltpu.roll(jnp.where(own, 0.0, u["X"]), N, axis=1).astype(BF16)
        upper = jnp.concatenate([p1, p2], axis=-1)
        lower = jnp.concatenate([jnp.zeros((2 * C, LANES), BF16), u["V"]], axis=-1)
        u["W2"] = jnp.concatenate([upper, lower], axis=0)
    for u in st:
        bk = jnp.concatenate([u["Bg"], u["Kg"]], axis=0).astype(BF16)
        u["MG"] = lax.dot_general(bk, u["W2"], (((0,), (0,)), ((), ())), preferred_element_type=F32)
    for u in st:
        u["QZ"] = jnp.dot(u["A_r"], u["W2"], preferred_element_type=F32)
    for c, u in enumerate(st):
        rows = slice(c * LANES, (c + 1) * LANES)
        m_out[rows, :] = (eye_p * u["gam"] + u["MG"][:, :LANES]).astype(m_out.dtype)
        g_out[rows, :] = u["MG"][:, LANES:].astype(g_out.dtype)
        qs = u["Rt"] + u["QZ"][:, :LANES]
        zs = u["QZ"][:, LANES:]
        q_out[ts_of(c), :] = (qs[:C] + qs[C:]).astype(q_out.dtype)
        z_out[ts_of(c), :] = zs[:C] + zs[C:]


def wkv_pre(proj_b, hids, w2s, vecs, v_first, tt=WKV_PRE_TOKENS):
    assert B_HDIM == WKV_CHUNK and 2 * B_HDIM == LANES
    S = proj_b.shape[0]
    tt = min(tt, S)
    has_v = v_first is not None
    nb = B_WIDTH // LANES
    nstate_rows = (tt // WKV_CHUNK) * LANES
    tok = lambda off: pl.BlockSpec((tt, LANES), lambda i, p: (i, off + p))
    prev = lambda off: pl.BlockSpec((8, LANES), lambda i, p: (jnp.maximum(i * (tt // 8) - 1, 0), off + p))
    hid = lambda a: pl.BlockSpec((tt, a.shape[1]), lambda i, p: (i, 0))
    wcol = lambda a: pl.BlockSpec((a.shape[0], LANES), lambda i, p: (0, p))
    in_specs = [tok(0), tok(nb), tok(2 * nb), prev(0), prev(nb), prev(2 * nb),
                hid(hids[0]), hid(hids[1]), hid(hids[2]), wcol(w2s[0]), wcol(w2s[1]), wcol(w2s[2]),
                pl.BlockSpec((N_VEC_ROWS, LANES), lambda i, p: (0, p))]
    args = [proj_b] * 6 + list(hids[:3]) + list(w2s[:3]) + [vecs]
    if has_v:
        in_specs += [hid(hids[3]), wcol(w2s[3]), tok(0)]
        args += [hids[3], w2s[3], v_first]
    state_shape = jax.ShapeDtypeStruct(((S // WKV_CHUNK) * LANES, B_WIDTH), BF16)
    out_tok = pl.BlockSpec((tt, LANES), lambda i, p: (i, p))
    out_st = pl.BlockSpec((nstate_rows, LANES), lambda i, p: (i, p))
    return pl.pallas_call(
        functools.partial(_wkv_pre_body, has_v=has_v, tt=tt),
        grid=(S // tt, nb),
        in_specs=in_specs,
        out_specs=[out_st, out_st, out_tok, out_tok, out_tok, out_tok, out_tok],
        out_shape=[state_shape, state_shape, jax.ShapeDtypeStruct((S, B_WIDTH), BF16)]
                  + [jax.ShapeDtypeStruct((S, B_WIDTH), F32)] * 4,
        compiler_params=_cparams(("parallel", "parallel")),
    )(*args)


def _wkv_scan_body(m_ref, g_ref, q_ref, z_ref, gate_ref, bonus_ref, ln_ref, o_ref, h_ref, *, tt):
    i = pl.program_id(0)
    C = WKV_CHUNK
    N = B_HDIM
    npair = B_WIDTH // LANES

    @pl.when(i == 0)
    def _():
        h_ref[...] = jnp.zeros_like(h_ref)

    row = lax.broadcasted_iota(jnp.int32, (LANES, LANES), 0)
    col = lax.broadcasted_iota(jnp.int32, (LANES, LANES), 1)
    head_avg = jnp.where((row // N) == (col // N), 1.0 / N, 0.0).astype(BF16)

    def head_mean(x):
        hi = x.astype(BF16)
        lo = (x - hi.astype(F32)).astype(BF16)
        return (jnp.dot(hi, head_avg, preferred_element_type=F32)
                + jnp.dot(lo, head_avg, preferred_element_type=F32))

    for c in range(tt // C):
        ts = slice(c * C, (c + 1) * C)
        ss = slice(c * LANES, (c + 1) * LANES)
        ys = []
        for p in range(npair):
            ps = slice(p * LANES, (p + 1) * LANES)
            Hb = h_ref[p].astype(BF16)
            ys.append(jnp.dot(q_ref[ts, ps], Hb, preferred_element_type=F32) + z_ref[ts, ps])
            h_ref[p] = jnp.dot(m_ref[ss, ps], Hb, preferred_element_type=F32) + g_ref[ss, ps].astype(F32)
        for p in range(npair):
            ps = slice(p * LANES, (p + 1) * LANES)
            yc = ys[p] - head_mean(ys[p])
            var = head_mean(yc * yc)
            yn = yc * lax.rsqrt(var + B_GN_EPS) * ln_ref[0:1, ps] + ln_ref[1:2, ps]
            o_ref[ts, ps] = ((yn + bonus_ref[ts, ps]) * gate_ref[ts, ps]).astype(o_ref.dtype)


def wkv_scan(m, g, q, z, gate, bonus, ln, tt=WKV_SCAN_TOKENS):
    S = q.shape[0]
    tt = min(tt, S)
    npair = B_WIDTH // LANES
    nstate_rows = (tt // WKV_CHUNK) * LANES
    tok = pl.BlockSpec((tt, B_WIDTH), lambda i: (i, 0))
    st = pl.BlockSpec((nstate_rows, B_WIDTH), lambda i: (i, 0))
    return pl.pallas_call(
        functools.partial(_wkv_scan_body, tt=tt),
        grid=(S // tt,),
        in_specs=[st, st, tok, tok, tok, tok, pl.BlockSpec((8, B_WIDTH), lambda i: (0, 0))],
        out_specs=tok,
        out_shape=jax.ShapeDtypeStruct((S, B_WIDTH), BF16),
        scratch_shapes=[pltpu.VMEM((npair, LANES, LANES), F32)],
        compiler_params=_cparams(("arbitrary",)),
    )(m, g, q, z, gate, bonus, ln)


def rwkv7_mixer(h, proj_b, p, v_first):
    has_v = v_first is not None
    mu = p["mu_wag"] if not has_v else jnp.concatenate([p["mu_wag"], p["v_mu"][None]], axis=0)
    hids = rwkv_lora(h, mu, p["w1"].astype(BF16), p["a1"].astype(BF16), p["g1"].astype(BF16),
                     p["v1"].astype(BF16) if has_v else None)
    rows = [p["mu_rkv"][0], p["mu_rkv"][1], p["mu_rkv"][2], p["w0"], p["a0"], p["k_k"], p["k_a"],
            p["r_k"].reshape(-1), p["v0"] if has_v else jnp.zeros((B_WIDTH,), F32)]
    vecs = jnp.concatenate([jnp.stack(rows), jnp.zeros((N_VEC_ROWS - len(rows), B_WIDTH), F32)], axis=0)
    w2s = [p["w2"], p["a2"], p["g2"]] + ([p["v2"]] if has_v else [])
    m, g, q, z, v_out, gate, bonus = wkv_pre(proj_b, hids, w2s, vecs, v_first)
    ln = jnp.concatenate([p["ln_w"][None], p["ln_b"][None], jnp.zeros((6, B_WIDTH), F32)], axis=0)
    return wkv_scan(m, g, q, z, gate, bonus, ln), v_out


def kernel(x, norm_mix_g, w_in, lam_q1, lam_k1, lam_q2, lam_k2, diff_subln_g, rw_mu_rkv, rw_mu_wag, rw_w0, rw_w1, rw_w2, rw_a0, rw_a1, rw_a2, rw_g1, rw_g2, rw_k_k, rw_k_a, rw_r_k, rw_ln_w, rw_ln_b, rw_v_mu, rw_v0, rw_v1, rw_v2, w_out, norm_ffn_g, w_up, w_down, norm_final_g):
    Bsz, S, D = x.shape
    depth = w_in.shape[0]
    topk = min(TOPK_MAX, S // 4)
    w_in_t = jnp.swapaxes(w_in, 1, 2)
    w_down_bf = w_down.astype(BF16)
    nq = IDX_HEADS * IDX_HDIM
    n_kw = IDX_HDIM + IDX_HEADS
    w_idx_kw = w_in_t[:, w_in.shape[2] - n_kw:, :]
    outs = []
    for b in range(Bsz):
        xb = x[b]
        v_first = None
        for l in range(depth):
            h = rmsnorm(xb, norm_mix_g[l], BF16)
            oA, oB, oC = 3 * A_WIDTH, 3 * A_WIDTH + 3 * B_WIDTH, 3 * A_WIDTH + 3 * B_WIDTH + 3 * C_WIDTH
            proj_a = matmul_ws(h, w_in_t, l, 0, oA, BF16, w_transposed=True)
            proj_b = matmul_ws(h, w_in_t, l, oA, oB - oA, F32, w_transposed=True)
            proj_c = matmul_ws(h, w_in_t, l, oB, oC - oB, BF16, w_transposed=True)
            qi = matmul_ws(h, w_in_t, l, oC, nq, BF16, w_transposed=True)
            proj_kw = matmul_ws(h, w_idx_kw, l, 0, n_kw, F32, tn=n_kw, w_transposed=True)

            lam_init = 0.8 - 0.6 * math.exp(-0.3 * l)
            lam_vecs = jnp.stack([lam_q1[l], lam_k1[l], lam_q2[l], lam_k2[l]])
            o_a = diff_attention(proj_a, lam_vecs, diff_subln_g[l], lam_init)

            p = dict(mu_rkv=rw_mu_rkv[l], mu_wag=rw_mu_wag[l], w0=rw_w0[l], w1=rw_w1[l], w2=rw_w2[l],
                     a0=rw_a0[l], a1=rw_a1[l], a2=rw_a2[l], g1=rw_g1[l], g2=rw_g2[l], k_k=rw_k_k[l],
                     k_a=rw_k_a[l], r_k=rw_r_k[l], ln_w=rw_ln_w[l], ln_b=rw_ln_b[l])
            if l > 0:
                p.update(v_mu=rw_v_mu[l - 1], v0=rw_v0[l - 1], v1=rw_v1[l - 1], v2=rw_v2[l - 1])
            o_b, v_out = rwkv7_mixer(h, proj_b, p, v_first if l > 0 else None)
            if l == 0:
                v_first = v_out

            ki = proj_kw[:, :IDX_HDIM].astype(BF16)
            wiT = proj_kw[:, IDX_HDIM:].T
            bias = dsa_index(ki, qi, wiT, topk)
            vT = proj_c[:, 2 * C_WIDTH:].T
            o_c = dsa_attention(proj_c, vT, bias).T

            mixed = jnp.concatenate([o_a, o_b, o_c], axis=-1)
            xb = matmul_ws(mixed, w_out, l, 0, D, F32, epilogue="residual", residual=xb)
            h2 = rmsnorm(xb, norm_ffn_g[l], BF16)
            up = matmul_ws(h2, w_up, l, 0, w_up.shape[2], BF16, tn=2 * WS_COLS, tm=WS_ROWS // 2, epilogue="relu2")
            xb = matmul(up, w_down_bf, l, F32, epilogue="residual", residual=xb)
        outs.append(rmsnorm(xb, norm_final_g, F32))
    return jnp.stack(outs)
```

```python
import functools
import math

import jax
import jax.numpy as jnp
from jax import lax
from jax.experimental import pallas as pl
from jax.experimental.pallas import tpu as pltpu

F32 = jnp.float32
BF16 = jnp.bfloat16

A_HEADS, A_HDIM = 8, 64
A_WIDTH = A_HEADS * 2 * A_HDIM
B_HDIM, B_WIDTH = 64, 2048
B_HEADS = B_WIDTH // B_HDIM
B_GN_EPS = 64e-5
C_HEADS, C_HDIM = 8, 128
C_WIDTH = C_HEADS * C_HDIM
IDX_HEADS, IDX_HDIM = 16, 64
TOPK_MAX = 256
EPS = 1e-6

LANES = 128
VMEM_LIMIT = 56 * 1024 * 1024
NEG = -1e30
WKV_CHUNK = 64
SOFTMAX_ROWS = 64

NORM_ROWS = 256
WS_ROWS, WS_COLS = 1024, 512
ATTN_TILE = 512
INDEX_QUERIES, INDEX_KEYS = 512, 512
DSA_QUERIES, DSA_KEYS = 512, 1024
LORA_ROWS = 256
WKV_PRE_TOKENS = 1024
WKV_SCAN_TOKENS = 256


def _cparams(sem):
    return pltpu.CompilerParams(dimension_semantics=sem, vmem_limit_bytes=VMEM_LIMIT)


def _rmsnorm_body(x_ref, g_ref, o_ref):
    x = x_ref[...]
    ms = jnp.mean(x * x, axis=-1, keepdims=True)
    o_ref[...] = (x * lax.rsqrt(ms + EPS) * g_ref[...]).astype(o_ref.dtype)


def rmsnorm(x, g, out_dtype, tm=NORM_ROWS):
    S, D = x.shape
    tm = min(tm, S)
    return pl.pallas_call(
        _rmsnorm_body,
        grid=(S // tm,),
        in_specs=[pl.BlockSpec((tm, D), lambda i: (i, 0)),
                  pl.BlockSpec((1, D), lambda i: (0, 0))],
        out_specs=pl.BlockSpec((tm, D), lambda i: (i, 0)),
        out_shape=jax.ShapeDtypeStruct((S, D), out_dtype),
        compiler_params=_cparams(("parallel",)),
    )(x, g.reshape(1, D))


def _mm_body(a_ref, b_ref, *rest, nk, epilogue):
    if epilogue == "residual":
        res_ref, o_ref, acc_ref = rest
    else:
        o_ref, acc_ref = rest
    k = pl.program_id(2)

    @pl.when(k == 0)
    def _():
        acc_ref[...] = jnp.zeros_like(acc_ref)

    acc_ref[...] += jnp.dot(a_ref[...], b_ref[...], preferred_element_type=F32)

    @pl.when(k == nk - 1)
    def _():
        acc = acc_ref[...]
        if epilogue == "relu2":
            r = jnp.maximum(acc, 0.0)
            acc = r * r
        elif epilogue == "residual":
            acc = acc + res_ref[...]
        o_ref[...] = acc.astype(o_ref.dtype)


def _pick(n, pref):
    for t in pref:
        if n % t == 0:
            return t
    return n


def matmul(a, b3, layer, out_dtype, epilogue="none", residual=None):
    M, K = a.shape
    _, _, N = b3.shape
    tm = _pick(M, (1024, 512, 256))
    tn = _pick(N, (1024, 768, 512, 384, 256, 128))
    tk = _pick(K, (2048, 1024, 512))
    nk = K // tk
    in_specs = [pl.BlockSpec((tm, tk), lambda i, j, k: (i, k)),
                pl.BlockSpec((None, tk, tn), lambda i, j, k: (layer, k, j))]
    args = [a, b3]
    if epilogue == "residual":
        in_specs.append(pl.BlockSpec((tm, tn), lambda i, j, k: (i, j)))
        args.append(residual)
    return pl.pallas_call(
        functools.partial(_mm_body, nk=nk, epilogue=epilogue),
        grid=(M // tm, N // tn, nk),
        in_specs=in_specs,
        out_specs=pl.BlockSpec((tm, tn), lambda i, j, k: (i, j)),
        out_shape=jax.ShapeDtypeStruct((M, N), out_dtype),
        scratch_shapes=[pltpu.VMEM((tm, tn), F32)],
        compiler_params=_cparams(("parallel", "parallel", "arbitrary")),
    )(*args)


def _mm_ws_body(x_ref, w_ref, *rest, epilogue, w_transposed):
    if epilogue == "residual":
        res_ref, o_ref, wb_ref = rest
    else:
        o_ref, wb_ref = rest

    @pl.when(pl.program_id(1) == 0)
    def _():
        wb_ref[...] = w_ref[...].astype(BF16)

    contract = (((1,), (1,)), ((), ())) if w_transposed else (((1,), (0,)), ((), ()))
    acc = lax.dot_general(x_ref[...], wb_ref[...], contract, preferred_element_type=F32)
    if epilogue == "relu2":
        r = jnp.maximum(acc, 0.0)
        acc = r * r
    elif epilogue == "residual":
        acc = acc + res_ref[...]
    o_ref[...] = acc.astype(o_ref.dtype)


def matmul_ws(x, w3, layer, col0, ncols, out_dtype, tn=WS_COLS, epilogue="none", residual=None, tm=WS_ROWS,
              w_transposed=False):
    M, K = x.shape
    tm = min(tm, M)
    nj = ncols // tn
    jb = col0 // tn
    assert col0 % tn == 0 and ncols % tn == 0 and M % tm == 0
    if w_transposed:
        w_spec = pl.BlockSpec((None, tn, K), lambda j, i: (layer, jb + j, 0))
        wb_shape = (tn, K)
    else:
        w_spec = pl.BlockSpec((None, K, tn), lambda j, i: (layer, 0, jb + j))
        wb_shape = (K, tn)
    in_specs = [pl.BlockSpec((tm, K), lambda j, i: (i, 0)), w_spec]
    args = [x, w3]
    if epilogue == "residual":
        in_specs.append(pl.BlockSpec((tm, tn), lambda j, i: (i, j)))
        args.append(residual)
    return pl.pallas_call(
        functools.partial(_mm_ws_body, epilogue=epilogue, w_transposed=w_transposed),
        grid=(nj, M // tm),
        in_specs=in_specs,
        out_specs=pl.BlockSpec((tm, tn), lambda j, i: (i, j)),
        out_shape=jax.ShapeDtypeStruct((M, nj * tn), out_dtype),
        scratch_shapes=[pltpu.VMEM(wb_shape, BF16)],
        compiler_params=_cparams(("parallel", "arbitrary")),
    )(*args)


def _diffattn_body(q_ref, k_ref, v_ref, lam_ref, g_ref, o_ref, m_ref, acc_ref,
                   sa_ref, sb_ref, pa_ref, pb_ref, aa_ref, ab_ref, *, tq, lam_init):
    tk = tq
    i = pl.program_id(1)
    hw = 2 * A_HDIM
    q = q_ref[...] * (A_HDIM ** -0.5)
    lane = lax.broadcasted_iota(jnp.int32, (tq, hw), 1)
    zero = jnp.zeros_like(q)
    qz = jnp.concatenate([jnp.where(lane < A_HDIM, q, zero), jnp.where(lane >= A_HDIM, q, zero)], axis=0)
    m_ref[...] = jnp.full_like(m_ref, NEG)
    acc_ref[...] = jnp.zeros_like(acc_ref)
    ones = jnp.ones((tk, hw), BF16)

    def scores(j):
        k = k_ref[pl.ds(pl.multiple_of(j * tk, tk), tk), :]
        return lax.dot_general(qz, k, (((1,), (1,)), ((), ())), preferred_element_type=F32)

    key_minus_row = (lax.broadcasted_iota(jnp.int32, (SOFTMAX_ROWS, tk), 1)
                     - lax.broadcasted_iota(jnp.int32, (SOFTMAX_ROWS, tk), 0))
    tile_rows = lambda j: pl.ds(pl.multiple_of(j * tk, tk), tk)

    def scores(j, s_out):
        s_out[...] = lax.dot_general(qz, k_ref[tile_rows(j), :], (((1,), (1,)), ((), ())),
                                     preferred_element_type=F32)

    def softmax(j, s_in, p_out, alpha_out, masked):
        for r0 in range(0, 2 * tq, SOFTMAX_ROWS):
            rs = slice(r0, r0 + SOFTMAX_ROWS)
            m_old = m_ref[rs, :]
            if masked:
                causal = key_minus_row <= (i * tq + r0 % tq) - j * tk
                row_max = jnp.max(jnp.where(causal, s_in[rs, :], NEG), axis=-1, keepdims=True)
            else:
                row_max = jnp.max(s_in[rs, :], axis=-1, keepdims=True)
            m_new = jnp.maximum(m_old, row_max)
            m_ref[rs, :] = m_new
            e = jnp.exp(s_in[rs, :] - jnp.concatenate([m_new] * (tk // hw), axis=-1))
            p_out[rs, :] = (jnp.where(causal, e, 0.0) if masked else e).astype(BF16)
            alpha_out[rs, :] = jnp.exp(m_old - m_new)

    def accumulate(j, p_in, alpha_in):
        v_ext = jnp.concatenate([v_ref[tile_rows(j), :], ones], axis=-1)
        alpha = alpha_in[...]
        acc_ref[...] = (jnp.concatenate([alpha, alpha], axis=-1) * acc_ref[...]
                        + jnp.dot(p_in[...], v_ext, preferred_element_type=F32))

    scores(0, sa_ref)
    pb_ref[...] = jnp.zeros_like(pb_ref)
    ab_ref[...] = jnp.ones_like(ab_ref)

    def pair(t, masked):
        j = 2 * t
        accumulate(jnp.maximum(j - 1, 0), pb_ref, ab_ref)
        softmax(j, sa_ref, pa_ref, aa_ref, masked)
        scores(j + 1, sb_ref)
        accumulate(j, pa_ref, aa_ref)
        softmax(j + 1, sb_ref, pb_ref, ab_ref, masked)
        if not masked:
            scores(j + 2, sa_ref)

    nfree = i // 2
    lax.fori_loop(0, nfree, lambda t, c: (pair(t, False), c)[1], 0)

    @pl.when(i % 2 == 0)
    def _():
        accumulate(jnp.maximum(i - 1, 0), pb_ref, ab_ref)
        softmax(i, sa_ref, pa_ref, aa_ref, True)
        accumulate(i, pa_ref, aa_ref)

    @pl.when(i % 2 == 1)
    def _():
        pair(nfree, True)
        accumulate(i, pb_ref, ab_ref)

    lv = lam_ref[...]
    lam = (jnp.exp(jnp.sum(lv[0:1] * lv[1:2], axis=-1, keepdims=True))
           - jnp.exp(jnp.sum(lv[2:3] * lv[3:4], axis=-1, keepdims=True)) + lam_init)
    acc = acc_ref[...]
    on = acc[:, :hw] / acc[:, hw:]
    o = on[:tq] - lam * on[tq:]
    ms = jnp.mean(o * o, axis=-1, keepdims=True)
    o_ref[...] = (o * lax.rsqrt(ms + EPS) * g_ref[...] * (1.0 - lam_init)).astype(o_ref.dtype)


def diff_attention(qkv, lam_vecs, subln_g, lam_init, tq=ATTN_TILE):
    S = qkv.shape[0]
    tq = min(tq, S)
    hw = 2 * A_HDIM
    return pl.pallas_call(
        functools.partial(_diffattn_body, tq=tq, lam_init=lam_init),
        grid=(A_HEADS, S // tq),
        in_specs=[pl.BlockSpec((tq, hw), lambda h, i: (i, h)),
                  pl.BlockSpec((S, hw), lambda h, i: (0, A_HEADS + h)),
                  pl.BlockSpec((S, hw), lambda h, i: (0, 2 * A_HEADS + h)),
                  pl.BlockSpec((4, A_HDIM), lambda h, i: (0, 0)),
                  pl.BlockSpec((1, hw), lambda h, i: (0, 0))],
        out_specs=pl.BlockSpec((tq, hw), lambda h, i: (i, h)),
        out_shape=jax.ShapeDtypeStruct((S, A_WIDTH), BF16),
        scratch_shapes=[pltpu.VMEM((2 * tq, hw), F32), pltpu.VMEM((2 * tq, 2 * hw), F32),
                        pltpu.VMEM((2 * tq, tq), F32), pltpu.VMEM((2 * tq, tq), F32),
                        pltpu.VMEM((2 * tq, tq), BF16), pltpu.VMEM((2 * tq, tq), BF16),
                        pltpu.VMEM((2 * tq, hw), F32), pltpu.VMEM((2 * tq, hw), F32)],
        compiler_params=_cparams(("parallel", "parallel")),
    )(qkv, qkv, qkv, lam_vecs, subln_g.reshape(1, hw))


KEY_NEG_INF = -2139095041


def _float_key(s):
    b = pltpu.bitcast(s, jnp.int32)
    return b ^ ((b >> 31) & jnp.int32(0x7FFFFFFF))


def _dsa_index_body(ki_ref, qi_ref, wiT_ref, bias_ref, key_ref, *, qb, kc, topk, nkc_total):
    pos_bits = (nkc_total * kc).bit_length()
    i = pl.program_id(0)
    nch = ((i + 1) * qb) // kc
    wi = wiT_ref[...] * (IDX_HEADS ** -0.5 * IDX_HDIM ** -0.5)
    qi = qi_ref[...]
    tpos = i * qb + lax.broadcasted_iota(jnp.int32, (kc, qb), 1)
    srow = lax.broadcasted_iota(jnp.int32, (kc, qb), 0)

    def score_chunk(c, carry):
        r0 = pl.multiple_of(c * kc, kc)
        kic = ki_ref[pl.ds(r0, kc), :]
        acc = jnp.zeros((kc, qb), F32)
        for h in range(IDX_HEADS):
            d = lax.dot_general(kic, qi[:, h * IDX_HDIM:(h + 1) * IDX_HDIM],
                                (((1,), (1,)), ((), ())), preferred_element_type=F32)
            acc = acc + jnp.maximum(d, 0.0) * wi[h:h + 1, :]
        acc = jnp.where(r0 + srow <= tpos, acc, -jnp.inf)
        key_ref[pl.ds(r0, kc), :] = _float_key(acc)
        return carry

    lax.fori_loop(0, nch, score_chunk, 0)

    def count(pred):
        def body(c, cnt):
            r0 = pl.multiple_of(c * kc, kc)
            hit = pred(key_ref[pl.ds(r0, kc), :], r0 + srow).astype(jnp.int32)
            return cnt + jnp.sum(hit.reshape(kc // 8, 8, qb), axis=0)
        cnt8 = lax.fori_loop(0, nch, body, jnp.zeros((8, qb), jnp.int32))
        return jnp.sum(cnt8, axis=0, keepdims=True)

    def bit_step(it, carry):
        tau, n_ge = carry
        cand = tau + (jnp.int32(1) << (31 - it))
        cnt = count(lambda key, pos: key >= cand)
        ok = cnt >= topk
        return jnp.where(ok, cand, tau), jnp.where(ok, cnt, n_ge)

    tau, n_ge = lax.fori_loop(0, 32, bit_step, (jnp.full((1, qb), jnp.iinfo(jnp.int32).min, jnp.int32),
                                                jnp.zeros((1, qb), jnp.int32)))
    n_ge = jnp.where(tau <= KEY_NEG_INF, 0, n_ge)
    tau = jnp.maximum(tau, KEY_NEG_INF + 1)

    tie_rounds = (jnp.max(n_ge) > topk).astype(jnp.int32)
    n_gt = lax.fori_loop(0, tie_rounds, lambda _, c: count(lambda key, pos: key > tau), jnp.zeros((1, qb), jnp.int32))
    quota = topk - n_gt

    def pos_step(it, pos_end):
        cand = pos_end + (jnp.int32(1) << (pos_bits - 1 - it))
        n_tie = count(lambda key, pos: (key == tau) & (pos < cand))
        return jnp.where(n_tie <= quota, cand, pos_end)

    pos_all = jnp.int32((1 << pos_bits) - 1)
    pos_end = lax.fori_loop(0, tie_rounds * pos_bits, pos_step,
                            jnp.full((1, qb), 1, jnp.int32) * (pos_all * (1 - tie_rounds)))

    def write_chunk(c, carry):
        r0 = pl.multiple_of(c * kc, kc)
        key = key_ref[pl.ds(r0, kc), :]
        tie_kept = (key == tau) & (r0 + srow < pos_end)
        bias = jnp.where(key > tau, 0.0, jnp.where(tie_kept, 0.0, NEG))
        bias_ref[pl.ds(r0, kc), :] = bias.astype(bias_ref.dtype)
        return carry

    lax.fori_loop(0, nch, write_chunk, 0)

    def fill_chunk(c, carry):
        r0 = pl.multiple_of(c * kc, kc)
        bias_ref[pl.ds(r0, kc), :] = jnp.full((kc, qb), NEG, bias_ref.dtype)
        return carry

    lax.fori_loop(nch, nkc_total, fill_chunk, 0)


def dsa_index(ki, qi, wiT, topk, qb=INDEX_QUERIES, kc=INDEX_KEYS):
    S = ki.shape[0]
    qb = min(qb, S)
    kc = min(kc, qb)
    return pl.pallas_call(
        functools.partial(_dsa_index_body, qb=qb, kc=kc, topk=topk, nkc_total=S // kc),
        grid=(S // qb,),
        in_specs=[pl.BlockSpec((S, IDX_HDIM), lambda i: (0, 0)),
                  pl.BlockSpec((qb, IDX_HEADS * IDX_HDIM), lambda i: (i, 0)),
                  pl.BlockSpec((IDX_HEADS, qb), lambda i: (0, i))],
        out_specs=pl.BlockSpec((S, qb), lambda i: (0, i)),
        out_shape=jax.ShapeDtypeStruct((S, S), BF16),
        scratch_shapes=[pltpu.VMEM((S, qb), jnp.int32)],
        compiler_params=_cparams(("parallel",)),
    )(ki, qi, wiT)


def _dsa_attn_body(qi_ref, kj_ref, q_ref, k_ref, vT_ref, bias_ref, oT_ref, m_ref, l_ref, acc_ref, *, qb, kc):
    step = pl.program_id(0)
    i = qi_ref[step]
    j = kj_ref[step]
    scale = C_HDIM ** -0.5
    heads = [slice(h * C_HDIM, (h + 1) * C_HDIM) for h in range(C_HEADS)]

    @pl.when(j == 0)
    def _():
        m_ref[...] = jnp.full_like(m_ref, NEG)
        l_ref[...] = jnp.zeros_like(l_ref)
        acc_ref[...] = jnp.zeros_like(acc_ref)

    bias = bias_ref[...].astype(F32)
    sT = [lax.dot_general(k_ref[:, cs], q_ref[:, cs], (((1,), (1,)), ((), ())),
                          preferred_element_type=F32) * scale + bias for cs in heads]
    m_old = m_ref[...]
    m_new = jnp.maximum(m_old, jnp.concatenate([jnp.max(s, axis=0, keepdims=True) for s in sT], axis=0))
    alpha = jnp.exp(m_old - m_new)
    p = [jnp.exp(s - m_new[h:h + 1, :]) for h, s in enumerate(sT)]
    l_ref[...] = alpha * l_ref[...] + jnp.concatenate([jnp.sum(x, axis=0, keepdims=True) for x in p], axis=0)
    m_ref[...] = m_new
    for h, cs in enumerate(heads):
        acc_ref[cs, :] = alpha[h:h + 1, :] * acc_ref[cs, :] + jnp.dot(vT_ref[cs, :], p[h].astype(BF16),
                                                                       preferred_element_type=F32)

    @pl.when(j == ((i + 1) * qb - 1) // kc)
    def _():
        l = l_ref[...]
        for h, cs in enumerate(heads):
            oT_ref[cs, :] = (acc_ref[cs, :] / l[h:h + 1, :]).astype(oT_ref.dtype)


def dsa_attention(qkv, vT, bias, qb=DSA_QUERIES, kc=DSA_KEYS):
    S = qkv.shape[0]
    qb = min(qb, S)
    kc = min(kc, S)
    pairs = [(i, j) for i in range(S // qb) for j in range(((i + 1) * qb - 1) // kc + 1)]
    qi = jnp.asarray([p_[0] for p_ in pairs], jnp.int32)
    kj = jnp.asarray([p_[1] for p_ in pairs], jnp.int32)
    grid_spec = pltpu.PrefetchScalarGridSpec(
        num_scalar_prefetch=2,
        grid=(len(pairs),),
        in_specs=[pl.BlockSpec((qb, C_WIDTH), lambda s, qi, kj: (qi[s], 0)),
                  pl.BlockSpec((kc, C_WIDTH), lambda s, qi, kj: (kj[s], 1)),
                  pl.BlockSpec((C_WIDTH, kc), lambda s, qi, kj: (0, kj[s])),
                  pl.BlockSpec((kc, qb), lambda s, qi, kj: (kj[s], qi[s]))],
        out_specs=pl.BlockSpec((C_WIDTH, qb), lambda s, qi, kj: (0, qi[s])),
        scratch_shapes=[pltpu.VMEM((C_HEADS, qb), F32), pltpu.VMEM((C_HEADS, qb), F32),
                        pltpu.VMEM((C_WIDTH, qb), F32)])
    return pl.pallas_call(
        functools.partial(_dsa_attn_body, qb=qb, kc=kc),
        grid_spec=grid_spec,
        out_shape=jax.ShapeDtypeStruct((C_WIDTH, S), BF16),
        compiler_params=_cparams(("arbitrary",)),
    )(qi, kj, qkv, qkv, vT, bias)


def _sigmoid(x):
    return 1.0 / (1.0 + jnp.exp(-x))


def _split_bf16(x, terms):
    parts = []
    for _ in range(terms):
        p = x.astype(BF16)
        parts.append(p)
        x = x - p.astype(F32)
    return parts


def _dot_bf16x3(a, b):
    a_hi, a_lo = _split_bf16(a, 2)
    b_hi, b_lo = _split_bf16(b, 2)
    dot = lambda x, y: jnp.dot(x, y, preferred_element_type=F32)
    return dot(a_hi, b_hi) + (dot(a_hi, b_lo) + dot(a_lo, b_hi))


def _dot_exact_lhs(a, b):
    a = a.astype(BF16)
    b_hi, b_mid, b_lo = _split_bf16(b, 3)
    dot = lambda y: jnp.dot(a, y, preferred_element_type=F32)
    return dot(b_hi) + (dot(b_mid) + dot(b_lo))


def _shift_rows(cur, prev_ref, first_tile):
    prev_row = jnp.where(first_tile, 0.0, prev_ref[7:8, :].astype(F32))
    rolled = pltpu.roll(cur, 1, axis=0)
    row = lax.broadcasted_iota(jnp.int32, cur.shape, 0)
    return jnp.where(row == 0, prev_row, rolled)


def _rwkv_lora_body(*refs, has_v):
    if has_v:
        (h_ref, hp_ref, mu_ref, w1_ref, a1_ref, g1_ref, v1_ref, ow_ref, oa_ref, og_ref, ov_ref) = refs
    else:
        (h_ref, hp_ref, mu_ref, w1_ref, a1_ref, g1_ref, ow_ref, oa_ref, og_ref) = refs
    i = pl.program_id(0)
    h = h_ref[...].astype(F32)
    dh = _shift_rows(h, hp_ref, i == 0) - h

    def lora(row, w_ref):
        xm = (h + dh * mu_ref[row:row + 1, :]).astype(BF16)
        return jnp.dot(xm, w_ref[...], preferred_element_type=F32)

    ow_ref[...] = jnp.tanh(lora(0, w1_ref))
    oa_ref[...] = lora(1, a1_ref)
    og_ref[...] = _sigmoid(lora(2, g1_ref))
    if has_v:
        ov_ref[...] = lora(3, v1_ref)


def rwkv_lora(h, mu, w1, a1, g1, v1, tm=LORA_ROWS):
    S, D = h.shape
    tm = min(tm, S)
    has_v = v1 is not None
    ws = [w1, a1, g1] + ([v1] if has_v else [])
    full = lambda a: pl.BlockSpec(a.shape, lambda i: (0, 0))
    return pl.pallas_call(
        functools.partial(_rwkv_lora_body, has_v=has_v),
        grid=(S // tm,),
        in_specs=[pl.BlockSpec((tm, D), lambda i: (i, 0)),
                  pl.BlockSpec((8, D), lambda i: (jnp.maximum(i * (tm // 8) - 1, 0), 0)),
                  full(mu)] + [full(w) for w in ws],
        out_specs=[pl.BlockSpec((tm, w.shape[1]), lambda i: (i, 0)) for w in ws],
        out_shape=[jax.ShapeDtypeStruct((S, w.shape[1]), F32) for w in ws],
        compiler_params=_cparams(("parallel",)),
    )(h, h, mu, *ws)


(V_MU_R, V_MU_K, V_MU_V, V_W0, V_A0, V_KK, V_KA, V_RK, V_V0) = range(9)
N_VEC_ROWS = 16


def _wkv_pre_body(*refs, has_v, tt):
    if has_v:
        (r_ref, k_ref, v_ref, rp_ref, kp_ref, vp_ref, hw_ref, ha_ref, hg_ref, w2_ref, a2_ref, g2_ref,
         vec_ref, hv_ref, v2_ref, vf_ref,
         m_out, g_out, q_out, z_out, vout_ref, gate_ref, bonus_ref) = refs
    else:
        (r_ref, k_ref, v_ref, rp_ref, kp_ref, vp_ref, hw_ref, ha_ref, hg_ref, w2_ref, a2_ref, g2_ref,
         vec_ref,
         m_out, g_out, q_out, z_out, vout_ref, gate_ref, bonus_ref) = refs
    i = pl.program_id(0)
    first = i == 0
    C = WKV_CHUNK
    N = B_HDIM
    vec = lambda row: vec_ref[row:row + 1, :]

    r = r_ref[...]
    k = k_ref[...]
    v = v_ref[...]
    r = r + (_shift_rows(r, rp_ref, first) - r) * vec(V_MU_R)
    k = k + (_shift_rows(k, kp_ref, first) - k) * vec(V_MU_K)
    v = v + (_shift_rows(v, vp_ref, first) - v) * vec(V_MU_V)

    wl = vec(V_W0) + _dot_bf16x3(hw_ref[...], w2_ref[...])
    z = -wl
    softplus = jnp.maximum(z, 0.0) + jnp.log(1.0 + jnp.exp(-jnp.abs(z)))
    logw = -jnp.exp(-softplus - 0.5)
    bdot = lambda a_ref, b_ref: jnp.dot(a_ref[...].astype(BF16), b_ref[...].astype(BF16), preferred_element_type=F32)
    a_sig = _sigmoid(vec(V_A0) + bdot(ha_ref, a2_ref))
    gate = bdot(hg_ref, g2_ref)
    if has_v:
        mix = _sigmoid(vec(V_V0) + bdot(hv_ref, v2_ref))
        v = v + (vf_ref[...] - v) * mix
    vout_ref[...] = v
    gate_ref[...] = gate

    kk = k * vec(V_KK)
    k_new = k * (1.0 + (a_sig - 1.0) * vec(V_KA))
    rk = r * k_new * vec(V_RK)

    rowc = lax.broadcasted_iota(jnp.int32, (C, C), 0)
    colc = lax.broadcasted_iota(jnp.int32, (C, C), 1)
    tril_incl = (rowc >= colc).astype(F32)
    prow = lax.broadcasted_iota(jnp.int32, (2 * C, 2 * C), 0)
    pcol = lax.broadcasted_iota(jnp.int32, (2 * C, 2 * C), 1)
    same_head = (prow >= C) == (pcol >= C)
    dstep = jnp.where(prow >= C, prow - C, prow) - jnp.where(pcol >= C, pcol - C, pcol)
    strict_bd = jnp.where(same_head, dstep, -1) > 0
    incl_bd = jnp.where(same_head, dstep, -1) >= 0
    eye_p = (prow == pcol).astype(F32)
    head_ones = jnp.where(same_head, 1.0, 0.0).astype(BF16)

    def head_sum(x):
        hi = x.astype(BF16)
        lo = (x - hi.astype(F32)).astype(BF16)
        return (jnp.dot(hi, head_ones, preferred_element_type=F32)
                + jnp.dot(lo, head_ones, preferred_element_type=F32))

    kkn = kk / jnp.maximum(jnp.sqrt(head_sum(kk * kk)), 1e-12)
    bonus_ref[...] = head_sum(rk) * v
    a_scan = -kkn
    b_scan = kkn * a_sig

    head0 = lax.broadcasted_iota(jnp.int32, (C, LANES), 1) < N

    def stack2(x):
        return jnp.concatenate([jnp.where(head0, x, 0.0), jnp.where(head0, 0.0, x)], axis=0)

    ts_of = lambda c: slice(c * C, (c + 1) * C)
    st = []
    for c in range(tt // C):
        ts = ts_of(c)
        lw = logw[ts]
        cum = _dot_exact_lhs(tril_incl, lw)
        cum_last = cum[C - 1:C, :]
        e_neg = jnp.exp(-cum)
        e_end = jnp.exp(cum_last - cum)
        st.append(dict(At=stack2(a_scan[ts] * jnp.exp(cum - lw)), Rt=stack2(r[ts] * jnp.exp(cum)),
                       Bt=b_scan[ts] * e_neg, Kt=k_new[ts] * e_neg,
                       Bg=stack2(b_scan[ts] * e_end), Kg=stack2(k_new[ts] * e_end),
                       V=stack2(v[ts]).astype(BF16), gam=jnp.exp(cum_last)))
    for u in st:
        left = jnp.concatenate([u["At"], u["Rt"]], axis=0).astype(BF16)
        right = jnp.concatenate([u["Bt"], u["Bt"], u["Kt"], u["Kt"]], axis=0).astype(BF16)
        AA = lax.dot_general(left, right, (((1,), (1,)), ((), ())), preferred_element_type=F32)
        u["A_ab"] = jnp.where(strict_bd, AA[:2 * C, :2 * C], 0.0)
        u["A_ak"] = jnp.where(strict_bd, AA[:2 * C, 2 * C:], 0.0)
        u["A_r"] = jnp.concatenate([jnp.where(incl_bd, AA[2 * C:, :2 * C], 0.0),
                                    jnp.where(incl_bd, AA[2 * C:, 2 * C:], 0.0)], axis=-1).astype(BF16)
    for u in st:
        akv = jnp.dot(u["A_ak"].astype(BF16), u["V"], preferred_element_type=F32)
        u["XA"] = jnp.concatenate([u["At"] + pltpu.roll(akv, N, axis=1), u["A_ab"]], axis=-1)
    for step in range(6):
        for u in st:
            XA = u["XA"]
            Ap = XA[:, LANES:].astype(BF16)
            if step < 5:
                prod = jnp.dot(Ap, XA.astype(BF16), preferred_element_type=F32)
                u["XA"] = jnp.concatenate([XA[:, :LANES] + prod[:, :LANES], prod[:, LANES:]], axis=-1)
            else:
                u["X"] = XA[:, :LANES] + jnp.dot(Ap, XA[:, :LANES].astype(BF16), preferred_element_type=F32)
    own = (lax.broadcasted_iota(jnp.int32, (2 * C, LANES), 0) >= C) == (lax.broadcasted_iota(jnp.int32, (2 * C, LANES), 1) >= N)
    for u in st:
        p1 = jnp.where(own, u["X"], 0.0).astype(BF16)
        p2 = pltpu.roll(jnp.where(own, 0.0, u["X"]), N, axis=1).astype(BF16)
        upper = jnp.concatenate([p1, p2], axis=-1)
        lower = jnp.concatenate([jnp.zeros((2 * C, LANES), BF16), u["V"]], axis=-1)
        u["W2"] = jnp.concatenate([upper, lower], axis=0)
    for u in st:
        bk = jnp.concatenate([u["Bg"], u["Kg"]], axis=0).astype(BF16)
        u["MG"] = lax.dot_general(bk, u["W2"], (((0,), (0,)), ((), ())), preferred_element_type=F32)
    for u in st:
        u["QZ"] = jnp.dot(u["A_r"], u["W2"], preferred_element_type=F32)
    for c, u in enumerate(st):
        rows = slice(c * LANES, (c + 1) * LANES)
        m_out[rows, :] = (eye_p * u["gam"] + u["MG"][:, :LANES]).astype(m_out.dtype)
        g_out[rows, :] = u["MG"][:, LANES:].astype(g_out.dtype)
        qs = u["Rt"] + u["QZ"][:, :LANES]
        zs = u["QZ"][:, LANES:]
        q_out[ts_of(c), :] = (qs[:C] + qs[C:]).astype(q_out.dtype)
        z_out[ts_of(c), :] = zs[:C] + zs[C:]


def wkv_pre(proj_b, hids, w2s, vecs, v_first, tt=WKV_PRE_TOKENS):
    assert B_HDIM == WKV_CHUNK and 2 * B_HDIM == LANES
    S = proj_b.shape[0]
    tt = min(tt, S)
    has_v = v_first is not None
    nb = B_WIDTH // LANES
    nstate_rows = (tt // WKV_CHUNK) * LANES
    tok = lambda off: pl.BlockSpec((tt, LANES), lambda i, p: (i, off + p))
    prev = lambda off: pl.BlockSpec((8, LANES), lambda i, p: (jnp.maximum(i * (tt // 8) - 1, 0), off + p))
    hid = lambda a: pl.BlockSpec((tt, a.shape[1]), lambda i, p: (i, 0))
    wcol = lambda a: pl.BlockSpec((a.shape[0], LANES), lambda i, p: (0, p))
    in_specs = [tok(0), tok(nb), tok(2 * nb), prev(0), prev(nb), prev(2 * nb),
                hid(hids[0]), hid(hids[1]), hid(hids[2]), wcol(w2s[0]), wcol(w2s[1]), wcol(w2s[2]),
                pl.BlockSpec((N_VEC_ROWS, LANES), lambda i, p: (0, p))]
    args = [proj_b] * 6 + list(hids[:3]) + list(w2s[:3]) + [vecs]
    if has_v:
        in_specs += [hid(hids[3]), wcol(w2s[3]), tok(0)]
        args += [hids[3], w2s[3], v_first]
    state_shape = jax.ShapeDtypeStruct(((S // WKV_CHUNK) * LANES, B_WIDTH), BF16)
    out_tok = pl.BlockSpec((tt, LANES), lambda i, p: (i, p))
    out_st = pl.BlockSpec((nstate_rows, LANES), lambda i, p: (i, p))
    return pl.pallas_call(
        functools.partial(_wkv_pre_body, has_v=has_v, tt=tt),
        grid=(S // tt, nb),
        in_specs=in_specs,
        out_specs=[out_st, out_st, out_tok, out_tok, out_tok, out_tok, out_tok],
        out_shape=[state_shape, state_shape, jax.ShapeDtypeStruct((S, B_WIDTH), BF16)]
                  + [jax.ShapeDtypeStruct((S, B_WIDTH), F32)] * 4,
        compiler_params=_cparams(("parallel", "parallel")),
    )(*args)


def _wkv_scan_body(m_ref, g_ref, q_ref, z_ref, gate_ref, bonus_ref, ln_ref, o_ref, h_ref, *, tt):
    i = pl.program_id(0)
    C = WKV_CHUNK
    N = B_HDIM
    npair = B_WIDTH // LANES

    @pl.when(i == 0)
    def _():
        h_ref[...] = jnp.zeros_like(h_ref)

    row = lax.broadcasted_iota(jnp.int32, (LANES, LANES), 0)
    col = lax.broadcasted_iota(jnp.int32, (LANES, LANES), 1)
    head_avg = jnp.where((row // N) == (col // N), 1.0 / N, 0.0).astype(BF16)

    def head_mean(x):
        hi = x.astype(BF16)
        lo = (x - hi.astype(F32)).astype(BF16)
        return (jnp.dot(hi, head_avg, preferred_element_type=F32)
                + jnp.dot(lo, head_avg, preferred_element_type=F32))

    for c in range(tt // C):
        ts = slice(c * C, (c + 1) * C)
        ss = slice(c * LANES, (c + 1) * LANES)
        ys = []
        for p in range(npair):
            ps = slice(p * LANES, (p + 1) * LANES)
            Hb = h_ref[p].astype(BF16)
            ys.append(jnp.dot(q_ref[ts, ps], Hb, preferred_element_type=F32) + z_ref[ts, ps])
            h_ref[p] = jnp.dot(m_ref[ss, ps], Hb, preferred_element_type=F32) + g_ref[ss, ps].astype(F32)
        for p in range(npair):
            ps = slice(p * LANES, (p + 1) * LANES)
            yc = ys[p] - head_mean(ys[p])
            var = head_mean(yc * yc)
            yn = yc * lax.rsqrt(var + B_GN_EPS) * ln_ref[0:1, ps] + ln_ref[1:2, ps]
            o_ref[ts, ps] = ((yn + bonus_ref[ts, ps]) * gate_ref[ts, ps]).astype(o_ref.dtype)


def wkv_scan(m, g, q, z, gate, bonus, ln, tt=WKV_SCAN_TOKENS):
    S = q.shape[0]
    tt = min(tt, S)
    npair = B_WIDTH // LANES
    nstate_rows = (tt // WKV_CHUNK) * LANES
    tok = pl.BlockSpec((tt, B_WIDTH), lambda i: (i, 0))
    st = pl.BlockSpec((nstate_rows, B_WIDTH), lambda i: (i, 0))
    return pl.pallas_call(
        functools.partial(_wkv_scan_body, tt=tt),
        grid=(S // tt,),
        in_specs=[st, st, tok, tok, tok, tok, pl.BlockSpec((8, B_WIDTH), lambda i: (0, 0))],
        out_specs=tok,
        out_shape=jax.ShapeDtypeStruct((S, B_WIDTH), BF16),
        scratch_shapes=[pltpu.VMEM((npair, LANES, LANES), F32)],
        compiler_params=_cparams(("arbitrary",)),
    )(m, g, q, z, gate, bonus, ln)


def rwkv7_mixer(h, proj_b, p, v_first):
    has_v = v_first is not None
    mu = p["mu_wag"] if not has_v else jnp.concatenate([p["mu_wag"], p["v_mu"][None]], axis=0)
    hids = rwkv_lora(h, mu, p["w1"].astype(BF16), p["a1"].astype(BF16), p["g1"].astype(BF16),
                     p["v1"].astype(BF16) if has_v else None)
    rows = [p["mu_rkv"][0], p["mu_rkv"][1], p["mu_rkv"][2], p["w0"], p["a0"], p["k_k"], p["k_a"],
            p["r_k"].reshape(-1), p["v0"] if has_v else jnp.zeros((B_WIDTH,), F32)]
    vecs = jnp.concatenate([jnp.stack(rows), jnp.zeros((N_VEC_ROWS - len(rows), B_WIDTH), F32)], axis=0)
    w2s = [p["w2"], p["a2"], p["g2"]] + ([p["v2"]] if has_v else [])
    m, g, q, z, v_out, gate, bonus = wkv_pre(proj_b, hids, w2s, vecs, v_first)
    ln = jnp.concatenate([p["ln_w"][None], p["ln_b"][None], jnp.zeros((6, B_WIDTH), F32)], axis=0)
    return wkv_scan(m, g, q, z, gate, bonus, ln), v_out


def kernel(x, norm_mix_g, w_in, lam_q1, lam_k1, lam_q2, lam_k2, diff_subln_g, rw_mu_rkv, rw_mu_wag, rw_w0, rw_w1, rw_w2, rw_a0, rw_a1, rw_a2, rw_g1, rw_g2, rw_k_k, rw_k_a, rw_r_k, rw_ln_w, rw_ln_b, rw_v_mu, rw_v0, rw_v1, rw_v2, w_out, norm_ffn_g, w_up, w_down, norm_final_g):
    Bsz, S, D = x.shape
    depth = w_in.shape[0]
    topk = min(TOPK_MAX, S // 4)
    w_in_t = jnp.swapaxes(w_in, 1, 2)
    w_down_bf = w_down.astype(BF16)
    nq = IDX_HEADS * IDX_HDIM
    n_kw = IDX_HDIM + IDX_HEADS
    w_idx_kw = w_in_t[:, w_in.shape[2] - n_kw:, :]
    outs = []
    for b in range(Bsz):
        xb = x[b]
        v_first = None
        for l in range(depth):
            h = rmsnorm(xb, norm_mix_g[l], BF16)
            oA, oB, oC = 3 * A_WIDTH, 3 * A_WIDTH + 3 * B_WIDTH, 3 * A_WIDTH + 3 * B_WIDTH + 3 * C_WIDTH
            proj_a = matmul_ws(h, w_in_t, l, 0, oA, BF16, w_transposed=True)
            proj_b = matmul_ws(h, w_in_t, l, oA, oB - oA, F32, w_transposed=True)
            proj_c = matmul_ws(h, w_in_t, l, oB, oC - oB, BF16, w_transposed=True)
            qi = matmul_ws(h, w_in_t, l, oC, nq, BF16, w_transposed=True)
            proj_kw = matmul_ws(h, w_idx_kw, l, 0, n_kw, F32, tn=n_kw, w_transposed=True)

            lam_init = 0.8 - 0.6 * math.exp(-0.3 * l)
            lam_vecs = jnp.stack([lam_q1[l], lam_k1[l], lam_q2[l], lam_k2[l]])
            o_a = diff_attention(proj_a, lam_vecs, diff_subln_g[l], lam_init)

            p = dict(mu_rkv=rw_mu_rkv[l], mu_wag=rw_mu_wag[l], w0=rw_w0[l], w1=rw_w1[l], w2=rw_w2[l],
                     a0=rw_a0[l], a1=rw_a1[l], a2=rw_a2[l], g1=rw_g1[l], g2=rw_g2[l], k_k=rw_k_k[l],
                     k_a=rw_k_a[l], r_k=rw_r_k[l], ln_w=rw_ln_w[l], ln_b=rw_ln_b[l])
            if l > 0:
                p.update(v_mu=rw_v_mu[l - 1], v0=rw_v0[l - 1], v1=rw_v1[l - 1], v2=rw_v2[l - 1])
            o_b, v_out = rwkv7_mixer(h, proj_b, p, v_first if l > 0 else None)
            if l == 0:
                v_first = v_out

            ki = proj_kw[:, :IDX_HDIM].astype(BF16)
            wiT = proj_kw[:, IDX_HDIM:].T
            bias = dsa_index(ki, qi, wiT, topk)
            vT = proj_c[:, 2 * C_WIDTH:].T
            o_c = dsa_attention(proj_c, vT, bias).T

            mixed = jnp.concatenate([o_a, o_b, o_c], axis=-1)
            xb = matmul_ws(mixed, w_out, l, 0, D, F32, epilogue="residual", residual=xb)
            h2 = rmsnorm(xb, norm_ffn_g[l], BF16)
            up = matmul_ws(h2, w_up, l, 0, w_up.shape[2], BF16, tn=2 * WS_COLS, tm=WS_ROWS // 2, epilogue="relu2")
            xb = matmul(up, w_down_bf, l, F32, epilogue="residual", residual=xb)
        outs.append(rmsnorm(xb, norm_final_g, F32))
    return jnp.stack(outs)
```

```python
import functools
import math

import jax
import jax.numpy as jnp
from jax import lax
from jax.experimental import pallas as pl
from jax.experimental.pallas import tpu as pltpu

F32 = jnp.float32
BF16 = jnp.bfloat16

A_HEADS, A_HDIM = 8, 64
A_WIDTH = A_HEADS * 2 * A_HDIM
B_HDIM, B_WIDTH = 64, 2048
B_HEADS = B_WIDTH // B_HDIM
B_GN_EPS = 64e-5
C_HEADS, C_HDIM = 8, 128
C_WIDTH = C_HEADS * C_HDIM
IDX_HEADS, IDX_HDIM = 16, 64
TOPK_MAX = 256
EPS = 1e-6

LANES = 128
VMEM_LIMIT = 56 * 1024 * 1024
NEG = -1e30
WKV_CHUNK = 64
SOFTMAX_ROWS = 64

NORM_ROWS = 256
WS_ROWS, WS_COLS = 1024, 512
ATTN_TILE = 512
INDEX_QUERIES, INDEX_KEYS = 512, 512
DSA_QUERIES, DSA_KEYS = 512, 1024
LORA_ROWS = 256
WKV_PRE_TOKENS = 1024
WKV_SCAN_TOKENS = 256


def _cparams(sem):
    return pltpu.CompilerParams(dimension_semantics=sem, vmem_limit_bytes=VMEM_LIMIT)


def _rmsnorm_body(x_ref, g_ref, o_ref):
    x = x_ref[...]
    ms = jnp.mean(x * x, axis=-1, keepdims=True)
    o_ref[...] = (x * lax.rsqrt(ms + EPS) * g_ref[...]).astype(o_ref.dtype)


def rmsnorm(x, g, out_dtype, tm=NORM_ROWS):
    S, D = x.shape
    tm = min(tm, S)
    return pl.pallas_call(
        _rmsnorm_body,
        grid=(S // tm,),
        in_specs=[pl.BlockSpec((tm, D), lambda i: (i, 0)),
                  pl.BlockSpec((1, D), lambda i: (0, 0))],
        out_specs=pl.BlockSpec((tm, D), lambda i: (i, 0)),
        out_shape=jax.ShapeDtypeStruct((S, D), out_dtype),
        compiler_params=_cparams(("parallel",)),
    )(x, g.reshape(1, D))


def _mm_body(a_ref, b_ref, *rest, nk, epilogue):
    if epilogue == "residual":
        res_ref, o_ref, acc_ref = rest
    else:
        o_ref, acc_ref = rest
    k = pl.program_id(2)

    @pl.when(k == 0)
    def _():
        acc_ref[...] = jnp.zeros_like(acc_ref)

    acc_ref[...] += jnp.dot(a_ref[...], b_ref[...], preferred_element_type=F32)

    @pl.when(k == nk - 1)
    def _():
        acc = acc_ref[...]
        if epilogue == "relu2":
            r = jnp.maximum(acc, 0.0)
            acc = r * r
        elif epilogue == "residual":
            acc = acc + res_ref[...]
        o_ref[...] = acc.astype(o_ref.dtype)


def _pick(n, pref):
    for t in pref:
        if n % t == 0:
            return t
    return n


def matmul(a, b3, layer, out_dtype, epilogue="none", residual=None):
    M, K = a.shape
    _, _, N = b3.shape
    tm = _pick(M, (1024, 512, 256))
    tn = _pick(N, (1024, 768, 512, 384, 256, 128))
    tk = _pick(K, (2048, 1024, 512))
    nk = K // tk
    in_specs = [pl.BlockSpec((tm, tk), lambda i, j, k: (i, k)),
                pl.BlockSpec((None, tk, tn), lambda i, j, k: (layer, k, j))]
    args = [a, b3]
    if epilogue == "residual":
        in_specs.append(pl.BlockSpec((tm, tn), lambda i, j, k: (i, j)))
        args.append(residual)
    return pl.pallas_call(
        functools.partial(_mm_body, nk=nk, epilogue=epilogue),
        grid=(M // tm, N // tn, nk),
        in_specs=in_specs,
        out_specs=pl.BlockSpec((tm, tn), lambda i, j, k: (i, j)),
        out_shape=jax.ShapeDtypeStruct((M, N), out_dtype),
        scratch_shapes=[pltpu.VMEM((tm, tn), F32)],
        compiler_params=_cparams(("parallel", "parallel", "arbitrary")),
    )(*args)


def _mm_ws_body(x_ref, w_ref, *rest, epilogue, w_transposed):
    if epilogue == "residual":
        res_ref, o_ref, wb_ref = rest
    else:
        o_ref, wb_ref = rest

    @pl.when(pl.program_id(1) == 0)
    def _():
        wb_ref[...] = w_ref[...].astype(BF16)

    contract = (((1,), (1,)), ((), ())) if w_transposed else (((1,), (0,)), ((), ()))
    acc = lax.dot_general(x_ref[...], wb_ref[...], contract, preferred_element_type=F32)
    if epilogue == "relu2":
        r = jnp.maximum(acc, 0.0)
        acc = r * r
    elif epilogue == "residual":
        acc = acc + res_ref[...]
    o_ref[...] = acc.astype(o_ref.dtype)


def matmul_ws(x, w3, layer, col0, ncols, out_dtype, tn=WS_COLS, epilogue="none", residual=None, tm=WS_ROWS,
              w_transposed=False):
    M, K = x.shape
    tm = min(tm, M)
    nj = ncols // tn
    jb = col0 // tn
    assert col0 % tn == 0 and ncols % tn == 0 and M % tm == 0
    if w_transposed:
        w_spec = pl.BlockSpec((None, tn, K), lambda j, i: (layer, jb + j, 0))
        wb_shape = (tn, K)
    else:
        w_spec = pl.BlockSpec((None, K, tn), lambda j, i: (layer, 0, jb + j))
        wb_shape = (K, tn)
    in_specs = [pl.BlockSpec((tm, K), lambda j, i: (i, 0)), w_spec]
    args = [x, w3]
    if epilogue == "residual":
        in_specs.append(pl.BlockSpec((tm, tn), lambda j, i: (i, j)))
        args.append(residual)
    return pl.pallas_call(
        functools.partial(_mm_ws_body, epilogue=epilogue, w_transposed=w_transposed),
        grid=(nj, M // tm),
        in_specs=in_specs,
        out_specs=pl.BlockSpec((tm, tn), lambda j, i: (i, j)),
        out_shape=jax.ShapeDtypeStruct((M, nj * tn), out_dtype),
        scratch_shapes=[pltpu.VMEM(wb_shape, BF16)],
        compiler_params=_cparams(("parallel", "arbitrary")),
    )(*args)


def _diffattn_body(q_ref, k_ref, v_ref, lam_ref, g_ref, o_ref, m_ref, acc_ref,
                   sa_ref, sb_ref, pa_ref, pb_ref, aa_ref, ab_ref, *, tq, lam_init):
    tk = tq
    i = pl.program_id(1)
    hw = 2 * A_HDIM
    q = q_ref[...] * (A_HDIM ** -0.5)
    lane = lax.broadcasted_iota(jnp.int32, (tq, hw), 1)
    zero = jnp.zeros_like(q)
    qz = jnp.concatenate([jnp.where(lane < A_HDIM, q, zero), jnp.where(lane >= A_HDIM, q, zero)], axis=0)
    m_ref[...] = jnp.full_like(m_ref, NEG)
    acc_ref[...] = jnp.zeros_like(acc_ref)
    ones = jnp.ones((tk, hw), BF16)

    def scores(j):
        k = k_ref[pl.ds(pl.multiple_of(j * tk, tk), tk), :]
        return lax.dot_general(qz, k, (((1,), (1,)), ((), ())), preferred_element_type=F32)

    key_minus_row = (lax.broadcasted_iota(jnp.int32, (SOFTMAX_ROWS, tk), 1)
                     - lax.broadcasted_iota(jnp.int32, (SOFTMAX_ROWS, tk), 0))
    tile_rows = lambda j: pl.ds(pl.multiple_of(j * tk, tk), tk)

    def scores(j, s_out):
        s_out[...] = lax.dot_general(qz, k_ref[tile_rows(j), :], (((1,), (1,)), ((), ())),
                                     preferred_element_type=F32)

    def softmax(j, s_in, p_out, alpha_out, masked):
        for r0 in range(0, 2 * tq, SOFTMAX_ROWS):
            rs = slice(r0, r0 + SOFTMAX_ROWS)
            m_old = m_ref[rs, :]
            if masked:
                causal = key_minus_row <= (i * tq + r0 % tq) - j * tk
                row_max = jnp.max(jnp.where(causal, s_in[rs, :], NEG), axis=-1, keepdims=True)
            else:
                row_max = jnp.max(s_in[rs, :], axis=-1, keepdims=True)
            m_new = jnp.maximum(m_old, row_max)
            m_ref[rs, :] = m_new
            e = jnp.exp(s_in[rs, :] - jnp.concatenate([m_new] * (tk // hw), axis=-1))
            p_out[rs, :] = (jnp.where(causal, e, 0.0) if masked else e).astype(BF16)
            alpha_out[rs, :] = jnp.exp(m_old - m_new)

    def accumulate(j, p_in, alpha_in):
        v_ext = jnp.concatenate([v_ref[tile_rows(j), :], ones], axis=-1)
        alpha = alpha_in[...]
        acc_ref[...] = (jnp.concatenate([alpha, alpha], axis=-1) * acc_ref[...]
                        + jnp.dot(p_in[...], v_ext, preferred_element_type=F32))

    scores(0, sa_ref)
    pb_ref[...] = jnp.zeros_like(pb_ref)
    ab_ref[...] = jnp.ones_like(ab_ref)

    def pair(t, masked):
        j = 2 * t
        accumulate(jnp.maximum(j - 1, 0), pb_ref, ab_ref)
        softmax(j, sa_ref, pa_ref, aa_ref, masked)
        scores(j + 1, sb_ref)
        accumulate(j, pa_ref, aa_ref)
        softmax(j + 1, sb_ref, pb_ref, ab_ref, masked)
        if not masked:
            scores(j + 2, sa_ref)

    nfree = i // 2
    lax.fori_loop(0, nfree, lambda t, c: (pair(t, False), c)[1], 0)

    @pl.when(i % 2 == 0)
    def _():
        accumulate(jnp.maximum(i - 1, 0), pb_ref, ab_ref)
        softmax(i, sa_ref, pa_ref, aa_ref, True)
        accumulate(i, pa_ref, aa_ref)

    @pl.when(i % 2 == 1)
    def _():
        pair(nfree, True)
        accumulate(i, pb_ref, ab_ref)

    lv = lam_ref[...]
    lam = (jnp.exp(jnp.sum(lv[0:1] * lv[1:2], axis=-1, keepdims=True))
           - jnp.exp(jnp.sum(lv[2:3] * lv[3:4], axis=-1, keepdims=True)) + lam_init)
    acc = acc_ref[...]
    on = acc[:, :hw] / acc[:, hw:]
    o = on[:tq] - lam * on[tq:]
    ms = jnp.mean(o * o, axis=-1, keepdims=True)
    o_ref[...] = (o * lax.rsqrt(ms + EPS) * g_ref[...] * (1.0 - lam_init)).astype(o_ref.dtype)


def diff_attention(qkv, lam_vecs, subln_g, lam_init, tq=ATTN_TILE):
    S = qkv.shape[0]
    tq = min(tq, S)
    hw = 2 * A_HDIM
    return pl.pallas_call(
        functools.partial(_diffattn_body, tq=tq, lam_init=lam_init),
        grid=(A_HEADS, S // tq),
        in_specs=[pl.BlockSpec((tq, hw), lambda h, i: (i, h)),
                  pl.BlockSpec((S, hw), lambda h, i: (0, A_HEADS + h)),
                  pl.BlockSpec((S, hw), lambda h, i: (0, 2 * A_HEADS + h)),
                  pl.BlockSpec((4, A_HDIM), lambda h, i: (0, 0)),
                  pl.BlockSpec((1, hw), lambda h, i: (0, 0))],
        out_specs=pl.BlockSpec((tq, hw), lambda h, i: (i, h)),
        out_shape=jax.ShapeDtypeStruct((S, A_WIDTH), BF16),
        scratch_shapes=[pltpu.VMEM((2 * tq, hw), F32), pltpu.VMEM((2 * tq, 2 * hw), F32),
                        pltpu.VMEM((2 * tq, tq), F32), pltpu.VMEM((2 * tq, tq), F32),
                        pltpu.VMEM((2 * tq, tq), BF16), pltpu.VMEM((2 * tq, tq), BF16),
                        pltpu.VMEM((2 * tq, hw), F32), pltpu.VMEM((2 * tq, hw), F32)],
        compiler_params=_cparams(("parallel", "parallel")),
    )(qkv, qkv, qkv, lam_vecs, subln_g.reshape(1, hw))


KEY_NEG_INF = -2139095041


def _float_key(s):
    b = pltpu.bitcast(s, jnp.int32)
    return b ^ ((b >> 31) & jnp.int32(0x7FFFFFFF))


def _dsa_index_body(ki_ref, qi_ref, wiT_ref, bias_ref, key_ref, *, qb, kc, topk, nkc_total):
    pos_bits = (nkc_total * kc).bit_length()
    i = pl.program_id(0)
    nch = ((i + 1) * qb) // kc
    wi = wiT_ref[...] * (IDX_HEADS ** -0.5 * IDX_HDIM ** -0.5)
    qi = qi_ref[...]
    tpos = i * qb + lax.broadcasted_iota(jnp.int32, (kc, qb), 1)
    srow = lax.broadcasted_iota(jnp.int32, (kc, qb), 0)

    def score_chunk(c, carry):
        r0 = pl.multiple_of(c * kc, kc)
        kic = ki_ref[pl.ds(r0, kc), :]
        acc = jnp.zeros((kc, qb), F32)
        for h in range(IDX_HEADS):
            d = lax.dot_general(kic, qi[:, h * IDX_HDIM:(h + 1) * IDX_HDIM],
                                (((1,), (1,)), ((), ())), preferred_element_type=F32)
            acc = acc + jnp.maximum(d, 0.0) * wi[h:h + 1, :]
        acc = jnp.where(r0 + srow <= tpos, acc, -jnp.inf)
        key_ref[pl.ds(r0, kc), :] = _float_key(acc)
        return carry

    lax.fori_loop(0, nch, score_chunk, 0)

    def count(pred):
        def body(c, cnt):
            r0 = pl.multiple_of(c * kc, kc)
            hit = pred(key_ref[pl.ds(r0, kc), :], r0 + srow).astype(jnp.int32)
            return cnt + jnp.sum(hit.reshape(kc // 8, 8, qb), axis=0)
        cnt8 = lax.fori_loop(0, nch, body, jnp.zeros((8, qb), jnp.int32))
        return jnp.sum(cnt8, axis=0, keepdims=True)

    def bit_step(it, carry):
        tau, n_ge = carry
        cand = tau + (jnp.int32(1) << (31 - it))
        cnt = count(lambda key, pos: key >= cand)
        ok = cnt >= topk
        return jnp.where(ok, cand, tau), jnp.where(ok, cnt, n_ge)

    tau, n_ge = lax.fori_loop(0, 32, bit_step, (jnp.full((1, qb), jnp.iinfo(jnp.int32).min, jnp.int32),
                                                jnp.zeros((1, qb), jnp.int32)))
    n_ge = jnp.where(tau <= KEY_NEG_INF, 0, n_ge)
    tau = jnp.maximum(tau, KEY_NEG_INF + 1)

    tie_rounds = (jnp.max(n_ge) > topk).astype(jnp.int32)
    n_gt = lax.fori_loop(0, tie_rounds, lambda _, c: count(lambda key, pos: key > tau), jnp.zeros((1, qb), jnp.int32))
    quota = topk - n_gt

    def pos_step(it, pos_end):
        cand = pos_end + (jnp.int32(1) << (pos_bits - 1 - it))
        n_tie = count(lambda key, pos: (key == tau) & (pos < cand))
        return jnp.where(n_tie <= quota, cand, pos_end)

    pos_all = jnp.int32((1 << pos_bits) - 1)
    pos_end = lax.fori_loop(0, tie_rounds * pos_bits, pos_step,
                            jnp.full((1, qb), 1, jnp.int32) * (pos_all * (1 - tie_rounds)))

    def write_chunk(c, carry):
        r0 = pl.multiple_of(c * kc, kc)
        key = key_ref[pl.ds(r0, kc), :]
        tie_kept = (key == tau) & (r0 + srow < pos_end)
        bias = jnp.where(key > tau, 0.0, jnp.where(tie_kept, 0.0, NEG))
        bias_ref[pl.ds(r0, kc), :] = bias.astype(bias_ref.dtype)
        return carry

    lax.fori_loop(0, nch, write_chunk, 0)

    def fill_chunk(c, carry):
        r0 = pl.multiple_of(c * kc, kc)
        bias_ref[pl.ds(r0, kc), :] = jnp.full((kc, qb), NEG, bias_ref.dtype)
        return carry

    lax.fori_loop(nch, nkc_total, fill_chunk, 0)


def dsa_index(ki, qi, wiT, topk, qb=INDEX_QUERIES, kc=INDEX_KEYS):
    S = ki.shape[0]
    qb = min(qb, S)
    kc = min(kc, qb)
    return pl.pallas_call(
        functools.partial(_dsa_index_body, qb=qb, kc=kc, topk=topk, nkc_total=S // kc),
        grid=(S // qb,),
        in_specs=[pl.BlockSpec((S, IDX_HDIM), lambda i: (0, 0)),
                  pl.BlockSpec((qb, IDX_HEADS * IDX_HDIM), lambda i: (i, 0)),
                  pl.BlockSpec((IDX_HEADS, qb), lambda i: (0, i))],
        out_specs=pl.BlockSpec((S, qb), lambda i: (0, i)),
        out_shape=jax.ShapeDtypeStruct((S, S), BF16),
        scratch_shapes=[pltpu.VMEM((S, qb), jnp.int32)],
        compiler_params=_cparams(("parallel",)),
    )(ki, qi, wiT)


def _dsa_attn_body(qi_ref, kj_ref, q_ref, k_ref, vT_ref, bias_ref, oT_ref, m_ref, l_ref, acc_ref, *, qb, kc):
    step = pl.program_id(0)
    i = qi_ref[step]
    j = kj_ref[step]
    scale = C_HDIM ** -0.5
    heads = [slice(h * C_HDIM, (h + 1) * C_HDIM) for h in range(C_HEADS)]

    @pl.when(j == 0)
    def _():
        m_ref[...] = jnp.full_like(m_ref, NEG)
        l_ref[...] = jnp.zeros_like(l_ref)
        acc_ref[...] = jnp.zeros_like(acc_ref)

    bias = bias_ref[...].astype(F32)
    sT = [lax.dot_general(k_ref[:, cs], q_ref[:, cs], (((1,), (1,)), ((), ())),
                          preferred_element_type=F32) * scale + bias for cs in heads]
    m_old = m_ref[...]
    m_new = jnp.maximum(m_old, jnp.concatenate([jnp.max(s, axis=0, keepdims=True) for s in sT], axis=0))
    alpha = jnp.exp(m_old - m_new)
    p = [jnp.exp(s - m_new[h:h + 1, :]) for h, s in enumerate(sT)]
    l_ref[...] = alpha * l_ref[...] + jnp.concatenate([jnp.sum(x, axis=0, keepdims=True) for x in p], axis=0)
    m_ref[...] = m_new
    for h, cs in enumerate(heads):
        acc_ref[cs, :] = alpha[h:h + 1, :] * acc_ref[cs, :] + jnp.dot(vT_ref[cs, :], p[h].astype(BF16),
                                                                       preferred_element_type=F32)

    @pl.when(j == ((i + 1) * qb - 1) // kc)
    def _():
        l = l_ref[...]
        for h, cs in enumerate(heads):
            oT_ref[cs, :] = (acc_ref[cs, :] / l[h:h + 1, :]).astype(oT_ref.dtype)


def dsa_attention(qkv, vT, bias, qb=DSA_QUERIES, kc=DSA_KEYS):
    S = qkv.shape[0]
    qb = min(qb, S)
    kc = min(kc, S)
    pairs = [(i, j) for i in range(S // qb) for j in range(((i + 1) * qb - 1) // kc + 1)]
    qi = jnp.asarray([p_[0] for p_ in pairs], jnp.int32)
    kj = jnp.asarray([p_[1] for p_ in pairs], jnp.int32)
    grid_spec = pltpu.PrefetchScalarGridSpec(
        num_scalar_prefetch=2,
        grid=(len(pairs),),
        in_specs=[pl.BlockSpec((qb, C_WIDTH), lambda s, qi, kj: (qi[s], 0)),
                  pl.BlockSpec((kc, C_WIDTH), lambda s, qi, kj: (kj[s], 1)),
                  pl.BlockSpec((C_WIDTH, kc), lambda s, qi, kj: (0, kj[s])),
                  pl.BlockSpec((kc, qb), lambda s, qi, kj: (kj[s], qi[s]))],
        out_specs=pl.BlockSpec((C_WIDTH, qb), lambda s, qi, kj: (0, qi[s])),
        scratch_shapes=[pltpu.VMEM((C_HEADS, qb), F32), pltpu.VMEM((C_HEADS, qb), F32),
                        pltpu.VMEM((C_WIDTH, qb), F32)])
    return pl.pallas_call(
        functools.partial(_dsa_attn_body, qb=qb, kc=kc),
        grid_spec=grid_spec,
        out_shape=jax.ShapeDtypeStruct((C_WIDTH, S), BF16),
        compiler_params=_cparams(("arbitrary",)),
    )(qi, kj, qkv, qkv, vT, bias)


def _sigmoid(x):
    return 1.0 / (1.0 + jnp.exp(-x))


def _split_bf16(x, terms):
    parts = []
    for _ in range(terms):
        p = x.astype(BF16)
        parts.append(p)
        x = x - p.astype(F32)
    return parts


def _dot_bf16x3(a, b):
    a_hi, a_lo = _split_bf16(a, 2)
    b_hi, b_lo = _split_bf16(b, 2)
    dot = lambda x, y: jnp.dot(x, y, preferred_element_type=F32)
    return dot(a_hi, b_hi) + (dot(a_hi, b_lo) + dot(a_lo, b_hi))


def _dot_exact_lhs(a, b):
    a = a.astype(BF16)
    b_hi, b_mid, b_lo = _split_bf16(b, 3)
    dot = lambda y: jnp.dot(a, y, preferred_element_type=F32)
    return dot(b_hi) + (dot(b_mid) + dot(b_lo))


def _shift_rows(cur, prev_ref, first_tile):
    prev_row = jnp.where(first_tile, 0.0, prev_ref[7:8, :].astype(F32))
    rolled = pltpu.roll(cur, 1, axis=0)
    row = lax.broadcasted_iota(jnp.int32, cur.shape, 0)
    return jnp.where(row == 0, prev_row, rolled)


def _norm_lora_body(*refs, has_v):
    if has_v:
        (x_ref, xp_ref, g_ref, mu_ref, w1_ref, a1_ref, g1_ref, v1_ref, h_ref, ow_ref, oa_ref, og_ref, ov_ref) = refs
    else:
        (x_ref, xp_ref, g_ref, mu_ref, w1_ref, a1_ref, g1_ref, h_ref, ow_ref, oa_ref, og_ref) = refs
    i = pl.program_id(0)

    def norm(x):
        ms = jnp.mean(x * x, axis=-1, keepdims=True)
        return (x * lax.rsqrt(ms + EPS) * g_ref[...]).astype(BF16)

    hb = norm(x_ref[...])
    h_ref[...] = hb
    h = hb.astype(F32)
    prev_row = jnp.where(i == 0, 0.0, norm(xp_ref[...])[7:8, :].astype(F32))
    row = lax.broadcasted_iota(jnp.int32, h.shape, 0)
    dh = jnp.where(row == 0, prev_row, pltpu.roll(h, 1, axis=0)) - h

    def lora(row, w_ref):
        xm = (h + dh * mu_ref[row:row + 1, :]).astype(BF16)
        return jnp.dot(xm, w_ref[...], preferred_element_type=F32)

    ow_ref[...] = jnp.tanh(lora(0, w1_ref))
    oa_ref[...] = lora(1, a1_ref)
    og_ref[...] = _sigmoid(lora(2, g1_ref))
    if has_v:
        ov_ref[...] = lora(3, v1_ref)


def norm_lora(x, g, mu, w1, a1, g1, v1, tm=LORA_ROWS):
    S, D = x.shape
    tm = min(tm, S)
    has_v = v1 is not None
    ws = [w1, a1, g1] + ([v1] if has_v else [])
    full = lambda a: pl.BlockSpec(a.shape, lambda i: (0, 0))
    outs = pl.pallas_call(
        functools.partial(_norm_lora_body, has_v=has_v),
        grid=(S // tm,),
        in_specs=[pl.BlockSpec((tm, D), lambda i: (i, 0)),
                  pl.BlockSpec((8, D), lambda i: (jnp.maximum(i * (tm // 8) - 1, 0), 0)),
                  pl.BlockSpec((1, D), lambda i: (0, 0)),
                  full(mu)] + [full(w) for w in ws],
        out_specs=[pl.BlockSpec((tm, D), lambda i: (i, 0))]
                  + [pl.BlockSpec((tm, w.shape[1]), lambda i: (i, 0)) for w in ws],
        out_shape=[jax.ShapeDtypeStruct((S, D), BF16)]
                  + [jax.ShapeDtypeStruct((S, w.shape[1]), F32) for w in ws],
        compiler_params=_cparams(("parallel",)),
    )(x, x, g.reshape(1, D), mu, *ws)
    return outs[0], outs[1:]


(V_MU_R, V_MU_K, V_MU_V, V_W0, V_A0, V_KK, V_KA, V_RK, V_V0) = range(9)
N_VEC_ROWS = 16


def _wkv_pre_body(*refs, has_v, tt):
    if has_v:
        (r_ref, k_ref, v_ref, rp_ref, kp_ref, vp_ref, hw_ref, ha_ref, hg_ref, w2_ref, a2_ref, g2_ref,
         vec_ref, hv_ref, v2_ref, vf_ref,
         m_out, g_out, q_out, z_out, vout_ref, gate_ref, bonus_ref) = refs
    else:
        (r_ref, k_ref, v_ref, rp_ref, kp_ref, vp_ref, hw_ref, ha_ref, hg_ref, w2_ref, a2_ref, g2_ref,
         vec_ref,
         m_out, g_out, q_out, z_out, vout_ref, gate_ref, bonus_ref) = refs
    i = pl.program_id(0)
    first = i == 0
    C = WKV_CHUNK
    N = B_HDIM
    vec = lambda row: vec_ref[row:row + 1, :]

    r = r_ref[...]
    k = k_ref[...]
    v = v_ref[...]
    r = r + (_shift_rows(r, rp_ref, first) - r) * vec(V_MU_R)
    k = k + (_shift_rows(k, kp_ref, first) - k) * vec(V_MU_K)
    v = v + (_shift_rows(v, vp_ref, first) - v) * vec(V_MU_V)

    wl = vec(V_W0) + _dot_bf16x3(hw_ref[...], w2_ref[...])
    z = -wl
    softplus = jnp.maximum(z, 0.0) + jnp.log(1.0 + jnp.exp(-jnp.abs(z)))
    logw = -jnp.exp(-softplus - 0.5)
    bdot = lambda a_ref, b_ref: jnp.dot(a_ref[...].astype(BF16), b_ref[...].astype(BF16), preferred_element_type=F32)
    a_sig = _sigmoid(vec(V_A0) + bdot(ha_ref, a2_ref))
    gate = bdot(hg_ref, g2_ref)
    if has_v:
        mix = _sigmoid(vec(V_V0) + bdot(hv_ref, v2_ref))
        v = v + (vf_ref[...] - v) * mix
    vout_ref[...] = v
    gate_ref[...] = gate

    kk = k * vec(V_KK)
    k_new = k * (1.0 + (a_sig - 1.0) * vec(V_KA))
    rk = r * k_new * vec(V_RK)

    rowc = lax.broadcasted_iota(jnp.int32, (C, C), 0)
    colc = lax.broadcasted_iota(jnp.int32, (C, C), 1)
    tril_incl = (rowc >= colc).astype(F32)
    prow = lax.broadcasted_iota(jnp.int32, (2 * C, 2 * C), 0)
    pcol = lax.broadcasted_iota(jnp.int32, (2 * C, 2 * C), 1)
    same_head = (prow >= C) == (pcol >= C)
    dstep = jnp.where(prow >= C, prow - C, prow) - jnp.where(pcol >= C, pcol - C, pcol)
    strict_bd = jnp.where(same_head, dstep, -1) > 0
    incl_bd = jnp.where(same_head, dstep, -1) >= 0
    eye_p = (prow == pcol).astype(F32)
    head_ones = jnp.where(same_head, 1.0, 0.0).astype(BF16)

    def head_sum(x):
        hi = x.astype(BF16)
        lo = (x - hi.astype(F32)).astype(BF16)
        return (jnp.dot(hi, head_ones, preferred_element_type=F32)
                + jnp.dot(lo, head_ones, preferred_element_type=F32))

    kkn = kk / jnp.maximum(jnp.sqrt(head_sum(kk * kk)), 1e-12)
    bonus_ref[...] = head_sum(rk) * v
    a_scan = -kkn
    b_scan = kkn * a_sig

    head0 = lax.broadcasted_iota(jnp.int32, (C, LANES), 1) < N

    def stack2(x):
        return jnp.concatenate([jnp.where(head0, x, 0.0), jnp.where(head0, 0.0, x)], axis=0)

    ts_of = lambda c: slice(c * C, (c + 1) * C)
    st = []
    for c in range(tt // C):
        ts = ts_of(c)
        lw = logw[ts]
        cum = _dot_exact_lhs(tril_incl, lw)
        cum_last = cum[C - 1:C, :]
        e_neg = jnp.exp(-cum)
        e_end = jnp.exp(cum_last - cum)
        st.append(dict(At=stack2(a_scan[ts] * jnp.exp(cum - lw)), Rt=stack2(r[ts] * jnp.exp(cum)),
                       Bt=b_scan[ts] * e_neg, Kt=k_new[ts] * e_neg,
                       Bg=stack2(b_scan[ts] * e_end), Kg=stack2(k_new[ts] * e_end),
                       V=stack2(v[ts]).astype(BF16), gam=jnp.exp(cum_last)))
    for u in st:
        left = jnp.concatenate([u["At"], u["Rt"]], axis=0).astype(BF16)
        right = jnp.concatenate([u["Bt"], u["Bt"], u["Kt"], u["Kt"]], axis=0).astype(BF16)
        AA = lax.dot_general(left, right, (((1,), (1,)), ((), ())), preferred_element_type=F32)
        u["A_ab"] = jnp.where(strict_bd, AA[:2 * C, :2 * C], 0.0)
        u["A_ak"] = jnp.where(strict_bd, AA[:2 * C, 2 * C:], 0.0)
        u["A_r"] = jnp.concatenate([jnp.where(incl_bd, AA[2 * C:, :2 * C], 0.0),
                                    jnp.where(incl_bd, AA[2 * C:, 2 * C:], 0.0)], axis=-1).astype(BF16)
    for u in st:
        akv = jnp.dot(u["A_ak"].astype(BF16), u["V"], preferred_element_type=F32)
        u["XA"] = jnp.concatenate([u["At"] + pltpu.roll(akv, N, axis=1), u["A_ab"]], axis=-1)
    for step in range(6):
        for u in st:
            XA = u["XA"]
            Ap = XA[:, LANES:].astype(BF16)
            if step < 5:
                prod = jnp.dot(Ap, XA.astype(BF16), preferred_element_type=F32)
                u["XA"] = jnp.concatenate([XA[:, :LANES] + prod[:, :LANES], prod[:, LANES:]], axis=-1)
            else:
                u["X"] = XA[:, :LANES] + jnp.dot(Ap, XA[:, :LANES].astype(BF16), preferred_element_type=F32)
    own = (lax.broadcasted_iota(jnp.int32, (2 * C, LANES), 0) >= C) == (lax.broadcasted_iota(jnp.int32, (2 * C, LANES), 1) >= N)
    for u in st:
        p1 = jnp.where(own, u["X"], 0.0).astype(BF16)
        p2 = pltpu.roll(jnp.where(own, 0.0, u["X"]), N, axis=1).astype(BF16)
        upper = jnp.concatenate([p1, p2], axis=-1)
        lower = jnp.concatenate([jnp.zeros((2 * C, LANES), BF16), u["V"]], axis=-1)
        u["W2"] = jnp.concatenate([upper, lower], axis=0)
    for u in st:
        bk = jnp.concatenate([u["Bg"], u["Kg"]], axis=0).astype(BF16)
        u["MG"] = lax.dot_general(bk, u["W2"], (((0,), (0,)), ((), ())), preferred_element_type=F32)
    for u in st:
        u["QZ"] = jnp.dot(u["A_r"], u["W2"], preferred_element_type=F32)
    for c, u in enumerate(st):
        rows = slice(c * LANES, (c + 1) * LANES)
        m_out[rows, :] = (eye_p * u["gam"] + u["MG"][:, :LANES]).astype(m_out.dtype)
        g_out[rows, :] = u["MG"][:, LANES:].astype(g_out.dtype)
        qs = u["Rt"] + u["QZ"][:, :LANES]
        zs = u["QZ"][:, LANES:]
        q_out[ts_of(c), :] = (qs[:C] + qs[C:]).astype(q_out.dtype)
        z_out[ts_of(c), :] = zs[:C] + zs[C:]


def wkv_pre(proj_b, hids, w2s, vecs, v_first, tt=WKV_PRE_TOKENS):
    assert B_HDIM == WKV_CHUNK and 2 * B_HDIM == LANES
    S = proj_b.shape[0]
    tt = min(tt, S)
    has_v = v_first is not None
    nb = B_WIDTH // LANES
    nstate_rows = (tt // WKV_CHUNK) * LANES
    tok = lambda off: pl.BlockSpec((tt, LANES), lambda i, p: (i, off + p))
    prev = lambda off: pl.BlockSpec((8, LANES), lambda i, p: (jnp.maximum(i * (tt // 8) - 1, 0), off + p))
    hid = lambda a: pl.BlockSpec((tt, a.shape[1]), lambda i, p: (i, 0))
    wcol = lambda a: pl.BlockSpec((a.shape[0], LANES), lambda i, p: (0, p))
    in_specs = [tok(0), tok(nb), tok(2 * nb), prev(0), prev(nb), prev(2 * nb),
                hid(hids[0]), hid(hids[1]), hid(hids[2]), wcol(w2s[0]), wcol(w2s[1]), wcol(w2s[2]),
                pl.BlockSpec((N_VEC_ROWS, LANES), lambda i, p: (0, p))]
    args = [proj_b] * 6 + list(hids[:3]) + list(w2s[:3]) + [vecs]
    if has_v:
        in_specs += [hid(hids[3]), wcol(w2s[3]), tok(0)]
        args += [hids[3], w2s[3], v_first]
    state_shape = jax.ShapeDtypeStruct(((S // WKV_CHUNK) * LANES, B_WIDTH), BF16)
    out_tok = pl.BlockSpec((tt, LANES), lambda i, p: (i, p))
    out_st = pl.BlockSpec((nstate_rows, LANES), lambda i, p: (i, p))
    return pl.pallas_call(
        functools.partial(_wkv_pre_body, has_v=has_v, tt=tt),
        grid=(S // tt, nb),
        in_specs=in_specs,
        out_specs=[out_st, out_st, out_tok, out_tok, out_tok, out_tok, out_tok],
        out_shape=[state_shape, state_shape, jax.ShapeDtypeStruct((S, B_WIDTH), BF16)]
                  + [jax.ShapeDtypeStruct((S, B_WIDTH), F32)] * 4,
        compiler_params=_cparams(("parallel", "parallel")),
    )(*args)


def _wkv_scan_body(m_ref, g_ref, q_ref, z_ref, gate_ref, bonus_ref, ln_ref, o_ref, h_ref, *, tt):
    i = pl.program_id(0)
    C = WKV_CHUNK
    N = B_HDIM
    npair = B_WIDTH // LANES

    @pl.when(i == 0)
    def _():
        h_ref[...] = jnp.zeros_like(h_ref)

    row = lax.broadcasted_iota(jnp.int32, (LANES, LANES), 0)
    col = lax.broadcasted_iota(jnp.int32, (LANES, LANES), 1)
    head_avg = jnp.where((row // N) == (col // N), 1.0 / N, 0.0).astype(BF16)

    def head_mean(x):
        hi = x.astype(BF16)
        lo = (x - hi.astype(F32)).astype(BF16)
        return (jnp.dot(hi, head_avg, preferred_element_type=F32)
                + jnp.dot(lo, head_avg, preferred_element_type=F32))

    for c in range(tt // C):
        ts = slice(c * C, (c + 1) * C)
        ss = slice(c * LANES, (c + 1) * LANES)
        ys = []
        for p in range(npair):
            ps = slice(p * LANES, (p + 1) * LANES)
            Hb = h_ref[p].astype(BF16)
            ys.append(jnp.dot(q_ref[ts, ps], Hb, preferred_element_type=F32) + z_ref[ts, ps])
            h_ref[p] = jnp.dot(m_ref[ss, ps], Hb, preferred_element_type=F32) + g_ref[ss, ps].astype(F32)
        for p in range(npair):
            ps = slice(p * LANES, (p + 1) * LANES)
            yc = ys[p] - head_mean(ys[p])
            var = head_mean(yc * yc)
            yn = yc * lax.rsqrt(var + B_GN_EPS) * ln_ref[0:1, ps] + ln_ref[1:2, ps]
            o_ref[ts, ps] = ((yn + bonus_ref[ts, ps]) * gate_ref[ts, ps]).astype(o_ref.dtype)


def wkv_scan(m, g, q, z, gate, bonus, ln, tt=WKV_SCAN_TOKENS):
    S = q.shape[0]
    tt = min(tt, S)
    npair = B_WIDTH // LANES
    nstate_rows = (tt // WKV_CHUNK) * LANES
    tok = pl.BlockSpec((tt, B_WIDTH), lambda i: (i, 0))
    st = pl.BlockSpec((nstate_rows, B_WIDTH), lambda i: (i, 0))
    return pl.pallas_call(
        functools.partial(_wkv_scan_body, tt=tt),
        grid=(S // tt,),
        in_specs=[st, st, tok, tok, tok, tok, pl.BlockSpec((8, B_WIDTH), lambda i: (0, 0))],
        out_specs=tok,
        out_shape=jax.ShapeDtypeStruct((S, B_WIDTH), BF16),
        scratch_shapes=[pltpu.VMEM((npair, LANES, LANES), F32)],
        compiler_params=_cparams(("arbitrary",)),
    )(m, g, q, z, gate, bonus, ln)


def rwkv7_mixer(hids, proj_b, p, v_first):
    has_v = v_first is not None
    rows = [p["mu_rkv"][0], p["mu_rkv"][1], p["mu_rkv"][2], p["w0"], p["a0"], p["k_k"], p["k_a"],
            p["r_k"].reshape(-1), p["v0"] if has_v else jnp.zeros((B_WIDTH,), F32)]
    vecs = jnp.concatenate([jnp.stack(rows), jnp.zeros((N_VEC_ROWS - len(rows), B_WIDTH), F32)], axis=0)
    w2s = [p["w2"], p["a2"], p["g2"]] + ([p["v2"]] if has_v else [])
    m, g, q, z, v_out, gate, bonus = wkv_pre(proj_b, hids, w2s, vecs, v_first)
    ln = jnp.concatenate([p["ln_w"][None], p["ln_b"][None], jnp.zeros((6, B_WIDTH), F32)], axis=0)
    return wkv_scan(m, g, q, z, gate, bonus, ln), v_out


def kernel(x, norm_mix_g, w_in, lam_q1, lam_k1, lam_q2, lam_k2, diff_subln_g, rw_mu_rkv, rw_mu_wag, rw_w0, rw_w1, rw_w2, rw_a0, rw_a1, rw_a2, rw_g1, rw_g2, rw_k_k, rw_k_a, rw_r_k, rw_ln_w, rw_ln_b, rw_v_mu, rw_v0, rw_v1, rw_v2, w_out, norm_ffn_g, w_up, w_down, norm_final_g):
    Bsz, S, D = x.shape
    depth = w_in.shape[0]
    topk = min(TOPK_MAX, S // 4)
    w_in_t = jnp.swapaxes(w_in, 1, 2)
    w_down_bf = w_down.astype(BF16)
    nq = IDX_HEADS * IDX_HDIM
    n_kw = IDX_HDIM + IDX_HEADS
    w_idx_kw = w_in_t[:, w_in.shape[2] - n_kw:, :]
    outs = []
    for b in range(Bsz):
        xb = x[b]
        v_first = None
        for l in range(depth):
            mu = rw_mu_wag[l] if l == 0 else jnp.concatenate([rw_mu_wag[l], rw_v_mu[l - 1][None]], axis=0)
            h, hids = norm_lora(xb, norm_mix_g[l], mu, rw_w1[l].astype(BF16), rw_a1[l].astype(BF16),
                                rw_g1[l].astype(BF16), rw_v1[l - 1].astype(BF16) if l > 0 else None)
            oA, oB, oC = 3 * A_WIDTH, 3 * A_WIDTH + 3 * B_WIDTH, 3 * A_WIDTH + 3 * B_WIDTH + 3 * C_WIDTH
            proj_a = matmul_ws(h, w_in_t, l, 0, oA, BF16, w_transposed=True)
            proj_b = matmul_ws(h, w_in_t, l, oA, oB - oA, F32, w_transposed=True)
            proj_c = matmul_ws(h, w_in_t, l, oB, oC - oB, BF16, w_transposed=True)
            qi = matmul_ws(h, w_in_t, l, oC, nq, BF16, w_transposed=True)
            proj_kw = matmul_ws(h, w_idx_kw, l, 0, n_kw, F32, tn=n_kw, w_transposed=True)

            lam_init = 0.8 - 0.6 * math.exp(-0.3 * l)
            lam_vecs = jnp.stack([lam_q1[l], lam_k1[l], lam_q2[l], lam_k2[l]])
            o_a = diff_attention(proj_a, lam_vecs, diff_subln_g[l], lam_init)

            p = dict(mu_rkv=rw_mu_rkv[l], w0=rw_w0[l], w2=rw_w2[l], a0=rw_a0[l], a2=rw_a2[l], g2=rw_g2[l],
                     k_k=rw_k_k[l], k_a=rw_k_a[l], r_k=rw_r_k[l], ln_w=rw_ln_w[l], ln_b=rw_ln_b[l])
            if l > 0:
                p.update(v0=rw_v0[l - 1], v2=rw_v2[l - 1])
            o_b, v_out = rwkv7_mixer(hids, proj_b, p, v_first if l > 0 else None)
            if l == 0:
                v_first = v_out

            ki = proj_kw[:, :IDX_HDIM].astype(BF16)
            wiT = proj_kw[:, IDX_HDIM:].T
            bias = dsa_index(ki, qi, wiT, topk)
            vT = proj_c[:, 2 * C_WIDTH:].T
            o_c = dsa_attention(proj_c, vT, bias).T

            mixed = jnp.concatenate([o_a, o_b, o_c], axis=-1)
            xb = matmul_ws(mixed, w_out, l, 0, D, F32, epilogue="residual", residual=xb)
            h2 = rmsnorm(xb, norm_ffn_g[l], BF16)
            up = matmul_ws(h2, w_up, l, 0, w_up.shape[2], BF16, tn=2 * WS_COLS, tm=WS_ROWS // 2, epilogue="relu2")
            xb = matmul(up, w_down_bf, l, F32, epilogue="residual", residual=xb)
        outs.append(rmsnorm(xb, norm_final_g, F32))
    return jnp.stack(outs)
```
